```python
import jax
import jax.numpy as jnp
from jax import lax
import numpy as np

D_MODEL = 2048
BATCH = 2
SEQ = 16384
DEPTH = 1
DEC_BATCH = 16
DEC_SEQ = 64
PAST_LEN = 1024

CHUNK = 64
MIX_WIDTH = D_MODEL
SB_HEADS = 8
SB_HEAD_DIM = 128
SB_WIDTH = SB_HEADS * SB_HEAD_DIM
ML_HEADS = 4
ML_V_DIM = (MIX_WIDTH - SB_WIDTH) // ML_HEADS
ML_QK_DIM = ML_V_DIM // 2
ML_WIDTH = ML_HEADS * ML_V_DIM
ML_QK_WIDTH = ML_HEADS * ML_QK_DIM
Q_BLOCK = 128
N_EXPERTS = 32
TOP_K = 4
D_FF = D_MODEL
SWIGLU_ALPHA = 1.702
SWIGLU_LIMIT = 7.0
MOE_BLOCK = 128
EPS = 1e-6
IN_SPLITS = (SB_WIDTH, 2 * SB_WIDTH, 3 * SB_WIDTH,
             3 * SB_WIDTH + ML_QK_WIDTH,
             3 * SB_WIDTH + 2 * ML_QK_WIDTH,
             3 * SB_WIDTH + 2 * ML_QK_WIDTH + ML_WIDTH,
             3 * SB_WIDTH + 2 * ML_QK_WIDTH + 2 * ML_WIDTH,
             3 * SB_WIDTH + 2 * ML_QK_WIDTH + 2 * ML_WIDTH + ML_HEADS)
IN_WIDTH = 3 * SB_WIDTH + 2 * ML_QK_WIDTH + 2 * ML_WIDTH + 2 * ML_HEADS

kernel_name = 'hybrid_stickbreak_mlstm_moe_stream_step'


def rms_norm(x, g):
    xf = x.astype(jnp.float32)
    y = xf * lax.rsqrt(jnp.mean(xf * xf, axis=-1, keepdims=True) + EPS)
    return (y * g.astype(jnp.float32)).astype(x.dtype)


def head_rms_norm(a, g):
    H, d = a.shape[-2:]
    a = a * lax.rsqrt(jnp.mean(a * a, axis=-1, keepdims=True) + EPS)
    return (a * g.reshape(H, d).astype(jnp.float32)).reshape(a.shape[:-2] + (H * d,))


def stick_breaking_attention(q, k, v, q_offset):
    B, Sq, H, Dh = q.shape
    Sk = k.shape[1]
    qb = min(Sq, Q_BLOCK)
    nb = Sq // qb
    kpos = jnp.arange(Sk)
    scale = Dh ** -0.5
    vf = v.astype(jnp.float32)

    def block(i):
        start = i * qb
        q_blk = lax.dynamic_slice_in_dim(q, start, qb, axis=1)
        z = jnp.einsum('bqhd,bkhd->bhqk', q_blk, k, preferred_element_type=jnp.float32) * scale
        qpos = q_offset + start + jnp.arange(qb)
        mask = kpos[None, :] < qpos[:, None]
        log_beta = jax.nn.log_sigmoid(z)
        log_stay = jnp.where(mask, log_beta - z, 0.0)
        suffix = lax.cumsum(log_stay, axis=3, reverse=True) - log_stay
        wts = jnp.where(mask, jnp.exp(log_beta + suffix), 0.0)
        return jnp.einsum('bhqk,bkhd->bqhd', wts, vf)

    out = lax.map(block, jnp.arange(nb))
    return out.swapaxes(0, 1).reshape(B, Sq, H, Dh)


def mlstm_chunk(q, k, v, ig, lf, C, n, m):
    L = q.shape[1]
    b = jnp.cumsum(lf, axis=1).transpose(0, 2, 1)
    i_t = ig.transpose(0, 2, 1)
    causal = jnp.tril(jnp.ones((L, L), dtype=bool))
    log_d = jnp.where(causal, b[..., :, None] - b[..., None, :] + i_t[..., None, :], -jnp.inf)
    log_inter = b + m[..., None]
    m_row = jnp.maximum(log_inter, jnp.max(log_d, axis=-1))
    w = jnp.einsum('blhd,bshd->bhls', q, k) * jnp.exp(log_d - m_row[..., None])
    inter = jnp.exp(log_inter - m_row)
    num = jnp.einsum('bhls,bshv->bhlv', w, v) + inter[..., None] * jnp.einsum('blhd,bhdv->bhlv', q, C)
    den = jnp.sum(w, axis=-1) + inter * jnp.einsum('blhd,bhd->bhl', q, n)
    h = num / jnp.maximum(jnp.abs(den), jnp.exp(-m_row))[..., None]
    b_last = b[..., -1]
    log_w = b_last[..., None] - b + i_t
    m_new = jnp.maximum(b_last + m, jnp.max(log_w, axis=-1))
    wk = jnp.exp(log_w - m_new[..., None])
    decay = jnp.exp(b_last + m - m_new)
    C_new = decay[..., None, None] * C + jnp.einsum('bhs,bshd,bshv->bhdv', wk, k, v)
    n_new = decay[..., None] * n + jnp.einsum('bhs,bshd->bhd', wk, k)
    return h.transpose(0, 2, 1, 3), (C_new, n_new, m_new)


def mlstm_scan(q, k, v, ig, lf, C0, n0, m0):
    B, S = q.shape[:2]
    L = min(S, CHUNK)
    nc = S // L

    def chunks(a):
        return a.reshape((B, nc, L) + a.shape[2:]).swapaxes(0, 1)

    def step(carry, inp):
        h, carry = mlstm_chunk(*inp, *carry)
        return carry, h

    state0 = (C0.astype(jnp.float32), n0.astype(jnp.float32), m0.astype(jnp.float32))
    state, h = lax.scan(step, state0, (chunks(q), chunks(k), chunks(v), chunks(ig), chunks(lf)))
    return h.swapaxes(0, 1).reshape(B, S, ML_HEADS, ML_V_DIM), state


def moe_ffn(x2d, w_router, b_router, w_gate_up, b_gate_up, w_down, b_down):
    T = x2d.shape[0]
    logits = (x2d @ w_router).astype(jnp.float32) + b_router.astype(jnp.float32)
    top_v, top_e = lax.top_k(logits, TOP_K)
    gates = jax.nn.softmax(top_v, axis=-1)
    A = T * TOP_K
    e_flat = top_e.reshape(A)
    g_flat = gates.reshape(A)
    tok_flat = jnp.arange(A, dtype=jnp.int32) // TOP_K
    order = jnp.argsort(e_flat, stable=True)
    e_sorted = e_flat[order]
    counts = jnp.bincount(e_flat, length=N_EXPERTS)
    padded = (counts + MOE_BLOCK - 1) // MOE_BLOCK * MOE_BLOCK
    start = jnp.cumsum(counts) - counts
    pend = jnp.cumsum(padded)
    pstart = pend - padded
    dest = pstart[e_sorted] + (jnp.arange(A) - start[e_sorted])
    n_blocks = -(-A // MOE_BLOCK) + N_EXPERTS
    n_rows = n_blocks * MOE_BLOCK
    row_tok = jnp.zeros((n_rows,), jnp.int32).at[dest].set(tok_flat[order])
    row_gate = jnp.zeros((n_rows,), jnp.float32).at[dest].set(g_flat[order])
    block_e = jnp.minimum(jnp.searchsorted(pend, jnp.arange(n_blocks) * MOE_BLOCK, side='right'),
                          N_EXPERTS - 1).astype(jnp.int32)

    def step(acc, inp):
        tok, gate, e = inp
        xb = x2d[tok]
        hu = xb @ w_gate_up[e] + b_gate_up[e]
        glu = jnp.minimum(hu[:, :D_FF], SWIGLU_LIMIT)
        lin = jnp.clip(hu[:, D_FF:], -SWIGLU_LIMIT, SWIGLU_LIMIT)
        act = glu * jax.nn.sigmoid(SWIGLU_ALPHA * glu) * (lin + 1.0)
        out = act @ w_down[e] + b_down[e]
        acc = acc.at[tok].add(out.astype(jnp.float32) * gate[:, None])
        return acc, None

    acc0 = jnp.zeros((T, x2d.shape[1]), jnp.float32)
    y, _ = lax.scan(step, acc0, (row_tok.reshape(n_blocks, MOE_BLOCK),
                                 row_gate.reshape(n_blocks, MOE_BLOCK), block_e))
    return y.astype(x2d.dtype)


def hybrid_layer(x, past, norm_mix_g, w_in, b_igate, b_fgate, g_sb_out, g_ml_out, w_out,
                 norm_ffn_g, w_router, b_router, w_gate_up, b_gate_up, w_down, b_down):
    B, S, _ = x.shape
    xn = rms_norm(x, norm_mix_g)
    proj = xn @ w_in
    q_sb, k_sb, v_sb, q_ml, k_ml, v_ml, o_ml, i_ml, f_ml = jnp.split(proj, list(IN_SPLITS), axis=-1)
    q_sb = q_sb.reshape(B, S, SB_HEADS, SB_HEAD_DIM)
    k_sb = k_sb.reshape(B, S, SB_HEADS, SB_HEAD_DIM)
    v_sb = v_sb.reshape(B, S, SB_HEADS, SB_HEAD_DIM)
    q_ml = q_ml.reshape(B, S, ML_HEADS, ML_QK_DIM).astype(jnp.float32)
    k_ml = k_ml.reshape(B, S, ML_HEADS, ML_QK_DIM).astype(jnp.float32) * (ML_QK_DIM ** -0.5)
    v_ml = v_ml.reshape(B, S, ML_HEADS, ML_V_DIM).astype(jnp.float32)
    i_pre = (i_ml + b_igate).astype(jnp.float32)
    log_f = jax.nn.log_sigmoid((f_ml + b_fgate).astype(jnp.float32))
    if past is None:
        k_all, v_all, q_offset = k_sb, v_sb, 0
        C0 = jnp.zeros((B, ML_HEADS, ML_QK_DIM, ML_V_DIM), jnp.float32)
        n0 = jnp.zeros((B, ML_HEADS, ML_QK_DIM), jnp.float32)
        m0 = jnp.zeros((B, ML_HEADS), jnp.float32)
    else:
        k_cache, v_cache, C0, n0, m0 = past
        q_offset = k_cache.shape[1]
        k_all = jnp.concatenate([k_cache, k_sb.astype(k_cache.dtype)], axis=1)
        v_all = jnp.concatenate([v_cache, v_sb.astype(v_cache.dtype)], axis=1)
    sb = stick_breaking_attention(q_sb, k_all, v_all, q_offset)
    h_ml, (C, n, m) = mlstm_scan(q_ml, k_ml, v_ml, i_pre, log_f, C0, n0, m0)
    sb_o = head_rms_norm(sb, g_sb_out)
    ml_o = jax.nn.sigmoid(o_ml.astype(jnp.float32)) * head_rms_norm(h_ml, g_ml_out)
    mix = jnp.concatenate([sb_o, ml_o], axis=-1).astype(x.dtype)
    x = x + mix @ w_out
    xn2 = rms_norm(x, norm_ffn_g)
    ffn = moe_ffn(xn2.reshape(B * S, D_MODEL), w_router, b_router, w_gate_up, b_gate_up, w_down, b_down)
    x = x + ffn.reshape(B, S, D_MODEL).astype(x.dtype)
    return x, (k_sb, v_sb, C, n, m)


def setup_inputs(seed: int = 0) -> dict:
    key = jax.random.key(seed)
    ks = jax.random.split(key, 24)

    def nrm(k, shape, scale):
        return jax.random.normal(k, shape, jnp.float32) * scale

    b_f_base = jnp.linspace(3.0, 6.0, ML_HEADS, dtype=jnp.float32)
    return {
        'x_prompt': nrm(ks[0], (BATCH, SEQ, D_MODEL), 1.0),
        'x_sample': nrm(ks[1], (DEC_BATCH, DEC_SEQ, D_MODEL), 1.0),
        'cache_k': nrm(ks[2], (DEPTH, DEC_BATCH, PAST_LEN, SB_HEADS, SB_HEAD_DIM), 1.0),
        'cache_v': nrm(ks[3], (DEPTH, DEC_BATCH, PAST_LEN, SB_HEADS, SB_HEAD_DIM), 1.0),
        'state_C': nrm(ks[4], (DEPTH, DEC_BATCH, ML_HEADS, ML_QK_DIM, ML_V_DIM), 0.5),
        'state_n': nrm(ks[5], (DEPTH, DEC_BATCH, ML_HEADS, ML_QK_DIM), 0.5),
        'state_m': nrm(ks[6], (DEPTH, DEC_BATCH, ML_HEADS), 1.0),
        'norm_mix_g': 1.0 + nrm(ks[7], (DEPTH, D_MODEL), 0.01),
        'w_in': nrm(ks[8], (DEPTH, D_MODEL, IN_WIDTH), D_MODEL ** -0.5),
        'b_igate': nrm(ks[9], (DEPTH, ML_HEADS), 0.1),
        'b_fgate': b_f_base + nrm(ks[10], (DEPTH, ML_HEADS), 0.1),
        'g_sb_out': 1.0 + nrm(ks[11], (DEPTH, SB_WIDTH), 0.01),
        'g_ml_out': 1.0 + nrm(ks[12], (DEPTH, ML_WIDTH), 0.01),
        'w_out': nrm(ks[13], (DEPTH, MIX_WIDTH, D_MODEL), MIX_WIDTH ** -0.5),
        'norm_ffn_g': 1.0 + nrm(ks[14], (DEPTH, D_MODEL), 0.01),
        'w_router': nrm(ks[15], (DEPTH, D_MODEL, N_EXPERTS), D_MODEL ** -0.5),
        'b_router': nrm(ks[16], (DEPTH, N_EXPERTS), 0.01),
        'w_gate_up': nrm(ks[17], (DEPTH, N_EXPERTS, D_MODEL, 2 * D_FF), D_MODEL ** -0.5),
        'b_gate_up': nrm(ks[18], (DEPTH, N_EXPERTS, 2 * D_FF), 0.02),
        'w_down': nrm(ks[19], (DEPTH, N_EXPERTS, D_FF, D_MODEL), D_FF ** -0.5),
        'b_down': nrm(ks[20], (DEPTH, N_EXPERTS, D_MODEL), 0.02),
        'final_norm_g': 1.0 + nrm(ks[21], (D_MODEL,), 0.01),
    }


def reference(x_prompt, x_sample, cache_k, cache_v, state_C, state_n, state_m,
              norm_mix_g, w_in, b_igate, b_fgate, g_sb_out, g_ml_out, w_out,
              norm_ffn_g, w_router, b_router, w_gate_up, b_gate_up, w_down, b_down, final_norm_g):
    yp, ys = x_prompt, x_sample
    new_p, new_s = [], []
    for l in range(DEPTH):
        lw = (norm_mix_g[l], w_in[l], b_igate[l], b_fgate[l], g_sb_out[l], g_ml_out[l], w_out[l],
              norm_ffn_g[l], w_router[l], b_router[l], w_gate_up[l], b_gate_up[l], w_down[l], b_down[l])
        yp, st_p = hybrid_layer(yp, None, *lw)
        ys, st_s = hybrid_layer(ys, (cache_k[l], cache_v[l], state_C[l], state_n[l], state_m[l]), *lw)
        new_p.append(st_p)
        new_s.append(st_s)

    def stack(sts, j):
        return jnp.stack([s[j] for s in sts], axis=0)

    y_prompt = rms_norm(yp, final_norm_g)
    y_sample = rms_norm(ys, final_norm_g)
    return (y_prompt, y_sample,
            stack(new_p, 0), stack(new_p, 1), stack(new_p, 2), stack(new_p, 3), stack(new_p, 4),
            stack(new_s, 0), stack(new_s, 1), stack(new_s, 2), stack(new_s, 3), stack(new_s, 4))
```

```python
import functools

import jax
import jax.numpy as jnp
from jax import lax
from jax.experimental import pallas as pl
from jax.experimental.pallas import tpu as pltpu

F32 = jnp.float32
BF16 = jnp.bfloat16
I32 = jnp.int32

EPS = 1e-6
SB_HEADS = 8
HEAD_DIM = 128
ML_HEADS = 4
ML_V_DIM = 256
N_EXPERTS = 32
TOP_K = 4
SWIGLU_ALPHA = 1.702
SWIGLU_LIMIT = 7.0
LANES = 128
KEY_BLOCK = 128
NEG_BIG = -1e30
EXP_ZERO_BELOW = -105.0
VMEM_LIMIT = 56 * 1024 * 1024


def _cparams(*sem):
    return pltpu.CompilerParams(dimension_semantics=sem, vmem_limit_bytes=VMEM_LIMIT)


def _resident(shape):
    nd = len(shape)
    return pl.BlockSpec(shape, lambda *_: (0,) * nd, pipeline_mode=pl.Buffered(1))


def _rms(x, g):
    return x * lax.rsqrt(jnp.mean(x * x, axis=-1, keepdims=True) + EPS) * g


def _log_sigmoid(z):
    return jnp.minimum(z, 0.0) - jnp.log(1.0 + jnp.exp(-jnp.abs(z)))


def _split2(x):
    hi = x.astype(BF16)
    lo = (x - hi.astype(F32)).astype(BF16)
    return hi, lo


def _split3(x):
    h1 = x.astype(BF16)
    r = x - h1.astype(F32)
    h2 = r.astype(BF16)
    h3 = (r - h2.astype(F32)).astype(BF16)
    return h1, h2, h3


def _mm(a, b):
    return jnp.dot(a, b, preferred_element_type=F32)


def _inproj_kernel(x_ref, g_ref, wq_ref, wk_ref, wv_ref, wqm_ref, wkm_ref, wvm_ref, wom_ref,
                   wg_ref, bg_ref,
                   q_ref, kf_ref, kb_ref, vf_ref, vb_ref, qm_ref, km_ref, vm_ref, om_ref, gt_ref):
    xn = _rms(x_ref[...], g_ref[...]).astype(BF16)
    q_ref[...] = (_mm(xn, wq_ref[...]) * (HEAD_DIM ** -0.5)).astype(BF16)
    k = _mm(xn, wk_ref[...])
    kf_ref[...] = k
    kb_ref[...] = k.astype(BF16)
    v = _mm(xn, wv_ref[...])
    vf_ref[...] = v
    vb_ref[...] = v.astype(BF16)
    qm_ref[...] = _mm(xn, wqm_ref[...]).astype(BF16)
    km_ref[...] = (_mm(xn, wkm_ref[...]) * (HEAD_DIM ** -0.5)).astype(BF16)
    vm_ref[...] = _mm(xn, wvm_ref[...]).astype(BF16)
    om_ref[...] = _mm(xn, wom_ref[...])
    gpre = _mm(xn, wg_ref[...]) + bg_ref[...]
    lane = lax.broadcasted_iota(I32, gpre.shape, 1)
    is_f = (lane >= ML_HEADS) & (lane < 2 * ML_HEADS)
    gt_ref[...] = jnp.where(is_f, _log_sigmoid(gpre), gpre)


def _in_projection(x2d, norm_g, w_in, b_igate, b_fgate, tm):
    T, D = x2d.shape
    sbw = SB_HEADS * HEAD_DIM
    mqk = ML_HEADS * HEAD_DIM
    mlw = ML_HEADS * ML_V_DIM
    o = 0
    ws = []
    for width in (sbw, sbw, sbw, mqk, mqk, mlw, mlw):
        ws.append(w_in[:, o:o + width].astype(BF16))
        o += width
    wg = jnp.zeros((D, LANES), F32).at[:, :2 * ML_HEADS].set(w_in[:, o:o + 2 * ML_HEADS]).astype(BF16)
    bg = jnp.zeros((1, LANES), F32).at[0, :ML_HEADS].set(b_igate).at[0, ML_HEADS:2 * ML_HEADS].set(b_fgate)
    row = lambda w: pl.BlockSpec((tm, w), lambda i: (i, 0))
    out_widths = (sbw, sbw, sbw, sbw, sbw, mqk, mqk, mlw, mlw, LANES)
    out_dtypes = (BF16, F32, BF16, F32, BF16, BF16, BF16, BF16, F32, F32)
    return pl.pallas_call(
        _inproj_kernel,
        grid=(T // tm,),
        in_specs=[row(D), _resident((1, D))] + [_resident(w.shape) for w in ws]
                 + [_resident(wg.shape), _resident(bg.shape)],
        out_specs=[row(w) for w in out_widths],
        out_shape=[jax.ShapeDtypeStruct((T, w), dt) for w, dt in zip(out_widths, out_dtypes)],
        compiler_params=_cparams("parallel"),
        name="in_projection",
    )(x2d, norm_g.reshape(1, D), *ws, wg, bg)


def _suffix_matrix():
    j = lax.broadcasted_iota(I32, (KEY_BLOCK, 2 * KEY_BLOCK), 0)
    c = lax.broadcasted_iota(I32, (KEY_BLOCK, 2 * KEY_BLOCK), 1)
    return jnp.where((c >= KEY_BLOCK) | (j > c), 1.0, 0.0).astype(BF16)


def _sb_step(q, k, v, carry, umat, mask):
    R, bq, _ = q.shape
    z = jnp.einsum("rqd,rkd->rqk", q, k, preferred_element_type=F32)
    lp = jnp.log(1.0 + jnp.exp(-jnp.abs(z)))
    log_beta = jnp.minimum(z, 0.0) - lp
    log_stay = log_beta - z
    if mask is not None:
        log_stay = jnp.where(mask, log_stay, 0.0)
    hi, lo = _split2(log_stay)
    st = _mm(hi.reshape(R * bq, KEY_BLOCK), umat) + _mm(lo.reshape(R * bq, KEY_BLOCK), umat)
    st = st.reshape(R, bq, 2 * KEY_BLOCK)
    w = jnp.exp(log_beta + st[:, :, :KEY_BLOCK] + carry)
    if mask is not None:
        w = jnp.where(mask, w, 0.0)
    pv = jnp.einsum("rqk,rkd->rqd", w.astype(BF16), v, preferred_element_type=F32)
    return pv, carry + st[:, :, KEY_BLOCK:]


def _sb_prompt_kernel(q_ref, k_ref, v_ref, g_ref, o_ref, acc_ref, carry_ref, *, R):
    qi = pl.program_id(2)
    blk0 = qi * R
    q = q_ref[...].reshape(R, KEY_BLOCK, HEAD_DIM)
    umat = _suffix_matrix()
    shape3 = (R, KEY_BLOCK, KEY_BLOCK)
    t_io = lax.broadcasted_iota(I32, shape3, 1)
    s_io = lax.broadcasted_iota(I32, shape3, 2)
    r_io = lax.broadcasted_iota(I32, shape3, 0)

    def load(ref, d):
        return jnp.stack([ref[pl.ds(pl.multiple_of(jnp.maximum(blk0 + r - d, 0) * KEY_BLOCK, KEY_BLOCK),
                                    KEY_BLOCK), :] for r in range(R)])

    def penalty(d_next):
        return jnp.where(r_io < d_next - blk0, NEG_BIG, 0.0)

    pv, carry = _sb_step(q, load(k_ref, 0), load(v_ref, 0), jnp.zeros(shape3, F32), umat, s_io < t_io)
    acc_ref[...] = pv
    carry = carry + penalty(1)
    carry_ref[...] = carry

    def cond(state):
        d, mx = state
        return (d < blk0 + R) & (mx > EXP_ZERO_BELOW)

    def body(state):
        d, _ = state
        pv, carry = _sb_step(q, load(k_ref, d), load(v_ref, d), carry_ref[...], umat, None)
        acc_ref[...] += pv
        carry = carry + penalty(d + 1)
        carry_ref[...] = carry
        return d + 1, jnp.max(carry)

    lax.while_loop(cond, body, (jnp.int32(1), jnp.max(carry)))
    a = acc_ref[...]
    out = a * lax.rsqrt(jnp.mean(a * a, axis=-1, keepdims=True) + EPS) * g_ref[...]
    o_ref[...] = out.reshape(R * KEY_BLOCK, HEAD_DIM).astype(BF16)


def _sb_prompt(q, k, v, g_sb, B, S, R):
    tq = R * KEY_BLOCK
    nq = S // tq
    return pl.pallas_call(
        functools.partial(_sb_prompt_kernel, R=R),
        grid=(B, SB_HEADS, nq),
        in_specs=[pl.BlockSpec((tq, HEAD_DIM), lambda b, h, i: (b * nq + i, h)),
                  pl.BlockSpec((S, HEAD_DIM), lambda b, h, i: (b, h)),
                  pl.BlockSpec((S, HEAD_DIM), lambda b, h, i: (b, h)),
                  pl.BlockSpec((1, HEAD_DIM), lambda b, h, i: (0, h))],
        out_specs=pl.BlockSpec((tq, HEAD_DIM), lambda b, h, i: (b * nq + i, h)),
        out_shape=jax.ShapeDtypeStruct((B * S, SB_HEADS * HEAD_DIM), BF16),
        scratch_shapes=[pltpu.VMEM((R, KEY_BLOCK, HEAD_DIM), F32),
                        pltpu.VMEM((R, KEY_BLOCK, KEY_BLOCK), F32)],
        compiler_params=_cparams("parallel", "parallel", "parallel"),
        name="sb_prompt",
    )(q, k, v, g_sb.reshape(1, -1))


def _sb_sample_kernel(q_ref, kn_ref, vn_ref, ck_ref, cv_ref, g_ref, o_ref, acc_ref, carry_ref, *, S, P):
    H = SB_HEADS
    hs = lambda h: slice(h * HEAD_DIM, (h + 1) * HEAD_DIM)
    umat = _suffix_matrix()
    q = jnp.stack([q_ref[:, hs(h)] for h in range(H)])
    pad = jnp.zeros((KEY_BLOCK - S, HEAD_DIM), BF16)

    def new_keys(ref):
        return jnp.stack([jnp.concatenate([ref[:, hs(h)], pad], axis=0) for h in range(H)])

    def past_keys(ref, j):
        start = pl.multiple_of(j * KEY_BLOCK, KEY_BLOCK)
        return jnp.stack([ref[pl.ds(start, KEY_BLOCK), hs(h)].astype(BF16) for h in range(H)])

    shape3 = (H, S, KEY_BLOCK)
    t_io = lax.broadcasted_iota(I32, shape3, 1)
    s_io = lax.broadcasted_iota(I32, shape3, 2)
    pv, carry = _sb_step(q, new_keys(kn_ref), new_keys(vn_ref), jnp.zeros(shape3, F32), umat, s_io < t_io)
    acc_ref[...] = pv
    carry_ref[...] = carry

    def cond(state):
        j, mx = state
        return (j >= 0) & (mx > EXP_ZERO_BELOW)

    def body(state):
        j, _ = state
        pv, carry = _sb_step(q, past_keys(ck_ref, j), past_keys(cv_ref, j), carry_ref[...], umat, None)
        acc_ref[...] += pv
        carry_ref[...] = carry
        return j - 1, jnp.max(carry)

    lax.while_loop(cond, body, (jnp.int32(P // KEY_BLOCK - 1), jnp.max(carry)))
    a = acc_ref[...]
    a = a * lax.rsqrt(jnp.mean(a * a, axis=-1, keepdims=True) + EPS)
    for h in range(H):
        o_ref[:, hs(h)] = (a[h] * g_ref[:, hs(h)]).astype(BF16)


def _sb_sample(q, kn, vn, cache_k, cache_v, g_sb, B, S, P):
    W = SB_HEADS * HEAD_DIM
    row = pl.BlockSpec((S, W), lambda b: (b, 0))
    past = pl.BlockSpec((P, W), lambda b: (b, 0))
    return pl.pallas_call(
        functools.partial(_sb_sample_kernel, S=S, P=P),
        grid=(B,),
        in_specs=[row, row, row, past, past, pl.BlockSpec((1, W), lambda b: (0, 0))],
        out_specs=row,
        out_shape=jax.ShapeDtypeStruct((B * S, W), BF16),
        scratch_shapes=[pltpu.VMEM((SB_HEADS, S, HEAD_DIM), F32),
                        pltpu.VMEM((SB_HEADS, S, KEY_BLOCK), F32)],
        compiler_params=_cparams("parallel"),
        name="sb_sample",
    )(q, kn, vn, cache_k.reshape(B * P, W), cache_v.reshape(B * P, W), g_sb.reshape(1, W))


def _mlstm_kernel(q_ref, k_ref, v_ref, o_ref, gt_ref, c0_ref, n0_ref, m0_ref, g_ref,
                  out_ref, c_out_ref, n_out_ref, m_out_ref, cext_ref, m_ref, *, L, Lp):
    c = pl.program_id(1)
    H = ML_HEADS
    VW = ML_V_DIM + LANES
    lane_row = lax.broadcasted_iota(I32, (1, LANES), 1)
    onehot0 = jnp.where(lane_row == 0, 1.0, 0.0)

    @pl.when(c == 0)
    def _():
        for h in range(H):
            cext_ref[h] = jnp.concatenate([c0_ref[0, h], n0_ref[0, h] * onehot0], axis=1)
            m_ref[h] = jnp.broadcast_to(m0_ref[0, h], (8, LANES))

    def pad_rows(a, fill=0.0):
        if Lp == L:
            return a
        return jnp.concatenate([a, jnp.full((Lp - L, a.shape[1]), fill, a.dtype)], axis=0)

    gt = gt_ref[...]
    lane = lax.broadcasted_iota(I32, (Lp, LANES), 1)
    if Lp != L:
        gt = jnp.concatenate([gt, jnp.broadcast_to(jnp.where(lane_row < H, NEG_BIG, 0.0), (Lp - L, LANES))], axis=0)
    lf = jnp.where((lane >= H) & (lane < 2 * H), gt, 0.0)
    ti = lax.broadcasted_iota(I32, (Lp, Lp), 0)
    si = lax.broadcasted_iota(I32, (Lp, Lp), 1)
    causal = si <= ti
    tri = jnp.where(causal, 1.0, 0.0).astype(BF16)
    bc = sum(_mm(tri, p) for p in _split3(lf))
    bc_t = bc.T
    gt_t = gt.T
    ones_blk = jnp.broadcast_to(onehot0, (Lp, LANES)).astype(BF16)

    for h in range(H):
        b_col = bc[:, H + h:H + h + 1]
        b_row = bc_t[H + h:H + h + 1, :]
        i_col = gt[:, h:h + 1]
        i_row = gt_t[h:h + 1, :]
        m_prev = m_ref[h][0:1, 0:1]
        qh = pad_rows(q_ref[:, h * HEAD_DIM:(h + 1) * HEAD_DIM])
        kh = pad_rows(k_ref[:, h * HEAD_DIM:(h + 1) * HEAD_DIM])
        vh = pad_rows(v_ref[:, h * ML_V_DIM:(h + 1) * ML_V_DIM])
        vext = jnp.concatenate([vh, ones_blk], axis=1)
        cext = cext_ref[h]

        log_d = jnp.where(causal, b_col - b_row + i_row, NEG_BIG)
        log_inter = b_col + m_prev
        m_row = jnp.maximum(log_inter, jnp.max(log_d, axis=1, keepdims=True))
        dmat = jnp.exp(log_d - m_row)
        s = lax.dot_general(qh, kh, (((1,), (1,)), ((), ())), preferred_element_type=F32) * dmat
        inter = jnp.exp(log_inter - m_row)
        num = _mm(s.astype(BF16), vext) + inter * _mm(qh, cext.astype(BF16))
        den = num[:, ML_V_DIM:ML_V_DIM + 1]
        hh = num[:L, :ML_V_DIM] / jnp.maximum(jnp.abs(den), jnp.exp(-m_row))[:L]
        hn = hh * lax.rsqrt(jnp.mean(hh * hh, axis=-1, keepdims=True) + EPS)
        cols = slice(h * ML_V_DIM, (h + 1) * ML_V_DIM)
        ogate = 1.0 / (1.0 + jnp.exp(-o_ref[:, cols]))
        out_ref[:, cols] = (ogate * (hn * g_ref[:, cols])).astype(BF16)

        b_last = b_col[Lp - 1:Lp, :]
        log_w = b_last - b_col + i_col
        m_new = jnp.maximum(b_last + m_prev, jnp.max(log_w, axis=0, keepdims=True))
        wk = jnp.exp(log_w - m_new)
        decay = jnp.exp(b_last + m_prev - m_new)
        upd = lax.dot_general(kh, (wk * vext.astype(F32)).astype(BF16), (((0,), (0,)), ((), ())),
                              preferred_element_type=F32)
        cnew = decay * cext + upd
        cext_ref[h] = cnew
        m_ref[h] = jnp.broadcast_to(m_new, (8, LANES))

    @pl.when(c == pl.num_programs(1) - 1)
    def _():
        for h in range(H):
            cf = cext_ref[h]
            c_out_ref[0, h] = cf[:, :ML_V_DIM]
            n_out_ref[0, h] = cf[:, ML_V_DIM:ML_V_DIM + 1]
            m_out_ref[0, h] = m_ref[h][0:1, 0:1]


def _mlstm(qm, km, vm, om, gt, C0, n0, m0, g_ml, B, S, L):
    nc = S // L
    Lp = max(L, LANES)
    H = ML_HEADS
    VW = ML_V_DIM + LANES
    row = lambda w: pl.BlockSpec((L, w), lambda b, c: (b * nc + c, 0))
    st = lambda *tail: pl.BlockSpec((1, H) + tail, lambda b, c: (b, 0, 0, 0))
    return pl.pallas_call(
        functools.partial(_mlstm_kernel, L=L, Lp=Lp),
        grid=(B, nc),
        in_specs=[row(H * HEAD_DIM), row(H * HEAD_DIM), row(H * ML_V_DIM), row(H * ML_V_DIM), row(LANES),
                  st(HEAD_DIM, ML_V_DIM), st(HEAD_DIM, 1), st(1, 1),
                  pl.BlockSpec((1, H * ML_V_DIM), lambda b, c: (0, 0))],
        out_specs=[row(H * ML_V_DIM), st(HEAD_DIM, ML_V_DIM), st(HEAD_DIM, 1), st(1, 1)],
        out_shape=[jax.ShapeDtypeStruct((B * S, H * ML_V_DIM), BF16),
                   jax.ShapeDtypeStruct((B, H, HEAD_DIM, ML_V_DIM), F32),
                   jax.ShapeDtypeStruct((B, H, HEAD_DIM, 1), F32),
                   jax.ShapeDtypeStruct((B, H, 1, 1), F32)],
        scratch_shapes=[pltpu.VMEM((H, HEAD_DIM, VW), F32), pltpu.VMEM((H, 8, LANES), F32)],
        compiler_params=_cparams("parallel", "arbitrary"),
        name="mlstm",
    )(qm, km, vm, om, gt, C0, n0.reshape(B, H, HEAD_DIM, 1), m0.reshape(B, H, 1, 1), g_ml.reshape(1, -1))


def _outproj_router_kernel(x_ref, sb_ref, ml_ref, wos_ref, wom_ref, g_ref, wr_hi_ref, wr_lo_ref, br_ref,
                           x2_ref, xn_ref, eid_ref, gate_ref, rank_ref, cnt_ref, carry_ref):
    i = pl.program_id(0)

    @pl.when(i == 0)
    def _():
        carry_ref[...] = jnp.zeros_like(carry_ref)

    x2 = x_ref[...] + _mm(sb_ref[...], wos_ref[...]) + _mm(ml_ref[...], wom_ref[...])
    x2_ref[...] = x2
    xn = _rms(x2, g_ref[...])
    xn_ref[...] = xn.astype(BF16)
    hi, lo = _split2(xn)
    logits = _mm(hi, wr_hi_ref[...]) + _mm(lo, wr_hi_ref[...]) + _mm(hi, wr_lo_ref[...]) + br_ref[...]
    tm = logits.shape[0]
    lane = lax.broadcasted_iota(I32, (tm, LANES), 1)
    vals, ids = [], []
    cnt = jnp.zeros((tm, LANES), F32)
    for _ in range(TOP_K):
        mx = jnp.max(logits, axis=-1, keepdims=True)
        idx = jnp.min(jnp.where(logits == mx, lane, LANES), axis=-1, keepdims=True)
        sel = lane == idx
        vals.append(mx)
        ids.append(idx)
        logits = jnp.where(sel, -jnp.inf, logits)
        cnt = cnt + jnp.where(sel, 1.0, 0.0)
    es = [jnp.exp(v - vals[0]) for v in vals]
    inv = 1.0 / sum(es)
    ti = lax.broadcasted_iota(I32, (tm, tm), 0)
    si = lax.broadcasted_iota(I32, (tm, tm), 1)
    before = _mm(jnp.where(si < ti, 1.0, 0.0).astype(BF16), cnt.astype(BF16)) + carry_ref[...]
    eid_o = jnp.zeros((tm, LANES), I32)
    gate_o = jnp.zeros((tm, LANES), F32)
    rank_o = jnp.zeros((tm, LANES), I32)
    for k in range(TOP_K):
        rk = jnp.sum(jnp.where(lane == ids[k], before, 0.0), axis=-1, keepdims=True)
        eid_o = jnp.where(lane == k, ids[k], eid_o)
        gate_o = jnp.where(lane == k, es[k] * inv, gate_o)
        rank_o = jnp.where(lane == k, rk.astype(I32), rank_o)
    eid_ref[...] = eid_o
    gate_ref[...] = gate_o
    rank_ref[...] = rank_o
    carry_ref[...] += jnp.sum(cnt, axis=0, keepdims=True)
    cnt_ref[...] = carry_ref[...]


def _outproj_router(x2d, sb_o, ml_o, w_out, norm_g, w_router, b_router, tm):
    T, D = x2d.shape
    W = sb_o.shape[1]
    wos = w_out[:W].astype(BF16)
    wom = w_out[W:].astype(BF16)
    wr = jnp.zeros((D, LANES), F32).at[:, :N_EXPERTS].set(w_router)
    wr_hi = wr.astype(BF16)
    wr_lo = (wr - wr_hi.astype(F32)).astype(BF16)
    br = jnp.full((1, LANES), NEG_BIG, F32).at[0, :N_EXPERTS].set(b_router)
    row = lambda w: pl.BlockSpec((tm, w), lambda i: (i, 0))
    return pl.pallas_call(
        _outproj_router_kernel,
        grid=(T // tm,),
        in_specs=[row(D), row(W), row(ml_o.shape[1]), _resident(wos.shape), _resident(wom.shape),
                  _resident((1, D)), _resident(wr_hi.shape), _resident(wr_lo.shape), _resident(br.shape)],
        out_specs=[row(D), row(D), row(LANES), row(LANES), row(LANES),
                   pl.BlockSpec((1, LANES), lambda i: (0, 0))],
        out_shape=[jax.ShapeDtypeStruct((T, D), F32), jax.ShapeDtypeStruct((T, D), BF16),
                   jax.ShapeDtypeStruct((T, LANES), I32), jax.ShapeDtypeStruct((T, LANES), F32),
                   jax.ShapeDtypeStruct((T, LANES), I32), jax.ShapeDtypeStruct((1, LANES), F32)],
        scratch_shapes=[pltpu.VMEM((1, LANES), F32)],
        compiler_params=_cparams("arbitrary"),
        name="outproj_router",
    )(x2d, sb_o, ml_o, wos, wom, norm_g.reshape(1, D), wr_hi, wr_lo, br)


def _gate_up_kernel(te_ref, nu_ref, x_ref, wg_ref, wl_ref, bg_ref, bl_ref, act_ref):
    @pl.when(pl.program_id(0) < nu_ref[0])
    def _():
        x = x_ref[...]
        glu = jnp.minimum(_mm(x, wg_ref[0]) + bg_ref[0], SWIGLU_LIMIT)
        lin = jnp.clip(_mm(x, wl_ref[0]) + bl_ref[0], -SWIGLU_LIMIT, SWIGLU_LIMIT)
        act = glu * (1.0 / (1.0 + jnp.exp(-SWIGLU_ALPHA * glu))) * (lin + 1.0)
        act_ref[...] = act.astype(BF16)


def _down_kernel(te_ref, nu_ref, a_ref, w_ref, b_ref, o_ref):
    @pl.when(pl.program_id(0) < nu_ref[0])
    def _():
        o_ref[...] = _mm(a_ref[...], w_ref[0]) + b_ref[0]


def _expert_ffn(xs, tile_e, n_used, w_gu, b_gu, w_dn, b_dn, tm, tn):
    n_rows, D = xs.shape
    F = w_dn.shape[1]
    nj = F // tn
    n_tiles = n_rows // tm

    def ic(i, nu):
        return jnp.minimum(i, nu[0] - 1)

    def jc(i, j, nu):
        return jnp.where(i < nu[0], j, nj - 1)

    act = pl.pallas_call(
        _gate_up_kernel,
        grid_spec=pltpu.PrefetchScalarGridSpec(
            num_scalar_prefetch=2,
            grid=(n_tiles, nj),
            in_specs=[pl.BlockSpec((tm, D), lambda i, j, te, nu: (ic(i, nu), 0)),
                      pl.BlockSpec((1, D, tn), lambda i, j, te, nu: (te[ic(i, nu)], 0, jc(i, j, nu))),
                      pl.BlockSpec((1, D, tn), lambda i, j, te, nu: (te[ic(i, nu)], 0, nj + jc(i, j, nu))),
                      pl.BlockSpec((1, 1, tn), lambda i, j, te, nu: (te[ic(i, nu)], 0, jc(i, j, nu))),
                      pl.BlockSpec((1, 1, tn), lambda i, j, te, nu: (te[ic(i, nu)], 0, nj + jc(i, j, nu)))],
            out_specs=pl.BlockSpec((tm, tn), lambda i, j, te, nu: (ic(i, nu), jc(i, j, nu)))),
        out_shape=jax.ShapeDtypeStruct((n_rows, F), BF16),
        compiler_params=_cparams("arbitrary", "arbitrary"),
        name="expert_gate_up",
    )(tile_e, n_used, xs, w_gu, w_gu, b_gu, b_gu)

    njd = D // tn

    def jd(i, j, nu):
        return jnp.where(i < nu[0], j, njd - 1)

    return pl.pallas_call(
        _down_kernel,
        grid_spec=pltpu.PrefetchScalarGridSpec(
            num_scalar_prefetch=2,
            grid=(n_tiles, njd),
            in_specs=[pl.BlockSpec((tm, F), lambda i, j, te, nu: (ic(i, nu), 0)),
                      pl.BlockSpec((1, F, tn), lambda i, j, te, nu: (te[ic(i, nu)], 0, jd(i, j, nu))),
                      pl.BlockSpec((1, 1, tn), lambda i, j, te, nu: (te[ic(i, nu)], 0, jd(i, j, nu)))],
            out_specs=pl.BlockSpec((tm, tn), lambda i, j, te, nu: (ic(i, nu), jd(i, j, nu)))),
        out_shape=jax.ShapeDtypeStruct((n_rows, D), F32),
        compiler_params=_cparams("arbitrary", "arbitrary"),
        name="expert_down",
    )(tile_e, n_used, act, w_dn, b_dn)


def _moe(xn2, eid, gate, rank, counts, w_gu, b_gu, w_dn, b_dn, tm, tn):
    T, D = xn2.shape
    A = T * TOP_K
    counts = counts[0, :N_EXPERTS].astype(I32)
    padded = (counts + tm - 1) // tm * tm
    pend = jnp.cumsum(padded)
    pstart = pend - padded
    e4 = eid[:, :TOP_K]
    dest = pstart[e4] + rank[:, :TOP_K]
    n_tiles = -(-A // tm) + N_EXPERTS
    tile_e = jnp.minimum(jnp.searchsorted(pend, jnp.arange(n_tiles, dtype=I32) * tm, side="right"),
                         N_EXPERTS - 1).astype(I32)
    n_used = (pend[-1:] // tm).astype(I32)
    src_tok = jnp.zeros((n_tiles * tm,), I32).at[dest.reshape(A)].set(jnp.arange(A, dtype=I32) // TOP_K)
    xs = xn2[src_tok]
    out = _expert_ffn(xs, tile_e, n_used, w_gu, b_gu, w_dn, b_dn, tm, tn)
    return jnp.sum(out[dest] * gate[:, :TOP_K, None], axis=1)


def _final_kernel(x_ref, f_ref, g_ref, y_ref):
    y_ref[...] = _rms(x_ref[...] + f_ref[...], g_ref[...])


def _final(x2, ffn, g, tm):
    T, D = x2.shape
    row = pl.BlockSpec((tm, D), lambda i: (i, 0))
    return pl.pallas_call(
        _final_kernel, grid=(T // tm,), in_specs=[row, row, _resident((1, D))], out_specs=row,
        out_shape=jax.ShapeDtypeStruct((T, D), F32), compiler_params=_cparams("parallel"), name="final_norm",
    )(x2, ffn, g.reshape(1, D))


def _tile(n, pref):
    return pref if n % pref == 0 else n


def _layer(x, past, lw, ew, final_g):
    (norm_mix_g, w_in, b_igate, b_fgate, g_sb_out, g_ml_out, w_out, norm_ffn_g, w_router, b_router) = lw
    w_gu, b_gu, w_dn, b_dn = ew
    B, S, D = x.shape
    T = B * S
    x2d = x.reshape(T, D)
    q, kf, kb, vf, vb, qm, km, vm, om, gt = _in_projection(x2d, norm_mix_g, w_in, b_igate, b_fgate, _tile(T, 256))
    if past is None:
        sb_o = _sb_prompt(q, kb, vb, g_sb_out, B, S, R=min(4, S // KEY_BLOCK))
        C0 = jnp.zeros((B, ML_HEADS, HEAD_DIM, ML_V_DIM), F32)
        n0 = jnp.zeros((B, ML_HEADS, HEAD_DIM), F32)
        m0 = jnp.zeros((B, ML_HEADS), F32)
        L = _tile(S, 256)
    else:
        cache_k, cache_v, C0, n0, m0 = past
        sb_o = _sb_sample(q, kb, vb, cache_k, cache_v, g_sb_out, B, S, cache_k.shape[1])
        L = S
    ml_o, C, n, m = _mlstm(qm, km, vm, om, gt, C0, n0, m0, g_ml_out, B, S, L)
    x2, xn2, eid, gate, rank, counts = _outproj_router(x2d, sb_o, ml_o, w_out, norm_ffn_g, w_router, b_router,
                                                        _tile(T, 512))
    tm = 512 if T * TOP_K >= 32 * 1024 else 256
    ffn = _moe(xn2, eid, gate, rank, counts, w_gu, b_gu, w_dn, b_dn, tm, 512)
    y = _final(x2, ffn, final_g, _tile(T, 512))
    return (y.reshape(B, S, D),
            kf.reshape(B, S, SB_HEADS, HEAD_DIM), vf.reshape(B, S, SB_HEADS, HEAD_DIM),
            C, n.reshape(B, ML_HEADS, HEAD_DIM), m.reshape(B, ML_HEADS))


def kernel(x_prompt, x_sample, cache_k, cache_v, state_C, state_n, state_m, norm_mix_g, w_in, b_igate, b_fgate,
           g_sb_out, g_ml_out, w_out, norm_ffn_g, w_router, b_router, w_gate_up, b_gate_up, w_down, b_down,
           final_norm_g):
    assert w_in.shape[0] == 1, "single-layer trunk"
    lw = (norm_mix_g[0], w_in[0], b_igate[0], b_fgate[0], g_sb_out[0], g_ml_out[0], w_out[0],
          norm_ffn_g[0], w_router[0], b_router[0])
    E = w_gate_up.shape[1]
    ew = (w_gate_up[0].astype(BF16), b_gate_up[0].reshape(E, 1, -1), w_down[0].astype(BF16),
          b_down[0].reshape(E, 1, -1))
    yp, kp, vp, Cp, np_, mp = _layer(x_prompt, None, lw, ew, final_norm_g)
    ys, ks, vs, Cs, ns, ms = _layer(x_sample, (cache_k[0], cache_v[0], state_C[0], state_n[0], state_m[0]),
                                    lw, ew, final_norm_g)
    return (yp, ys, kp[None], vp[None], Cp[None], np_[None], mp[None],
            ks[None], vs[None], Cs[None], ns[None], ms[None])
```

```python
import functools

import jax
import jax.numpy as jnp
from jax import lax
from jax.experimental import pallas as pl
from jax.experimental.pallas import tpu as pltpu

F32 = jnp.float32
BF16 = jnp.bfloat16
I32 = jnp.int32

EPS = 1e-6
SB_HEADS = 8
HEAD_DIM = 128
ML_HEADS = 4
ML_V_DIM = 256
N_EXPERTS = 32
TOP_K = 4
SWIGLU_ALPHA = 1.702
SWIGLU_LIMIT = 7.0
LANES = 128
KEY_BLOCK = 128
NEG_BIG = -1e30
EXP_ZERO_BELOW = -105.0
VMEM_LIMIT = 56 * 1024 * 1024


def _cparams(*sem):
    return pltpu.CompilerParams(dimension_semantics=sem, vmem_limit_bytes=VMEM_LIMIT)


def _resident(shape):
    nd = len(shape)
    return pl.BlockSpec(shape, lambda *_: (0,) * nd, pipeline_mode=pl.Buffered(1))


def _rms(x, g):
    return x * lax.rsqrt(jnp.mean(x * x, axis=-1, keepdims=True) + EPS) * g


def _log_sigmoid(z):
    return jnp.minimum(z, 0.0) - jnp.log(1.0 + jnp.exp(-jnp.abs(z)))


def _split2(x):
    hi = x.astype(BF16)
    lo = (x - hi.astype(F32)).astype(BF16)
    return hi, lo


def _split3(x):
    h1 = x.astype(BF16)
    r = x - h1.astype(F32)
    h2 = r.astype(BF16)
    h3 = (r - h2.astype(F32)).astype(BF16)
    return h1, h2, h3


def _mm(a, b):
    return jnp.dot(a, b, preferred_element_type=F32)


def _inproj_kernel(x_ref, g_ref, wq_ref, wk_ref, wv_ref, wqm_ref, wkm_ref, wvm_ref, wom_ref,
                   wg_ref, bg_ref,
                   q_ref, kf_ref, kb_ref, vf_ref, vb_ref, qm_ref, km_ref, vm_ref, om_ref, gt_ref):
    xn = _rms(x_ref[...], g_ref[...]).astype(BF16)
    q_ref[...] = (_mm(xn, wq_ref[...]) * (HEAD_DIM ** -0.5)).astype(BF16)
    k = _mm(xn, wk_ref[...])
    kf_ref[...] = k
    kb_ref[...] = k.astype(BF16)
    v = _mm(xn, wv_ref[...])
    vf_ref[...] = v
    vb_ref[...] = v.astype(BF16)
    qm_ref[...] = _mm(xn, wqm_ref[...]).astype(BF16)
    km_ref[...] = (_mm(xn, wkm_ref[...]) * (HEAD_DIM ** -0.5)).astype(BF16)
    vm_ref[...] = _mm(xn, wvm_ref[...]).astype(BF16)
    om_ref[...] = _mm(xn, wom_ref[...])
    gpre = _mm(xn, wg_ref[...]) + bg_ref[...]
    lane = lax.broadcasted_iota(I32, gpre.shape, 1)
    is_f = (lane >= ML_HEADS) & (lane < 2 * ML_HEADS)
    gt_ref[...] = jnp.where(is_f, _log_sigmoid(gpre), gpre)


def _in_projection(x2d, norm_g, w_in, b_igate, b_fgate, tm):
    T, D = x2d.shape
    sbw = SB_HEADS * HEAD_DIM
    mqk = ML_HEADS * HEAD_DIM
    mlw = ML_HEADS * ML_V_DIM
    o = 0
    ws = []
    for width in (sbw, sbw, sbw, mqk, mqk, mlw, mlw):
        ws.append(w_in[:, o:o + width].astype(BF16))
        o += width
    wg = jnp.zeros((D, LANES), F32).at[:, :2 * ML_HEADS].set(w_in[:, o:o + 2 * ML_HEADS]).astype(BF16)
    bg = jnp.zeros((1, LANES), F32).at[0, :ML_HEADS].set(b_igate).at[0, ML_HEADS:2 * ML_HEADS].set(b_fgate)
    row = lambda w: pl.BlockSpec((tm, w), lambda i: (i, 0))
    out_widths = (sbw, sbw, sbw, sbw, sbw, mqk, mqk, mlw, mlw, LANES)
    out_dtypes = (BF16, F32, BF16, F32, BF16, BF16, BF16, BF16, F32, F32)
    return pl.pallas_call(
        _inproj_kernel,
        grid=(T // tm,),
        in_specs=[row(D), _resident((1, D))] + [_resident(w.shape) for w in ws]
                 + [_resident(wg.shape), _resident(bg.shape)],
        out_specs=[row(w) for w in out_widths],
        out_shape=[jax.ShapeDtypeStruct((T, w), dt) for w, dt in zip(out_widths, out_dtypes)],
        compiler_params=_cparams("parallel"),
        name="in_projection",
    )(x2d, norm_g.reshape(1, D), *ws, wg, bg)


def _suffix_matrix():
    j = lax.broadcasted_iota(I32, (KEY_BLOCK, 2 * KEY_BLOCK), 0)
    c = lax.broadcasted_iota(I32, (KEY_BLOCK, 2 * KEY_BLOCK), 1)
    return jnp.where((c >= KEY_BLOCK) | (j > c), 1.0, 0.0).astype(BF16)


def _sb_step(q, k, v, carry, umat, mask):
    R, bq, _ = q.shape
    z = jnp.einsum("rqd,rkd->rqk", q, k, preferred_element_type=F32)
    lp = jnp.log(1.0 + jnp.exp(-jnp.abs(z)))
    log_beta = jnp.minimum(z, 0.0) - lp
    log_stay = log_beta - z
    if mask is not None:
        log_stay = jnp.where(mask, log_stay, 0.0)
    hi, lo = _split2(log_stay)
    st = _mm(hi.reshape(R * bq, KEY_BLOCK), umat) + _mm(lo.reshape(R * bq, KEY_BLOCK), umat)
    st = st.reshape(R, bq, 2 * KEY_BLOCK)
    w = jnp.exp(log_beta + st[:, :, :KEY_BLOCK] + carry)
    if mask is not None:
        w = jnp.where(mask, w, 0.0)
    pv = jnp.einsum("rqk,rkd->rqd", w.astype(BF16), v, preferred_element_type=F32)
    return pv, carry + st[:, :, KEY_BLOCK:]


def _sb_prompt_kernel(q_ref, k_ref, v_ref, g_ref, o_ref, acc_ref, carry_ref, *, R):
    qi = pl.program_id(2)
    blk0 = qi * R
    q = q_ref[...].reshape(R, KEY_BLOCK, HEAD_DIM)
    umat = _suffix_matrix()
    shape3 = (R, KEY_BLOCK, KEY_BLOCK)
    t_io = lax.broadcasted_iota(I32, shape3, 1)
    s_io = lax.broadcasted_iota(I32, shape3, 2)
    r_io = lax.broadcasted_iota(I32, shape3, 0)

    def load(ref, d):
        return jnp.stack([ref[pl.ds(pl.multiple_of(jnp.maximum(blk0 + r - d, 0) * KEY_BLOCK, KEY_BLOCK),
                                    KEY_BLOCK), :] for r in range(R)])

    def penalty(d_next):
        return jnp.where(r_io < d_next - blk0, NEG_BIG, 0.0)

    pv, carry = _sb_step(q, load(k_ref, 0), load(v_ref, 0), jnp.zeros(shape3, F32), umat, s_io < t_io)
    acc_ref[...] = pv
    carry = carry + penalty(1)
    carry_ref[...] = carry

    def cond(state):
        d, mx = state
        return (d < blk0 + R) & (mx > EXP_ZERO_BELOW)

    def body(state):
        d, _ = state
        pv, carry = _sb_step(q, load(k_ref, d), load(v_ref, d), carry_ref[...], umat, None)
        acc_ref[...] += pv
        carry = carry + penalty(d + 1)
        carry_ref[...] = carry
        return d + 1, jnp.max(carry)

    lax.while_loop(cond, body, (jnp.int32(1), jnp.max(carry)))
    a = acc_ref[...]
    out = a * lax.rsqrt(jnp.mean(a * a, axis=-1, keepdims=True) + EPS) * g_ref[...]
    o_ref[...] = out.reshape(R * KEY_BLOCK, HEAD_DIM).astype(BF16)


def _sb_prompt(q, k, v, g_sb, B, S, R):
    tq = R * KEY_BLOCK
    nq = S // tq
    return pl.pallas_call(
        functools.partial(_sb_prompt_kernel, R=R),
        grid=(B, SB_HEADS, nq),
        in_specs=[pl.BlockSpec((tq, HEAD_DIM), lambda b, h, i: (b * nq + i, h)),
                  pl.BlockSpec((S, HEAD_DIM), lambda b, h, i: (b, h)),
                  pl.BlockSpec((S, HEAD_DIM), lambda b, h, i: (b, h)),
                  pl.BlockSpec((1, HEAD_DIM), lambda b, h, i: (0, h))],
        out_specs=pl.BlockSpec((tq, HEAD_DIM), lambda b, h, i: (b * nq + i, h)),
        out_shape=jax.ShapeDtypeStruct((B * S, SB_HEADS * HEAD_DIM), BF16),
        scratch_shapes=[pltpu.VMEM((R, KEY_BLOCK, HEAD_DIM), F32),
                        pltpu.VMEM((R, KEY_BLOCK, KEY_BLOCK), F32)],
        compiler_params=_cparams("parallel", "parallel", "parallel"),
        name="sb_prompt",
    )(q, k, v, g_sb.reshape(1, -1))


def _sb_sample_kernel(q_ref, kn_ref, vn_ref, ck_ref, cv_ref, g_ref, o_ref, acc_ref, carry_ref, *, S, P):
    H = SB_HEADS
    hs = lambda h: slice(h * HEAD_DIM, (h + 1) * HEAD_DIM)
    umat = _suffix_matrix()
    q = jnp.stack([q_ref[:, hs(h)] for h in range(H)])
    pad = jnp.zeros((KEY_BLOCK - S, HEAD_DIM), BF16)

    def new_keys(ref):
        return jnp.stack([jnp.concatenate([ref[:, hs(h)], pad], axis=0) for h in range(H)])

    def past_keys(ref, j):
        start = pl.multiple_of(j * KEY_BLOCK, KEY_BLOCK)
        return jnp.stack([ref[pl.ds(start, KEY_BLOCK), hs(h)].astype(BF16) for h in range(H)])

    shape3 = (H, S, KEY_BLOCK)
    t_io = lax.broadcasted_iota(I32, shape3, 1)
    s_io = lax.broadcasted_iota(I32, shape3, 2)
    pv, carry = _sb_step(q, new_keys(kn_ref), new_keys(vn_ref), jnp.zeros(shape3, F32), umat, s_io < t_io)
    acc_ref[...] = pv
    carry_ref[...] = carry

    def cond(state):
        j, mx = state
        return (j >= 0) & (mx > EXP_ZERO_BELOW)

    def body(state):
        j, _ = state
        pv, carry = _sb_step(q, past_keys(ck_ref, j), past_keys(cv_ref, j), carry_ref[...], umat, None)
        acc_ref[...] += pv
        carry_ref[...] = carry
        return j - 1, jnp.max(carry)

    lax.while_loop(cond, body, (jnp.int32(P // KEY_BLOCK - 1), jnp.max(carry)))
    a = acc_ref[...]
    a = a * lax.rsqrt(jnp.mean(a * a, axis=-1, keepdims=True) + EPS)
    for h in range(H):
        o_ref[:, hs(h)] = (a[h] * g_ref[:, hs(h)]).astype(BF16)


def _sb_sample(q, kn, vn, cache_k, cache_v, g_sb, B, S, P):
    W = SB_HEADS * HEAD_DIM
    row = pl.BlockSpec((S, W), lambda b: (b, 0))
    past = pl.BlockSpec((P, W), lambda b: (b, 0))
    return pl.pallas_call(
        functools.partial(_sb_sample_kernel, S=S, P=P),
        grid=(B,),
        in_specs=[row, row, row, past, past, pl.BlockSpec((1, W), lambda b: (0, 0))],
        out_specs=row,
        out_shape=jax.ShapeDtypeStruct((B * S, W), BF16),
        scratch_shapes=[pltpu.VMEM((SB_HEADS, S, HEAD_DIM), F32),
                        pltpu.VMEM((SB_HEADS, S, KEY_BLOCK), F32)],
        compiler_params=_cparams("parallel"),
        name="sb_sample",
    )(q, kn, vn, cache_k.reshape(B * P, W), cache_v.reshape(B * P, W), g_sb.reshape(1, W))


def _mlstm_kernel(q_ref, k_ref, v_ref, o_ref, gt_ref, c0_ref, n0_ref, m0_ref, g_ref,
                  out_ref, c_out_ref, n_out_ref, m_out_ref, cext_ref, m_ref, *, L, Lp):
    c = pl.program_id(1)
    H = ML_HEADS
    VW = ML_V_DIM + LANES
    lane_row = lax.broadcasted_iota(I32, (1, LANES), 1)
    onehot0 = jnp.where(lane_row == 0, 1.0, 0.0)

    @pl.when(c == 0)
    def _():
        for h in range(H):
            cext_ref[h] = jnp.concatenate([c0_ref[0, h], n0_ref[0, h] * onehot0], axis=1)
            m_ref[h] = jnp.broadcast_to(m0_ref[0, h], (8, LANES))

    def pad_rows(a, fill=0.0):
        if Lp == L:
            return a
        return jnp.concatenate([a, jnp.full((Lp - L, a.shape[1]), fill, a.dtype)], axis=0)

    gt = gt_ref[...]
    lane = lax.broadcasted_iota(I32, (Lp, LANES), 1)
    if Lp != L:
        gt = jnp.concatenate([gt, jnp.broadcast_to(jnp.where(lane_row < H, NEG_BIG, 0.0), (Lp - L, LANES))], axis=0)
    lf = jnp.where((lane >= H) & (lane < 2 * H), gt, 0.0)
    ti = lax.broadcasted_iota(I32, (Lp, Lp), 0)
    si = lax.broadcasted_iota(I32, (Lp, Lp), 1)
    causal = si <= ti
    tri = jnp.where(causal, 1.0, 0.0).astype(BF16)
    bc = sum(_mm(tri, p) for p in _split3(lf))
    bc_t = bc.T
    gt_t = gt.T
    ones_blk = jnp.broadcast_to(onehot0, (Lp, LANES)).astype(BF16)

    for h in range(H):
        b_col = bc[:, H + h:H + h + 1]
        b_row = bc_t[H + h:H + h + 1, :]
        i_col = gt[:, h:h + 1]
        i_row = gt_t[h:h + 1, :]
        m_prev = m_ref[h][0:1, 0:1]
        qh = pad_rows(q_ref[:, h * HEAD_DIM:(h + 1) * HEAD_DIM])
        kh = pad_rows(k_ref[:, h * HEAD_DIM:(h + 1) * HEAD_DIM])
        vh = pad_rows(v_ref[:, h * ML_V_DIM:(h + 1) * ML_V_DIM])
        vext = jnp.concatenate([vh, ones_blk], axis=1)
        cext = cext_ref[h]

        log_d = jnp.where(causal, b_col - b_row + i_row, NEG_BIG)
        log_inter = b_col + m_prev
        m_row = jnp.maximum(log_inter, jnp.max(log_d, axis=1, keepdims=True))
        dmat = jnp.exp(log_d - m_row)
        s = lax.dot_general(qh, kh, (((1,), (1,)), ((), ())), preferred_element_type=F32) * dmat
        inter = jnp.exp(log_inter - m_row)
        num = _mm(s.astype(BF16), vext) + inter * _mm(qh, cext.astype(BF16))
        den = num[:, ML_V_DIM:ML_V_DIM + 1]
        hh = num[:L, :ML_V_DIM] / jnp.maximum(jnp.abs(den), jnp.exp(-m_row))[:L]
        hn = hh * lax.rsqrt(jnp.mean(hh * hh, axis=-1, keepdims=True) + EPS)
        cols = slice(h * ML_V_DIM, (h + 1) * ML_V_DIM)
        ogate = 1.0 / (1.0 + jnp.exp(-o_ref[:, cols]))
        out_ref[:, cols] = (ogate * (hn * g_ref[:, cols])).astype(BF16)

        b_last = b_col[Lp - 1:Lp, :]
        log_w = b_last - b_col + i_col
        m_new = jnp.maximum(b_last + m_prev, jnp.max(log_w, axis=0, keepdims=True))
        wk = jnp.exp(log_w - m_new)
        decay = jnp.exp(b_last + m_prev - m_new)
        upd = lax.dot_general(kh, (wk * vext.astype(F32)).astype(BF16), (((0,), (0,)), ((), ())),
                              preferred_element_type=F32)
        cnew = decay * cext + upd
        cext_ref[h] = cnew
        m_ref[h] = jnp.broadcast_to(m_new, (8, LANES))

    @pl.when(c == pl.num_programs(1) - 1)
    def _():
        for h in range(H):
            cf = cext_ref[h]
            c_out_ref[0, h] = cf[:, :ML_V_DIM]
            n_out_ref[0, h] = cf[:, ML_V_DIM:ML_V_DIM + 1]
            m_out_ref[0, h] = m_ref[h][0:1, 0:1]


def _mlstm(qm, km, vm, om, gt, C0, n0, m0, g_ml, B, S, L):
    nc = S // L
    Lp = max(L, LANES)
    H = ML_HEADS
    VW = ML_V_DIM + LANES
    row = lambda w: pl.BlockSpec((L, w), lambda b, c: (b * nc + c, 0))
    st = lambda *tail: pl.BlockSpec((1, H) + tail, lambda b, c: (b, 0, 0, 0))
    return pl.pallas_call(
        functools.partial(_mlstm_kernel, L=L, Lp=Lp),
        grid=(B, nc),
        in_specs=[row(H * HEAD_DIM), row(H * HEAD_DIM), row(H * ML_V_DIM), row(H * ML_V_DIM), row(LANES),
                  st(HEAD_DIM, ML_V_DIM), st(HEAD_DIM, 1), st(1, 1),
                  pl.BlockSpec((1, H * ML_V_DIM), lambda b, c: (0, 0))],
        out_specs=[row(H * ML_V_DIM), st(HEAD_DIM, ML_V_DIM), st(HEAD_DIM, 1), st(1, 1)],
        out_shape=[jax.ShapeDtypeStruct((B * S, H * ML_V_DIM), BF16),
                   jax.ShapeDtypeStruct((B, H, HEAD_DIM, ML_V_DIM), F32),
                   jax.ShapeDtypeStruct((B, H, HEAD_DIM, 1), F32),
                   jax.ShapeDtypeStruct((B, H, 1, 1), F32)],
        scratch_shapes=[pltpu.VMEM((H, HEAD_DIM, VW), F32), pltpu.VMEM((H, 8, LANES), F32)],
        compiler_params=_cparams("parallel", "arbitrary"),
        name="mlstm",
    )(qm, km, vm, om, gt, C0, n0.reshape(B, H, HEAD_DIM, 1), m0.reshape(B, H, 1, 1), g_ml.reshape(1, -1))


def _outproj_router_kernel(x_ref, sb_ref, ml_ref, wos_ref, wom_ref, g_ref, wr_hi_ref, wr_lo_ref, br_ref,
                           x2_ref, xn_ref, eid_ref, gate_ref, rank_ref, cnt_ref, carry_ref):
    i = pl.program_id(0)

    @pl.when(i == 0)
    def _():
        carry_ref[...] = jnp.zeros_like(carry_ref)

    x2 = x_ref[...] + _mm(sb_ref[...], wos_ref[...]) + _mm(ml_ref[...], wom_ref[...])
    x2_ref[...] = x2
    xn = _rms(x2, g_ref[...])
    xn_ref[...] = xn.astype(BF16)
    hi, lo = _split2(xn)
    logits = _mm(hi, wr_hi_ref[...]) + _mm(lo, wr_hi_ref[...]) + _mm(hi, wr_lo_ref[...]) + br_ref[...]
    tm = logits.shape[0]
    lane = lax.broadcasted_iota(I32, (tm, LANES), 1)
    vals, ids = [], []
    cnt = jnp.zeros((tm, LANES), F32)
    for _ in range(TOP_K):
        mx = jnp.max(logits, axis=-1, keepdims=True)
        idx = jnp.min(jnp.where(logits == mx, lane, LANES), axis=-1, keepdims=True)
        sel = lane == idx
        vals.append(mx)
        ids.append(idx)
        logits = jnp.where(sel, -jnp.inf, logits)
        cnt = cnt + jnp.where(sel, 1.0, 0.0)
    es = [jnp.exp(v - vals[0]) for v in vals]
    inv = 1.0 / sum(es)
    ti = lax.broadcasted_iota(I32, (tm, tm), 0)
    si = lax.broadcasted_iota(I32, (tm, tm), 1)
    before = _mm(jnp.where(si < ti, 1.0, 0.0).astype(BF16), cnt.astype(BF16)) + carry_ref[...]
    eid_o = jnp.zeros((tm, LANES), I32)
    gate_o = jnp.zeros((tm, LANES), F32)
    rank_o = jnp.zeros((tm, LANES), I32)
    for k in range(TOP_K):
        rk = jnp.sum(jnp.where(lane == ids[k], before, 0.0), axis=-1, keepdims=True)
        eid_o = jnp.where(lane == k, ids[k], eid_o)
        gate_o = jnp.where(lane == k, es[k] * inv, gate_o)
        rank_o = jnp.where(lane == k, rk.astype(I32), rank_o)
    eid_ref[...] = eid_o
    gate_ref[...] = gate_o
    rank_ref[...] = rank_o
    carry_ref[...] += jnp.sum(cnt, axis=0, keepdims=True)
    cnt_ref[...] = carry_ref[...]


def _outproj_router(x2d, sb_o, ml_o, w_out, norm_g, w_router, b_router, tm):
    T, D = x2d.shape
    W = sb_o.shape[1]
    wos = w_out[:W].astype(BF16)
    wom = w_out[W:].astype(BF16)
    wr = jnp.zeros((D, LANES), F32).at[:, :N_EXPERTS].set(w_router)
    wr_hi = wr.astype(BF16)
    wr_lo = (wr - wr_hi.astype(F32)).astype(BF16)
    br = jnp.full((1, LANES), NEG_BIG, F32).at[0, :N_EXPERTS].set(b_router)
    row = lambda w: pl.BlockSpec((tm, w), lambda i: (i, 0))
    return pl.pallas_call(
        _outproj_router_kernel,
        grid=(T // tm,),
        in_specs=[row(D), row(W), row(ml_o.shape[1]), _resident(wos.shape), _resident(wom.shape),
                  _resident((1, D)), _resident(wr_hi.shape), _resident(wr_lo.shape), _resident(br.shape)],
        out_specs=[row(D), row(D), row(LANES), row(LANES), row(LANES),
                   pl.BlockSpec((1, LANES), lambda i: (0, 0))],
        out_shape=[jax.ShapeDtypeStruct((T, D), F32), jax.ShapeDtypeStruct((T, D), BF16),
                   jax.ShapeDtypeStruct((T, LANES), I32), jax.ShapeDtypeStruct((T, LANES), F32),
                   jax.ShapeDtypeStruct((T, LANES), I32), jax.ShapeDtypeStruct((1, LANES), F32)],
        scratch_shapes=[pltpu.VMEM((1, LANES), F32)],
        compiler_params=_cparams("arbitrary"),
        name="outproj_router",
    )(x2d, sb_o, ml_o, wos, wom, norm_g.reshape(1, D), wr_hi, wr_lo, br)


def _gate_up_kernel(tile_ref, j_ref, e_ref, first_ref, n_ref, x_ref, wg_ref, wl_ref, bg_ref, bl_ref, act_ref,
                    wg_s, wl_s):
    w = pl.program_id(0)

    @pl.when(w < n_ref[0])
    def _():
        @pl.when(first_ref[w] == 1)
        def _():
            wg_s[...] = wg_ref[0].astype(BF16)
            wl_s[...] = wl_ref[0].astype(BF16)

        x = x_ref[...]
        glu = jnp.minimum(_mm(x, wg_s[...]) + bg_ref[0], SWIGLU_LIMIT)
        lin = jnp.clip(_mm(x, wl_s[...]) + bl_ref[0], -SWIGLU_LIMIT, SWIGLU_LIMIT)
        act = glu * (1.0 / (1.0 + jnp.exp(-SWIGLU_ALPHA * glu))) * (lin + 1.0)
        act_ref[...] = act.astype(BF16)


def _down_kernel(tile_ref, j_ref, e_ref, first_ref, n_ref, a_ref, w_ref, b_ref, o_ref, w_s):
    w = pl.program_id(0)

    @pl.when(w < n_ref[0])
    def _():
        @pl.when(first_ref[w] == 1)
        def _():
            w_s[...] = w_ref[0].astype(BF16)

        o_ref[...] = _mm(a_ref[...], w_s[...]) + b_ref[0]


def _work_list(tiles_per_expert, nj, n_tiles_max):
    tpe = tiles_per_expert
    tile_start = jnp.cumsum(tpe) - tpe
    item_end = jnp.cumsum(tpe * nj)
    item_start = item_end - tpe * nj
    n_items = item_end[-1:]
    w = jnp.arange(n_tiles_max * nj, dtype=I32)
    wc = jnp.minimum(w, n_items[0] - 1)
    e = jnp.minimum(jnp.searchsorted(item_end, wc, side="right"), N_EXPERTS - 1).astype(I32)
    local = wc - item_start[e]
    t = jnp.maximum(tpe[e], 1)
    j = local // t
    il = local - j * t
    first = ((il == 0) & (w < n_items[0])).astype(I32)
    return (tile_start[e] + il).astype(I32), j.astype(I32), e, first, n_items.astype(I32)


def _expert_ffn(xs, tiles_per_expert, w_gu, b_gu, w_dn, b_dn, tm, tn):
    n_rows, D = xs.shape
    F = w_dn.shape[1]
    n_tiles = n_rows // tm
    nj = F // tn
    wl = _work_list(tiles_per_expert, nj, n_tiles)
    act = pl.pallas_call(
        _gate_up_kernel,
        grid_spec=pltpu.PrefetchScalarGridSpec(
            num_scalar_prefetch=5,
            grid=(n_tiles * nj,),
            in_specs=[pl.BlockSpec((tm, D), lambda w, t, j, e, f, n: (t[w], 0)),
                      pl.BlockSpec((1, D, tn), lambda w, t, j, e, f, n: (e[w], 0, j[w])),
                      pl.BlockSpec((1, D, tn), lambda w, t, j, e, f, n: (e[w], 0, nj + j[w])),
                      pl.BlockSpec((1, 1, tn), lambda w, t, j, e, f, n: (e[w], 0, j[w])),
                      pl.BlockSpec((1, 1, tn), lambda w, t, j, e, f, n: (e[w], 0, nj + j[w]))],
            out_specs=pl.BlockSpec((tm, tn), lambda w, t, j, e, f, n: (t[w], j[w])),
            scratch_shapes=[pltpu.VMEM((D, tn), BF16), pltpu.VMEM((D, tn), BF16)]),
        out_shape=jax.ShapeDtypeStruct((n_rows, F), BF16),
        compiler_params=_cparams("arbitrary"),
        name="expert_gate_up",
    )(*wl, xs, w_gu, w_gu, b_gu, b_gu)

    njd = D // tn
    wl = _work_list(tiles_per_expert, njd, n_tiles)
    return pl.pallas_call(
        _down_kernel,
        grid_spec=pltpu.PrefetchScalarGridSpec(
            num_scalar_prefetch=5,
            grid=(n_tiles * njd,),
            in_specs=[pl.BlockSpec((tm, F), lambda w, t, j, e, f, n: (t[w], 0)),
                      pl.BlockSpec((1, F, tn), lambda w, t, j, e, f, n: (e[w], 0, j[w])),
                      pl.BlockSpec((1, 1, tn), lambda w, t, j, e, f, n: (e[w], 0, j[w]))],
            out_specs=pl.BlockSpec((tm, tn), lambda w, t, j, e, f, n: (t[w], j[w])),
            scratch_shapes=[pltpu.VMEM((F, tn), BF16)]),
        out_shape=jax.ShapeDtypeStruct((n_rows, D), F32),
        compiler_params=_cparams("arbitrary"),
        name="expert_down",
    )(*wl, act, w_dn, b_dn)


def _moe(xn2, eid, gate, rank, counts, w_gu, b_gu, w_dn, b_dn, tm, tn):
    T, D = xn2.shape
    A = T * TOP_K
    counts = counts[0, :N_EXPERTS].astype(I32)
    padded = (counts + tm - 1) // tm * tm
    pend = jnp.cumsum(padded)
    pstart = pend - padded
    e4 = eid[:, :TOP_K]
    dest = pstart[e4] + rank[:, :TOP_K]
    n_tiles = -(-A // tm) + N_EXPERTS
    src_tok = jnp.zeros((n_tiles * tm,), I32).at[dest.reshape(A)].set(jnp.arange(A, dtype=I32) // TOP_K)
    xs = xn2[src_tok]
    out = _expert_ffn(xs, padded // tm, w_gu, b_gu, w_dn, b_dn, tm, tn)
    return jnp.sum(out[dest] * gate[:, :TOP_K, None], axis=1)


def _final_kernel(x_ref, f_ref, g_ref, y_ref):
    y_ref[...] = _rms(x_ref[...] + f_ref[...], g_ref[...])


def _final(x2, ffn, g, tm):
    T, D = x2.shape
    row = pl.BlockSpec((tm, D), lambda i: (i, 0))
    return pl.pallas_call(
        _final_kernel, grid=(T // tm,), in_specs=[row, row, _resident((1, D))], out_specs=row,
        out_shape=jax.ShapeDtypeStruct((T, D), F32), compiler_params=_cparams("parallel"), name="final_norm",
    )(x2, ffn, g.reshape(1, D))


def _tile(n, pref):
    return pref if n % pref == 0 else n


def _layer(x, past, lw, ew, final_g):
    (norm_mix_g, w_in, b_igate, b_fgate, g_sb_out, g_ml_out, w_out, norm_ffn_g, w_router, b_router) = lw
    w_gu, b_gu, w_dn, b_dn = ew
    B, S, D = x.shape
    T = B * S
    x2d = x.reshape(T, D)
    q, kf, kb, vf, vb, qm, km, vm, om, gt = _in_projection(x2d, norm_mix_g, w_in, b_igate, b_fgate, _tile(T, 256))
    if past is None:
        sb_o = _sb_prompt(q, kb, vb, g_sb_out, B, S, R=min(4, S // KEY_BLOCK))
        C0 = jnp.zeros((B, ML_HEADS, HEAD_DIM, ML_V_DIM), F32)
        n0 = jnp.zeros((B, ML_HEADS, HEAD_DIM), F32)
        m0 = jnp.zeros((B, ML_HEADS), F32)
        L = _tile(S, 256)
    else:
        cache_k, cache_v, C0, n0, m0 = past
        sb_o = _sb_sample(q, kb, vb, cache_k, cache_v, g_sb_out, B, S, cache_k.shape[1])
        L = S
    ml_o, C, n, m = _mlstm(qm, km, vm, om, gt, C0, n0, m0, g_ml_out, B, S, L)
    x2, xn2, eid, gate, rank, counts = _outproj_router(x2d, sb_o, ml_o, w_out, norm_ffn_g, w_router, b_router,
                                                        _tile(T, 512))
    tm = 512 if T * TOP_K >= 32 * 1024 else 256
    ffn = _moe(xn2, eid, gate, rank, counts, w_gu, b_gu, w_dn, b_dn, tm, 1024)
    y = _final(x2, ffn, final_g, _tile(T, 512))
    return (y.reshape(B, S, D),
            kf.reshape(B, S, SB_HEADS, HEAD_DIM), vf.reshape(B, S, SB_HEADS, HEAD_DIM),
            C, n.reshape(B, ML_HEADS, HEAD_DIM), m.reshape(B, ML_HEADS))


def kernel(x_prompt, x_sample, cache_k, cache_v, state_C, state_n, state_m, norm_mix_g, w_in, b_igate, b_fgate,
           g_sb_out, g_ml_out, w_out, norm_ffn_g, w_router, b_router, w_gate_up, b_gate_up, w_down, b_down,
           final_norm_g):
    assert w_in.shape[0] == 1, "single-layer trunk"
    lw = (norm_mix_g[0], w_in[0], b_igate[0], b_fgate[0], g_sb_out[0], g_ml_out[0], w_out[0],
          norm_ffn_g[0], w_router[0], b_router[0])
    E = w_gate_up.shape[1]
    ew = (w_gate_up[0], b_gate_up[0].reshape(E, 1, -1), w_down[0], b_down[0].reshape(E, 1, -1))
    yp, kp, vp, Cp, np_, mp = _layer(x_prompt, None, lw, ew, final_norm_g)
    ys, ks, vs, Cs, ns, ms = _layer(x_sample, (cache_k[0], cache_v[0], state_C[0], state_n[0], state_m[0]),
                                    lw, ew, final_norm_g)
    return (yp, ys, kp[None], vp[None], Cp[None], np_[None], mp[None],
            ks[None], vs[None], Cs[None], ns[None], ms[None])
```

```python
import functools

import jax
import jax.numpy as jnp
from jax import lax
from jax.experimental import pallas as pl
from jax.experimental.pallas import tpu as pltpu

F32 = jnp.float32
BF16 = jnp.bfloat16
I32 = jnp.int32

EPS = 1e-6
SB_HEADS = 8
HEAD_DIM = 128
ML_HEADS = 4
ML_V_DIM = 256
N_EXPERTS = 32
TOP_K = 4
SWIGLU_ALPHA = 1.702
SWIGLU_LIMIT = 7.0
LANES = 128
KEY_BLOCK = 128
NEG_BIG = -1e30
EXP_ZERO_BELOW = -105.0
VMEM_LIMIT = 56 * 1024 * 1024


def _cparams(*sem):
    return pltpu.CompilerParams(dimension_semantics=sem, vmem_limit_bytes=VMEM_LIMIT)


def _resident(shape):
    nd = len(shape)
    return pl.BlockSpec(shape, lambda *_: (0,) * nd, pipeline_mode=pl.Buffered(1))


def _rms(x, g):
    return x * lax.rsqrt(jnp.mean(x * x, axis=-1, keepdims=True) + EPS) * g


def _log_sigmoid(z):
    return jnp.minimum(z, 0.0) - jnp.log(1.0 + jnp.exp(-jnp.abs(z)))


def _split2(x):
    hi = x.astype(BF16)
    lo = (x - hi.astype(F32)).astype(BF16)
    return hi, lo


def _split3(x):
    h1 = x.astype(BF16)
    r = x - h1.astype(F32)
    h2 = r.astype(BF16)
    h3 = (r - h2.astype(F32)).astype(BF16)
    return h1, h2, h3


def _mm(a, b):
    return jnp.dot(a, b, preferred_element_type=F32)


def _inproj_kernel(x_ref, g_ref, wq_ref, wk_ref, wv_ref, wqm_ref, wkm_ref, wvm_ref, wom_ref,
                   wg_ref, bg_ref,
                   q_ref, kf_ref, kb_ref, vf_ref, vb_ref, qm_ref, km_ref, vm_ref, om_ref, gt_ref):
    xn = _rms(x_ref[...], g_ref[...]).astype(BF16)
    q_ref[...] = (_mm(xn, wq_ref[...]) * (HEAD_DIM ** -0.5)).astype(BF16)
    k = _mm(xn, wk_ref[...])
    kf_ref[...] = k
    kb_ref[...] = k.astype(BF16)
    v = _mm(xn, wv_ref[...])
    vf_ref[...] = v
    vb_ref[...] = v.astype(BF16)
    qm_ref[...] = _mm(xn, wqm_ref[...]).astype(BF16)
    km_ref[...] = (_mm(xn, wkm_ref[...]) * (HEAD_DIM ** -0.5)).astype(BF16)
    vm_ref[...] = _mm(xn, wvm_ref[...]).astype(BF16)
    om_ref[...] = _mm(xn, wom_ref[...])
    gpre = _mm(xn, wg_ref[...]) + bg_ref[...]
    lane = lax.broadcasted_iota(I32, gpre.shape, 1)
    is_f = (lane >= ML_HEADS) & (lane < 2 * ML_HEADS)
    gt_ref[...] = jnp.where(is_f, _log_sigmoid(gpre), gpre)


def _in_projection(x2d, norm_g, w_in, b_igate, b_fgate, tm):
    T, D = x2d.shape
    sbw = SB_HEADS * HEAD_DIM
    mqk = ML_HEADS * HEAD_DIM
    mlw = ML_HEADS * ML_V_DIM
    o = 0
    ws = []
    for width in (sbw, sbw, sbw, mqk, mqk, mlw, mlw):
        ws.append(w_in[:, o:o + width].astype(BF16))
        o += width
    wg = jnp.zeros((D, LANES), F32).at[:, :2 * ML_HEADS].set(w_in[:, o:o + 2 * ML_HEADS]).astype(BF16)
    bg = jnp.zeros((1, LANES), F32).at[0, :ML_HEADS].set(b_igate).at[0, ML_HEADS:2 * ML_HEADS].set(b_fgate)
    row = lambda w: pl.BlockSpec((tm, w), lambda i: (i, 0))
    out_widths = (sbw, sbw, sbw, sbw, sbw, mqk, mqk, mlw, mlw, LANES)
    out_dtypes = (BF16, F32, BF16, F32, BF16, BF16, BF16, BF16, F32, F32)
    return pl.pallas_call(
        _inproj_kernel,
        grid=(T // tm,),
        in_specs=[row(D), _resident((1, D))] + [_resident(w.shape) for w in ws]
                 + [_resident(wg.shape), _resident(bg.shape)],
        out_specs=[row(w) for w in out_widths],
        out_shape=[jax.ShapeDtypeStruct((T, w), dt) for w, dt in zip(out_widths, out_dtypes)],
        compiler_params=_cparams("parallel"),
        name="in_projection",
    )(x2d, norm_g.reshape(1, D), *ws, wg, bg)


def _suffix_matrix():
    j = lax.broadcasted_iota(I32, (KEY_BLOCK, 2 * KEY_BLOCK), 0)
    c = lax.broadcasted_iota(I32, (KEY_BLOCK, 2 * KEY_BLOCK), 1)
    return jnp.where((c >= KEY_BLOCK) | (j > c), 1.0, 0.0).astype(BF16)


def _sb_step(q, k, v, carry, umat, mask):
    R, bq, _ = q.shape
    z = jnp.einsum("rqd,rkd->rqk", q, k, preferred_element_type=F32)
    lp = jnp.log(1.0 + jnp.exp(-jnp.abs(z)))
    log_beta = jnp.minimum(z, 0.0) - lp
    log_stay = log_beta - z
    if mask is not None:
        log_stay = jnp.where(mask, log_stay, 0.0)
    hi, lo = _split2(log_stay)
    st = _mm(hi.reshape(R * bq, KEY_BLOCK), umat) + _mm(lo.reshape(R * bq, KEY_BLOCK), umat)
    st = st.reshape(R, bq, 2 * KEY_BLOCK)
    w = jnp.exp(log_beta + st[:, :, :KEY_BLOCK] + carry)
    if mask is not None:
        w = jnp.where(mask, w, 0.0)
    pv = jnp.einsum("rqk,rkd->rqd", w.astype(BF16), v, preferred_element_type=F32)
    return pv, carry + st[:, :, KEY_BLOCK:]


def _sb_prompt_kernel(q_ref, k_ref, v_ref, g_ref, o_ref, acc_ref, carry_ref, *, R):
    qi = pl.program_id(2)
    blk0 = qi * R
    q = q_ref[...].reshape(R, KEY_BLOCK, HEAD_DIM)
    umat = _suffix_matrix()
    shape3 = (R, KEY_BLOCK, KEY_BLOCK)
    t_io = lax.broadcasted_iota(I32, shape3, 1)
    s_io = lax.broadcasted_iota(I32, shape3, 2)
    r_io = lax.broadcasted_iota(I32, shape3, 0)

    def load(ref, d):
        return jnp.stack([ref[pl.ds(pl.multiple_of(jnp.maximum(blk0 + r - d, 0) * KEY_BLOCK, KEY_BLOCK),
                                    KEY_BLOCK), :] for r in range(R)])

    def penalty(d_next):
        return jnp.where(r_io < d_next - blk0, NEG_BIG, 0.0)

    pv, carry = _sb_step(q, load(k_ref, 0), load(v_ref, 0), jnp.zeros(shape3, F32), umat, s_io < t_io)
    acc_ref[...] = pv
    carry = carry + penalty(1)
    carry_ref[...] = carry

    def cond(state):
        d, mx = state
        return (d < blk0 + R) & (mx > EXP_ZERO_BELOW)

    def body(state):
        d, _ = state
        pv, carry = _sb_step(q, load(k_ref, d), load(v_ref, d), carry_ref[...], umat, None)
        acc_ref[...] += pv
        carry = carry + penalty(d + 1)
        carry_ref[...] = carry
        return d + 1, jnp.max(carry)

    lax.while_loop(cond, body, (jnp.int32(1), jnp.max(carry)))
    a = acc_ref[...]
    out = a * lax.rsqrt(jnp.mean(a * a, axis=-1, keepdims=True) + EPS) * g_ref[...]
    o_ref[...] = out.reshape(R * KEY_BLOCK, HEAD_DIM).astype(BF16)


def _sb_prompt(q, k, v, g_sb, B, S, R):
    tq = R * KEY_BLOCK
    nq = S // tq
    return pl.pallas_call(
        functools.partial(_sb_prompt_kernel, R=R),
        grid=(B, SB_HEADS, nq),
        in_specs=[pl.BlockSpec((tq, HEAD_DIM), lambda b, h, i: (b * nq + i, h)),
                  pl.BlockSpec((S, HEAD_DIM), lambda b, h, i: (b, h)),
                  pl.BlockSpec((S, HEAD_DIM), lambda b, h, i: (b, h)),
                  pl.BlockSpec((1, HEAD_DIM), lambda b, h, i: (0, h))],
        out_specs=pl.BlockSpec((tq, HEAD_DIM), lambda b, h, i: (b * nq + i, h)),
        out_shape=jax.ShapeDtypeStruct((B * S, SB_HEADS * HEAD_DIM), BF16),
        scratch_shapes=[pltpu.VMEM((R, KEY_BLOCK, HEAD_DIM), F32),
                        pltpu.VMEM((R, KEY_BLOCK, KEY_BLOCK), F32)],
        compiler_params=_cparams("parallel", "parallel", "parallel"),
        name="sb_prompt",
    )(q, k, v, g_sb.reshape(1, -1))


def _sb_sample_kernel(q_ref, kn_ref, vn_ref, ck_ref, cv_ref, g_ref, o_ref, acc_ref, carry_ref, *, S, P):
    H = SB_HEADS
    hs = lambda h: slice(h * HEAD_DIM, (h + 1) * HEAD_DIM)
    umat = _suffix_matrix()
    q = jnp.stack([q_ref[:, hs(h)] for h in range(H)])
    pad = jnp.zeros((KEY_BLOCK - S, HEAD_DIM), BF16)

    def new_keys(ref):
        return jnp.stack([jnp.concatenate([ref[:, hs(h)], pad], axis=0) for h in range(H)])

    def past_keys(ref, j):
        start = pl.multiple_of(j * KEY_BLOCK, KEY_BLOCK)
        return jnp.stack([ref[pl.ds(start, KEY_BLOCK), hs(h)].astype(BF16) for h in range(H)])

    shape3 = (H, S, KEY_BLOCK)
    t_io = lax.broadcasted_iota(I32, shape3, 1)
    s_io = lax.broadcasted_iota(I32, shape3, 2)
    pv, carry = _sb_step(q, new_keys(kn_ref), new_keys(vn_ref), jnp.zeros(shape3, F32), umat, s_io < t_io)
    acc_ref[...] = pv
    carry_ref[...] = carry

    def cond(state):
        j, mx = state
        return (j >= 0) & (mx > EXP_ZERO_BELOW)

    def body(state):
        j, _ = state
        pv, carry = _sb_step(q, past_keys(ck_ref, j), past_keys(cv_ref, j), carry_ref[...], umat, None)
        acc_ref[...] += pv
        carry_ref[...] = carry
        return j - 1, jnp.max(carry)

    lax.while_loop(cond, body, (jnp.int32(P // KEY_BLOCK - 1), jnp.max(carry)))
    a = acc_ref[...]
    a = a * lax.rsqrt(jnp.mean(a * a, axis=-1, keepdims=True) + EPS)
    for h in range(H):
        o_ref[:, hs(h)] = (a[h] * g_ref[:, hs(h)]).astype(BF16)


def _sb_sample(q, kn, vn, cache_k, cache_v, g_sb, B, S, P):
    W = SB_HEADS * HEAD_DIM
    row = pl.BlockSpec((S, W), lambda b: (b, 0))
    past = pl.BlockSpec((P, W), lambda b: (b, 0))
    return pl.pallas_call(
        functools.partial(_sb_sample_kernel, S=S, P=P),
        grid=(B,),
        in_specs=[row, row, row, past, past, pl.BlockSpec((1, W), lambda b: (0, 0))],
        out_specs=row,
        out_shape=jax.ShapeDtypeStruct((B * S, W), BF16),
        scratch_shapes=[pltpu.VMEM((SB_HEADS, S, HEAD_DIM), F32),
                        pltpu.VMEM((SB_HEADS, S, KEY_BLOCK), F32)],
        compiler_params=_cparams("parallel"),
        name="sb_sample",
    )(q, kn, vn, cache_k.reshape(B * P, W), cache_v.reshape(B * P, W), g_sb.reshape(1, W))


def _mlstm_kernel(q_ref, k_ref, v_ref, o_ref, gt_ref, c0_ref, n0_ref, m0_ref, g_ref,
                  out_ref, c_out_ref, n_out_ref, m_out_ref, cext_ref, m_ref, *, L, Lp):
    c = pl.program_id(1)
    H = ML_HEADS
    VW = ML_V_DIM + LANES
    lane_row = lax.broadcasted_iota(I32, (1, LANES), 1)
    onehot0 = jnp.where(lane_row == 0, 1.0, 0.0)

    @pl.when(c == 0)
    def _():
        for h in range(H):
            cext_ref[h] = jnp.concatenate([c0_ref[0, h], n0_ref[0, h] * onehot0], axis=1)
            m_ref[h] = jnp.broadcast_to(m0_ref[0, h], (8, LANES))

    def pad_rows(a, fill=0.0):
        if Lp == L:
            return a
        return jnp.concatenate([a, jnp.full((Lp - L, a.shape[1]), fill, a.dtype)], axis=0)

    gt = gt_ref[...]
    lane = lax.broadcasted_iota(I32, (Lp, LANES), 1)
    if Lp != L:
        gt = jnp.concatenate([gt, jnp.broadcast_to(jnp.where(lane_row < H, NEG_BIG, 0.0), (Lp - L, LANES))], axis=0)
    lf = jnp.where((lane >= H) & (lane < 2 * H), gt, 0.0)
    ti = lax.broadcasted_iota(I32, (Lp, Lp), 0)
    si = lax.broadcasted_iota(I32, (Lp, Lp), 1)
    causal = si <= ti
    tri = jnp.where(causal, 1.0, 0.0).astype(BF16)
    bc = sum(_mm(tri, p) for p in _split3(lf))
    bc_t = bc.T
    gt_t = gt.T
    ones_blk = jnp.broadcast_to(onehot0, (Lp, LANES)).astype(BF16)

    for h in range(H):
        b_col = bc[:, H + h:H + h + 1]
        b_row = bc_t[H + h:H + h + 1, :]
        i_col = gt[:, h:h + 1]
        i_row = gt_t[h:h + 1, :]
        m_prev = m_ref[h][0:1, 0:1]
        qh = pad_rows(q_ref[:, h * HEAD_DIM:(h + 1) * HEAD_DIM])
        kh = pad_rows(k_ref[:, h * HEAD_DIM:(h + 1) * HEAD_DIM])
        vh = pad_rows(v_ref[:, h * ML_V_DIM:(h + 1) * ML_V_DIM])
        vext = jnp.concatenate([vh, ones_blk], axis=1)
        cext = cext_ref[h]

        log_d = jnp.where(causal, b_col - b_row + i_row, NEG_BIG)
        log_inter = b_col + m_prev
        m_row = jnp.maximum(log_inter, jnp.max(log_d, axis=1, keepdims=True))
        dmat = jnp.exp(log_d - m_row)
        s = lax.dot_general(qh, kh, (((1,), (1,)), ((), ())), preferred_element_type=F32) * dmat
        inter = jnp.exp(log_inter - m_row)
        num = _mm(s.astype(BF16), vext) + inter * _mm(qh, cext.astype(BF16))
        den = num[:, ML_V_DIM:ML_V_DIM + 1]
        hh = num[:L, :ML_V_DIM] / jnp.maximum(jnp.abs(den), jnp.exp(-m_row))[:L]
        hn = hh * lax.rsqrt(jnp.mean(hh * hh, axis=-1, keepdims=True) + EPS)
        cols = slice(h * ML_V_DIM, (h + 1) * ML_V_DIM)
        ogate = 1.0 / (1.0 + jnp.exp(-o_ref[:, cols]))
        out_ref[:, cols] = (ogate * (hn * g_ref[:, cols])).astype(BF16)

        b_last = b_col[Lp - 1:Lp, :]
        log_w = b_last - b_col + i_col
        m_new = jnp.maximum(b_last + m_prev, jnp.max(log_w, axis=0, keepdims=True))
        wk = jnp.exp(log_w - m_new)
        decay = jnp.exp(b_last + m_prev - m_new)
        upd = lax.dot_general(kh, (wk * vext.astype(F32)).astype(BF16), (((0,), (0,)), ((), ())),
                              preferred_element_type=F32)
        cnew = decay * cext + upd
        cext_ref[h] = cnew
        m_ref[h] = jnp.broadcast_to(m_new, (8, LANES))

    @pl.when(c == pl.num_programs(1) - 1)
    def _():
        for h in range(H):
            cf = cext_ref[h]
            c_out_ref[0, h] = cf[:, :ML_V_DIM]
            n_out_ref[0, h] = cf[:, ML_V_DIM:ML_V_DIM + 1]
            m_out_ref[0, h] = m_ref[h][0:1, 0:1]


def _mlstm(qm, km, vm, om, gt, C0, n0, m0, g_ml, B, S, L):
    nc = S // L
    Lp = max(L, LANES)
    H = ML_HEADS
    VW = ML_V_DIM + LANES
    row = lambda w: pl.BlockSpec((L, w), lambda b, c: (b * nc + c, 0))
    st = lambda *tail: pl.BlockSpec((1, H) + tail, lambda b, c: (b, 0, 0, 0))
    return pl.pallas_call(
        functools.partial(_mlstm_kernel, L=L, Lp=Lp),
        grid=(B, nc),
        in_specs=[row(H * HEAD_DIM), row(H * HEAD_DIM), row(H * ML_V_DIM), row(H * ML_V_DIM), row(LANES),
                  st(HEAD_DIM, ML_V_DIM), st(HEAD_DIM, 1), st(1, 1),
                  pl.BlockSpec((1, H * ML_V_DIM), lambda b, c: (0, 0))],
        out_specs=[row(H * ML_V_DIM), st(HEAD_DIM, ML_V_DIM), st(HEAD_DIM, 1), st(1, 1)],
        out_shape=[jax.ShapeDtypeStruct((B * S, H * ML_V_DIM), BF16),
                   jax.ShapeDtypeStruct((B, H, HEAD_DIM, ML_V_DIM), F32),
                   jax.ShapeDtypeStruct((B, H, HEAD_DIM, 1), F32),
                   jax.ShapeDtypeStruct((B, H, 1, 1), F32)],
        scratch_shapes=[pltpu.VMEM((H, HEAD_DIM, VW), F32), pltpu.VMEM((H, 8, LANES), F32)],
        compiler_params=_cparams("parallel", "arbitrary"),
        name="mlstm",
    )(qm, km, vm, om, gt, C0, n0.reshape(B, H, HEAD_DIM, 1), m0.reshape(B, H, 1, 1), g_ml.reshape(1, -1))


def _pack_halves(lo, hi):
    lo_bits = pltpu.bitcast(lo.astype(BF16).astype(F32), jnp.uint32) >> 16
    hi_bits = pltpu.bitcast(hi.astype(BF16).astype(F32), jnp.uint32) & jnp.uint32(0xFFFF0000)
    return lo_bits | hi_bits


def _unpack_halves(w):
    lo = pltpu.bitcast(w << 16, F32).astype(BF16)
    hi = pltpu.bitcast(w & jnp.uint32(0xFFFF0000), F32).astype(BF16)
    return lo, hi


def _outproj_router_kernel(x_ref, sb_ref, ml_ref, wos_ref, wom_ref, g_ref, wr_hi_ref, wr_lo_ref, br_ref, c0_ref,
                           x2_ref, xn_ref, eid_ref, gate_ref, rank_ref, cnt_ref, carry_ref):
    i = pl.program_id(0)

    @pl.when(i == 0)
    def _():
        carry_ref[...] = c0_ref[...]

    x2 = x_ref[...] + _mm(sb_ref[...], wos_ref[...]) + _mm(ml_ref[...], wom_ref[...])
    x2_ref[...] = x2
    xn = _rms(x2, g_ref[...])
    half = xn.shape[1] // 2
    for c in range(half // LANES):
        xn_ref[:, c, :] = _pack_halves(xn[:, c * LANES:(c + 1) * LANES],
                                       xn[:, half + c * LANES:half + (c + 1) * LANES])
    hi, lo = _split2(xn)
    logits = _mm(hi, wr_hi_ref[...]) + _mm(lo, wr_hi_ref[...]) + _mm(hi, wr_lo_ref[...]) + br_ref[...]
    tm = logits.shape[0]
    lane = lax.broadcasted_iota(I32, (tm, LANES), 1)
    vals, ids = [], []
    cnt = jnp.zeros((tm, LANES), F32)
    for _ in range(TOP_K):
        mx = jnp.max(logits, axis=-1, keepdims=True)
        idx = jnp.min(jnp.where(logits == mx, lane, LANES), axis=-1, keepdims=True)
        sel = lane == idx
        vals.append(mx)
        ids.append(idx)
        logits = jnp.where(sel, -jnp.inf, logits)
        cnt = cnt + jnp.where(sel, 1.0, 0.0)
    es = [jnp.exp(v - vals[0]) for v in vals]
    inv = 1.0 / sum(es)
    ti = lax.broadcasted_iota(I32, (tm, tm), 0)
    si = lax.broadcasted_iota(I32, (tm, tm), 1)
    before = _mm(jnp.where(si < ti, 1.0, 0.0).astype(BF16), cnt.astype(BF16)) + carry_ref[...]
    eid_o = jnp.zeros((tm, LANES), I32)
    gate_o = jnp.zeros((tm, LANES), F32)
    rank_o = jnp.zeros((tm, LANES), I32)
    for k in range(TOP_K):
        rk = jnp.sum(jnp.where(lane == ids[k], before, 0.0), axis=-1, keepdims=True)
        eid_o = jnp.where(lane == k, ids[k], eid_o)
        gate_o = jnp.where(lane == k, es[k] * inv, gate_o)
        rank_o = jnp.where(lane == k, rk.astype(I32), rank_o)
    eid_ref[...] = eid_o
    gate_ref[...] = gate_o
    rank_ref[...] = rank_o
    carry_ref[...] += jnp.sum(cnt, axis=0, keepdims=True)
    cnt_ref[...] = carry_ref[...]


def _outproj_router(x2d, sb_o, ml_o, w_out, norm_g, w_router, b_router, counts_in, tm):
    T, D = x2d.shape
    nch = D // 2 // LANES
    W = sb_o.shape[1]
    wos = w_out[:W].astype(BF16)
    wom = w_out[W:].astype(BF16)
    wr = jnp.zeros((D, LANES), F32).at[:, :N_EXPERTS].set(w_router)
    wr_hi = wr.astype(BF16)
    wr_lo = (wr - wr_hi.astype(F32)).astype(BF16)
    br = jnp.full((1, LANES), NEG_BIG, F32).at[0, :N_EXPERTS].set(b_router)
    row = lambda w: pl.BlockSpec((tm, w), lambda i: (i, 0))
    return pl.pallas_call(
        _outproj_router_kernel,
        grid=(T // tm,),
        in_specs=[row(D), row(W), row(ml_o.shape[1]), _resident(wos.shape), _resident(wom.shape),
                  _resident((1, D)), _resident(wr_hi.shape), _resident(wr_lo.shape), _resident(br.shape),
                  _resident((1, LANES))],
        out_specs=[row(D), pl.BlockSpec((tm, nch, LANES), lambda i: (i, 0, 0)), row(LANES), row(LANES), row(LANES),
                   pl.BlockSpec((1, LANES), lambda i: (0, 0))],
        out_shape=[jax.ShapeDtypeStruct((T, D), F32), jax.ShapeDtypeStruct((T, nch, LANES), jnp.uint32),
                   jax.ShapeDtypeStruct((T, LANES), I32), jax.ShapeDtypeStruct((T, LANES), F32),
                   jax.ShapeDtypeStruct((T, LANES), I32), jax.ShapeDtypeStruct((1, LANES), F32)],
        scratch_shapes=[pltpu.VMEM((1, LANES), F32)],
        compiler_params=_cparams("arbitrary"),
        name="outproj_router",
    )(x2d, sb_o, ml_o, wos, wom, norm_g.reshape(1, D), wr_hi, wr_lo, br, counts_in)


def _gate_up_kernel(tile_ref, j_ref, e_ref, first_ref, n_ref, x_ref, wg_ref, wl_ref, bg_ref, bl_ref, act_ref,
                    wg_s, wl_s, x_s):
    w = pl.program_id(0)

    @pl.when(w < n_ref[0])
    def _():
        @pl.when(first_ref[w] == 1)
        def _():
            wg_s[...] = wg_ref[0].astype(BF16)
            wl_s[...] = wl_ref[0].astype(BF16)

        nch = x_ref.shape[1]
        for c in range(nch):
            lo, hi = _unpack_halves(x_ref[:, c, :])
            x_s[:, c * LANES:(c + 1) * LANES] = lo
            x_s[:, (nch + c) * LANES:(nch + c + 1) * LANES] = hi
        x = x_s[...]
        glu = jnp.minimum(_mm(x, wg_s[...]) + bg_ref[0], SWIGLU_LIMIT)
        lin = jnp.clip(_mm(x, wl_s[...]) + bl_ref[0], -SWIGLU_LIMIT, SWIGLU_LIMIT)
        act = glu * (1.0 / (1.0 + jnp.exp(-SWIGLU_ALPHA * glu))) * (lin + 1.0)
        act_ref[...] = act.astype(BF16)

    @pl.when(w >= n_ref[0])
    def _():
        act_ref[...] = jnp.zeros_like(act_ref)


def _down_kernel(tile_ref, j_ref, e_ref, first_ref, n_ref, a_ref, w_ref, b_ref, o_ref, w_s):
    w = pl.program_id(0)

    @pl.when(w < n_ref[0])
    def _():
        @pl.when(first_ref[w] == 1)
        def _():
            w_s[...] = w_ref[0].astype(BF16)

        out = _mm(a_ref[...], w_s[...]) + b_ref[0]
        for c in range(o_ref.shape[1]):
            o_ref[:, c, :] = out[:, c * LANES:(c + 1) * LANES]

    @pl.when(w >= n_ref[0])
    def _():
        o_ref[...] = jnp.zeros_like(o_ref)


def _work_list(tiles_per_expert, nj, n_tiles_max):
    tpe = tiles_per_expert
    tile_start = jnp.cumsum(tpe) - tpe
    item_end = jnp.cumsum(tpe * nj)
    item_start = item_end - tpe * nj
    n_items = item_end[-1:]
    w = jnp.arange(n_tiles_max * nj, dtype=I32)
    wc = jnp.minimum(w, n_items[0] - 1)
    e = jnp.minimum(jnp.sum((wc[:, None] >= item_end[None, :]).astype(I32), axis=1), N_EXPERTS - 1)
    local = wc - item_start[e]
    t = jnp.maximum(tpe[e], 1)
    j = local // t
    il = local - j * t
    real = w < n_items[0]
    first = ((il == 0) & real).astype(I32)
    tail = jnp.maximum(w - n_items[0], 0)
    tile = jnp.where(real, tile_start[e] + il, jnp.sum(tpe) + tail // nj)
    j = jnp.where(real, j, tail % nj)
    return tile.astype(I32), j.astype(I32), e, first, n_items.astype(I32)


def _expert_ffn(xs, tiles_per_expert, w_gu, b_gu, w_dn, b_dn, tm, tn):
    n_rows, nch, _ = xs.shape
    D = 2 * nch * LANES
    F = w_dn.shape[1]
    n_tiles = n_rows // tm
    nj = F // tn
    wl = _work_list(tiles_per_expert, nj, n_tiles)
    act = pl.pallas_call(
        _gate_up_kernel,
        grid_spec=pltpu.PrefetchScalarGridSpec(
            num_scalar_prefetch=5,
            grid=(n_tiles * nj,),
            in_specs=[pl.BlockSpec((tm, nch, LANES), lambda w, t, j, e, f, n: (t[w], 0, 0)),
                      pl.BlockSpec((1, D, tn), lambda w, t, j, e, f, n: (e[w], 0, j[w])),
                      pl.BlockSpec((1, D, tn), lambda w, t, j, e, f, n: (e[w], 0, nj + j[w])),
                      pl.BlockSpec((1, 1, tn), lambda w, t, j, e, f, n: (e[w], 0, j[w])),
                      pl.BlockSpec((1, 1, tn), lambda w, t, j, e, f, n: (e[w], 0, nj + j[w]))],
            out_specs=pl.BlockSpec((tm, tn), lambda w, t, j, e, f, n: (t[w], j[w])),
            scratch_shapes=[pltpu.VMEM((D, tn), BF16), pltpu.VMEM((D, tn), BF16), pltpu.VMEM((tm, D), BF16)]),
        out_shape=jax.ShapeDtypeStruct((n_rows, F), BF16),
        compiler_params=_cparams("arbitrary"),
        name="expert_gate_up",
    )(*wl, xs, w_gu, w_gu, b_gu, b_gu)

    njd = D // tn
    wl = _work_list(tiles_per_expert, njd, n_tiles)
    return pl.pallas_call(
        _down_kernel,
        grid_spec=pltpu.PrefetchScalarGridSpec(
            num_scalar_prefetch=5,
            grid=(n_tiles * njd,),
            in_specs=[pl.BlockSpec((tm, F), lambda w, t, j, e, f, n: (t[w], 0)),
                      pl.BlockSpec((1, F, tn), lambda w, t, j, e, f, n: (e[w], 0, j[w])),
                      pl.BlockSpec((1, 1, tn), lambda w, t, j, e, f, n: (e[w], 0, j[w]))],
            out_specs=pl.BlockSpec((tm, tn // LANES, LANES), lambda w, t, j, e, f, n: (t[w], j[w], 0)),
            scratch_shapes=[pltpu.VMEM((F, tn), BF16)]),
        out_shape=jax.ShapeDtypeStruct((n_rows, D // LANES, LANES), F32),
        compiler_params=_cparams("arbitrary"),
        name="expert_down",
    )(*wl, act, w_dn, b_dn)


def _scatter_kernel(dest_ref, x_ref, xs_in_ref, xs_ref, sem):
    del xs_in_ref
    tm = x_ref.shape[0]

    def issue(r, carry):
        for k in range(TOP_K):
            pltpu.make_async_copy(x_ref.at[r], xs_ref.at[dest_ref[r * TOP_K + k]], sem).start()
        return carry

    lax.fori_loop(0, tm, issue, 0)
    for _ in range(TOP_K):
        pltpu.make_async_copy(x_ref, xs_ref.at[pl.ds(0, tm)], sem).wait()


def _scatter_rows(xn2, dest, xs, tm):
    T, nch, _ = xn2.shape
    return pl.pallas_call(
        _scatter_kernel,
        grid=(T // tm,),
        in_specs=[pl.BlockSpec((tm * TOP_K,), lambda i: (i,), memory_space=pltpu.SMEM),
                  pl.BlockSpec((tm, nch, LANES), lambda i: (i, 0, 0)),
                  pl.BlockSpec(memory_space=pl.ANY)],
        out_specs=pl.BlockSpec(memory_space=pl.ANY),
        out_shape=jax.ShapeDtypeStruct(xs.shape, xs.dtype),
        scratch_shapes=[pltpu.SemaphoreType.DMA],
        input_output_aliases={2: 0},
        compiler_params=_cparams("arbitrary"),
        name="scatter_rows",
    )(dest.reshape(T * TOP_K), xn2, xs)


def _combine_kernel(dest_ref, x2_ref, gate_ref, g_ref, out_ref, y_ref, buf, sem):
    tm, D = x2_ref.shape

    def issue(r, carry):
        for k in range(TOP_K):
            pltpu.make_async_copy(out_ref.at[dest_ref[r * TOP_K + k]], buf.at[k, r], sem).start()
        return carry

    lax.fori_loop(0, tm, issue, 0)
    for k in range(TOP_K):
        pltpu.make_async_copy(out_ref.at[pl.ds(0, tm)], buf.at[k], sem).wait()
    gate = gate_ref[...]
    gk = [gate[:, k:k + 1] for k in range(TOP_K)]
    ss = jnp.zeros((tm, 1), F32)
    for c in range(D // LANES):
        cols = slice(c * LANES, (c + 1) * LANES)
        acc = x2_ref[:, cols]
        for k in range(TOP_K):
            acc = acc + gk[k] * buf[k, :, c, :]
        y_ref[:, cols] = acc
        ss = ss + jnp.sum(acc * acc, axis=-1, keepdims=True)
    y_ref[...] = y_ref[...] * lax.rsqrt(ss * (1.0 / D) + EPS) * g_ref[...]


def _combine(x2, gate, dest, out, final_g, tm):
    T, D = x2.shape
    row = lambda w: pl.BlockSpec((tm, w), lambda i: (i, 0))
    return pl.pallas_call(
        _combine_kernel,
        grid=(T // tm,),
        in_specs=[pl.BlockSpec((tm * TOP_K,), lambda i: (i,), memory_space=pltpu.SMEM),
                  row(D), row(LANES), _resident((1, D)), pl.BlockSpec(memory_space=pl.ANY)],
        out_specs=row(D),
        out_shape=jax.ShapeDtypeStruct((T, D), F32),
        scratch_shapes=[pltpu.VMEM((TOP_K, tm, D // LANES, LANES), F32), pltpu.SemaphoreType.DMA],
        compiler_params=_cparams("arbitrary"),
        name="combine_norm",
    )(dest.reshape(T * TOP_K), x2, gate, final_g.reshape(1, D), out)


def _tile(n, pref):
    return pref if n % pref == 0 else n


def _mixers(x, past, lw):
    (norm_mix_g, w_in, b_igate, b_fgate, g_sb_out, g_ml_out) = lw
    B, S, D = x.shape
    T = B * S
    x2d = x.reshape(T, D)
    q, kf, kb, vf, vb, qm, km, vm, om, gt = _in_projection(x2d, norm_mix_g, w_in, b_igate, b_fgate, _tile(T, 256))
    if past is None:
        sb_o = _sb_prompt(q, kb, vb, g_sb_out, B, S, R=min(4, S // KEY_BLOCK))
        C0 = jnp.zeros((B, ML_HEADS, HEAD_DIM, ML_V_DIM), F32)
        n0 = jnp.zeros((B, ML_HEADS, HEAD_DIM), F32)
        m0 = jnp.zeros((B, ML_HEADS), F32)
        L = _tile(S, 256)
    else:
        cache_k, cache_v, C0, n0, m0 = past
        sb_o = _sb_sample(q, kb, vb, cache_k, cache_v, g_sb_out, B, S, cache_k.shape[1])
        L = S
    ml_o, C, n, m = _mlstm(qm, km, vm, om, gt, C0, n0, m0, g_ml_out, B, S, L)
    state = (kf.reshape(B, S, SB_HEADS, HEAD_DIM), vf.reshape(B, S, SB_HEADS, HEAD_DIM),
             C, n.reshape(B, ML_HEADS, HEAD_DIM), m.reshape(B, ML_HEADS))
    return (x2d, sb_o, ml_o), state


MOE_ROW_TILE = 512
MOE_COL_TILE = 1024
COMBINE_TILE = 256


def kernel(x_prompt, x_sample, cache_k, cache_v, state_C, state_n, state_m, norm_mix_g, w_in, b_igate, b_fgate,
           g_sb_out, g_ml_out, w_out, norm_ffn_g, w_router, b_router, w_gate_up, b_gate_up, w_down, b_down,
           final_norm_g):
    assert w_in.shape[0] == 1, "single-layer trunk"
    lw = (norm_mix_g[0], w_in[0], b_igate[0], b_fgate[0], g_sb_out[0], g_ml_out[0])
    E = w_gate_up.shape[1]
    D = x_prompt.shape[-1]
    groups = [_mixers(x_prompt, None, lw),
              _mixers(x_sample, (cache_k[0], cache_v[0], state_C[0], state_n[0], state_m[0]), lw)]

    counts = jnp.zeros((1, LANES), F32)
    routed = []
    for (x2d, sb_o, ml_o), _ in groups:
        x2, xn2, eid, gate, rank, counts = _outproj_router(x2d, sb_o, ml_o, w_out[0], norm_ffn_g[0], w_router[0],
                                                           b_router[0], counts, _tile(x2d.shape[0], 512))
        routed.append((x2, xn2, eid, gate, rank))
    tm = MOE_ROW_TILE
    cnt = counts[0, :N_EXPERTS].astype(I32)
    padded = (cnt + tm - 1) // tm * tm
    pstart = jnp.cumsum(padded) - padded
    n_assign = sum(r[0].shape[0] for r in routed) * TOP_K
    n_rows = (-(-n_assign // tm) + N_EXPERTS) * tm
    dests = [pstart[eid[:, :TOP_K]] + rank[:, :TOP_K] for (_, _, eid, _, rank) in routed]

    xs = jnp.zeros((n_rows, D // 2 // LANES, LANES), jnp.uint32)
    for (x2, xn2, _, _, _), dest in zip(routed, dests):
        xs = _scatter_rows(xn2, dest, xs, _tile(xn2.shape[0], 512))
    out = _expert_ffn(xs, padded // tm, w_gate_up[0], b_gate_up[0].reshape(E, 1, -1), w_down[0],
                      b_down[0].reshape(E, 1, -1), tm, MOE_COL_TILE)
    ys = [_combine(x2, gate, dest, out, final_norm_g, _tile(x2.shape[0], COMBINE_TILE))
          for (x2, _, _, gate, _), dest in zip(routed, dests)]

    (kp, vp, Cp, np_, mp), (ks, vs, Cs, ns, ms) = groups[0][1], groups[1][1]
    return (ys[0].reshape(x_prompt.shape), ys[1].reshape(x_sample.shape),
            kp[None], vp[None], Cp[None], np_[None], mp[None], ks[None], vs[None], Cs[None], ns[None], ms[None])
```

```python
import functools

import jax
import jax.numpy as jnp
from jax import lax
from jax.experimental import pallas as pl
from jax.experimental.pallas import tpu as pltpu

F32 = jnp.float32
BF16 = jnp.bfloat16
I32 = jnp.int32

EPS = 1e-6
SB_HEADS = 8
HEAD_DIM = 128
ML_HEADS = 4
ML_V_DIM = 256
N_EXPERTS = 32
TOP_K = 4
SWIGLU_ALPHA = 1.702
SWIGLU_LIMIT = 7.0
LANES = 128
KEY_BLOCK = 128
NEG_BIG = -1e30
EXP_ZERO_BELOW = -105.0
VMEM_LIMIT = 56 * 1024 * 1024


def _cparams(*sem):
    return pltpu.CompilerParams(dimension_semantics=sem, vmem_limit_bytes=VMEM_LIMIT)


def _resident(shape):
    nd = len(shape)
    return pl.BlockSpec(shape, lambda *_: (0,) * nd, pipeline_mode=pl.Buffered(1))


def _rms(x, g):
    return x * lax.rsqrt(jnp.mean(x * x, axis=-1, keepdims=True) + EPS) * g


def _log_sigmoid(z):
    return jnp.minimum(z, 0.0) - jnp.log(1.0 + jnp.exp(-jnp.abs(z)))


def _split2(x):
    hi = x.astype(BF16)
    lo = (x - hi.astype(F32)).astype(BF16)
    return hi, lo


def _split3(x):
    h1 = x.astype(BF16)
    r = x - h1.astype(F32)
    h2 = r.astype(BF16)
    h3 = (r - h2.astype(F32)).astype(BF16)
    return h1, h2, h3


def _mm(a, b):
    return jnp.dot(a, b, preferred_element_type=F32)


def _inproj_kernel(x_ref, g_ref, wq_ref, wk_ref, wv_ref, wqm_ref, wkm_ref, wvm_ref, wom_ref,
                   wg_ref, bg_ref,
                   q_ref, kf_ref, kb_ref, vf_ref, vb_ref, qm_ref, km_ref, vm_ref, om_ref, gt_ref):
    xn = _rms(x_ref[...], g_ref[...]).astype(BF16)
    q_ref[...] = (_mm(xn, wq_ref[...]) * (HEAD_DIM ** -0.5)).astype(BF16)
    k = _mm(xn, wk_ref[...])
    kf_ref[...] = k
    kb_ref[...] = k.astype(BF16)
    v = _mm(xn, wv_ref[...])
    vf_ref[...] = v
    vb_ref[...] = v.astype(BF16)
    qm_ref[...] = _mm(xn, wqm_ref[...]).astype(BF16)
    km_ref[...] = (_mm(xn, wkm_ref[...]) * (HEAD_DIM ** -0.5)).astype(BF16)
    vm_ref[...] = _mm(xn, wvm_ref[...]).astype(BF16)
    om_ref[...] = _mm(xn, wom_ref[...])
    gpre = _mm(xn, wg_ref[...]) + bg_ref[...]
    lane = lax.broadcasted_iota(I32, gpre.shape, 1)
    is_f = (lane >= ML_HEADS) & (lane < 2 * ML_HEADS)
    gt_ref[...] = jnp.where(is_f, _log_sigmoid(gpre), gpre)


def _in_projection(x2d, norm_g, w_in, b_igate, b_fgate, tm):
    T, D = x2d.shape
    sbw = SB_HEADS * HEAD_DIM
    mqk = ML_HEADS * HEAD_DIM
    mlw = ML_HEADS * ML_V_DIM
    o = 0
    ws = []
    for width in (sbw, sbw, sbw, mqk, mqk, mlw, mlw):
        ws.append(w_in[:, o:o + width].astype(BF16))
        o += width
    wg = jnp.zeros((D, LANES), F32).at[:, :2 * ML_HEADS].set(w_in[:, o:o + 2 * ML_HEADS]).astype(BF16)
    bg = jnp.zeros((1, LANES), F32).at[0, :ML_HEADS].set(b_igate).at[0, ML_HEADS:2 * ML_HEADS].set(b_fgate)
    row = lambda w: pl.BlockSpec((tm, w), lambda i: (i, 0))
    out_widths = (sbw, sbw, sbw, sbw, sbw, mqk, mqk, mlw, mlw, LANES)
    out_dtypes = (BF16, F32, BF16, F32, BF16, BF16, BF16, BF16, F32, F32)
    return pl.pallas_call(
        _inproj_kernel,
        grid=(T // tm,),
        in_specs=[row(D), _resident((1, D))] + [_resident(w.shape) for w in ws]
                 + [_resident(wg.shape), _resident(bg.shape)],
        out_specs=[row(w) for w in out_widths],
        out_shape=[jax.ShapeDtypeStruct((T, w), dt) for w, dt in zip(out_widths, out_dtypes)],
        compiler_params=_cparams("parallel"),
        name="in_projection",
    )(x2d, norm_g.reshape(1, D), *ws, wg, bg)


def _suffix_matrix():
    j = lax.broadcasted_iota(I32, (KEY_BLOCK, 2 * KEY_BLOCK), 0)
    c = lax.broadcasted_iota(I32, (KEY_BLOCK, 2 * KEY_BLOCK), 1)
    return jnp.where((c >= KEY_BLOCK) | (j > c), 1.0, 0.0).astype(BF16)


def _sb_step(q, k, v, carry, umat, mask):
    R, bq, _ = q.shape
    z = jnp.einsum("rqd,rkd->rqk", q, k, preferred_element_type=F32)
    lp = jnp.log(1.0 + jnp.exp(-jnp.abs(z)))
    log_beta = jnp.minimum(z, 0.0) - lp
    log_stay = log_beta - z
    if mask is not None:
        log_stay = jnp.where(mask, log_stay, 0.0)
    hi, lo = _split2(log_stay)
    st = _mm(hi.reshape(R * bq, KEY_BLOCK), umat) + _mm(lo.reshape(R * bq, KEY_BLOCK), umat)
    st = st.reshape(R, bq, 2 * KEY_BLOCK)
    w = jnp.exp(log_beta + st[:, :, :KEY_BLOCK] + carry)
    if mask is not None:
        w = jnp.where(mask, w, 0.0)
    pv = jnp.einsum("rqk,rkd->rqd", w.astype(BF16), v, preferred_element_type=F32)
    return pv, carry + st[:, :, KEY_BLOCK:]


def _sb_prompt_kernel(q_ref, k_ref, v_ref, g_ref, o_ref, acc_ref, carry_ref, *, R):
    qi = pl.program_id(2)
    blk0 = qi * R
    q = q_ref[...].reshape(R, KEY_BLOCK, HEAD_DIM)
    umat = _suffix_matrix()
    shape3 = (R, KEY_BLOCK, KEY_BLOCK)
    t_io = lax.broadcasted_iota(I32, shape3, 1)
    s_io = lax.broadcasted_iota(I32, shape3, 2)
    r_io = lax.broadcasted_iota(I32, shape3, 0)

    def load(ref, d):
        return jnp.stack([ref[pl.ds(pl.multiple_of(jnp.maximum(blk0 + r - d, 0) * KEY_BLOCK, KEY_BLOCK),
                                    KEY_BLOCK), :] for r in range(R)])

    def penalty(d_next):
        return jnp.where(r_io < d_next - blk0, NEG_BIG, 0.0)

    pv, carry = _sb_step(q, load(k_ref, 0), load(v_ref, 0), jnp.zeros(shape3, F32), umat, s_io < t_io)
    acc_ref[...] = pv
    carry = carry + penalty(1)
    carry_ref[...] = carry

    def cond(state):
        d, mx = state
        return (d < blk0 + R) & (mx > EXP_ZERO_BELOW)

    def body(state):
        d, _ = state
        pv, carry = _sb_step(q, load(k_ref, d), load(v_ref, d), carry_ref[...], umat, None)
        acc_ref[...] += pv
        carry = carry + penalty(d + 1)
        carry_ref[...] = carry
        return d + 1, jnp.max(carry)

    lax.while_loop(cond, body, (jnp.int32(1), jnp.max(carry)))
    a = acc_ref[...]
    out = a * lax.rsqrt(jnp.mean(a * a, axis=-1, keepdims=True) + EPS) * g_ref[...]
    o_ref[...] = out.reshape(R * KEY_BLOCK, HEAD_DIM).astype(BF16)


def _sb_prompt(q, k, v, g_sb, B, S, R):
    tq = R * KEY_BLOCK
    nq = S // tq
    return pl.pallas_call(
        functools.partial(_sb_prompt_kernel, R=R),
        grid=(B, SB_HEADS, nq),
        in_specs=[pl.BlockSpec((tq, HEAD_DIM), lambda b, h, i: (b * nq + i, h)),
                  pl.BlockSpec((S, HEAD_DIM), lambda b, h, i: (b, h)),
                  pl.BlockSpec((S, HEAD_DIM), lambda b, h, i: (b, h)),
                  pl.BlockSpec((1, HEAD_DIM), lambda b, h, i: (0, h))],
        out_specs=pl.BlockSpec((tq, HEAD_DIM), lambda b, h, i: (b * nq + i, h)),
        out_shape=jax.ShapeDtypeStruct((B * S, SB_HEADS * HEAD_DIM), BF16),
        scratch_shapes=[pltpu.VMEM((R, KEY_BLOCK, HEAD_DIM), F32),
                        pltpu.VMEM((R, KEY_BLOCK, KEY_BLOCK), F32)],
        compiler_params=_cparams("parallel", "parallel", "parallel"),
        name="sb_prompt",
    )(q, k, v, g_sb.reshape(1, -1))


def _sb_sample_kernel(q_ref, kn_ref, vn_ref, ck_ref, cv_ref, g_ref, o_ref, acc_ref, carry_ref, *, S, P):
    H = SB_HEADS
    hs = lambda h: slice(h * HEAD_DIM, (h + 1) * HEAD_DIM)
    umat = _suffix_matrix()
    q = jnp.stack([q_ref[:, hs(h)] for h in range(H)])
    pad = jnp.zeros((KEY_BLOCK - S, HEAD_DIM), BF16)

    def new_keys(ref):
        return jnp.stack([jnp.concatenate([ref[:, hs(h)], pad], axis=0) for h in range(H)])

    def past_keys(ref, j):
        start = pl.multiple_of(j * KEY_BLOCK, KEY_BLOCK)
        return jnp.stack([ref[pl.ds(start, KEY_BLOCK), hs(h)].astype(BF16) for h in range(H)])

    shape3 = (H, S, KEY_BLOCK)
    t_io = lax.broadcasted_iota(I32, shape3, 1)
    s_io = lax.broadcasted_iota(I32, shape3, 2)
    pv, carry = _sb_step(q, new_keys(kn_ref), new_keys(vn_ref), jnp.zeros(shape3, F32), umat, s_io < t_io)
    acc_ref[...] = pv
    carry_ref[...] = carry

    def cond(state):
        j, mx = state
        return (j >= 0) & (mx > EXP_ZERO_BELOW)

    def body(state):
        j, _ = state
        pv, carry = _sb_step(q, past_keys(ck_ref, j), past_keys(cv_ref, j), carry_ref[...], umat, None)
        acc_ref[...] += pv
        carry_ref[...] = carry
        return j - 1, jnp.max(carry)

    lax.while_loop(cond, body, (jnp.int32(P // KEY_BLOCK - 1), jnp.max(carry)))
    a = acc_ref[...]
    a = a * lax.rsqrt(jnp.mean(a * a, axis=-1, keepdims=True) + EPS)
    for h in range(H):
        o_ref[:, hs(h)] = (a[h] * g_ref[:, hs(h)]).astype(BF16)


def _sb_sample(q, kn, vn, cache_k, cache_v, g_sb, B, S, P):
    W = SB_HEADS * HEAD_DIM
    row = pl.BlockSpec((S, W), lambda b: (b, 0))
    past = pl.BlockSpec((P, W), lambda b: (b, 0))
    return pl.pallas_call(
        functools.partial(_sb_sample_kernel, S=S, P=P),
        grid=(B,),
        in_specs=[row, row, row, past, past, pl.BlockSpec((1, W), lambda b: (0, 0))],
        out_specs=row,
        out_shape=jax.ShapeDtypeStruct((B * S, W), BF16),
        scratch_shapes=[pltpu.VMEM((SB_HEADS, S, HEAD_DIM), F32),
                        pltpu.VMEM((SB_HEADS, S, KEY_BLOCK), F32)],
        compiler_params=_cparams("parallel"),
        name="sb_sample",
    )(q, kn, vn, cache_k.reshape(B * P, W), cache_v.reshape(B * P, W), g_sb.reshape(1, W))


def _mlstm_kernel(q_ref, k_ref, v_ref, o_ref, gt_ref, c0_ref, n0_ref, m0_ref, g_ref,
                  out_ref, c_out_ref, n_out_ref, m_out_ref, cext_ref, m_ref, *, L, Lp):
    c = pl.program_id(1)
    H = ML_HEADS
    VW = ML_V_DIM + LANES
    lane_row = lax.broadcasted_iota(I32, (1, LANES), 1)
    onehot0 = jnp.where(lane_row == 0, 1.0, 0.0)

    @pl.when(c == 0)
    def _():
        for h in range(H):
            cext_ref[h] = jnp.concatenate([c0_ref[0, h], n0_ref[0, h] * onehot0], axis=1)
            m_ref[h] = jnp.broadcast_to(m0_ref[0, h], (8, LANES))

    def pad_rows(a, fill=0.0):
        if Lp == L:
            return a
        return jnp.concatenate([a, jnp.full((Lp - L, a.shape[1]), fill, a.dtype)], axis=0)

    gt = gt_ref[...]
    lane = lax.broadcasted_iota(I32, (Lp, LANES), 1)
    if Lp != L:
        gt = jnp.concatenate([gt, jnp.broadcast_to(jnp.where(lane_row < H, NEG_BIG, 0.0), (Lp - L, LANES))], axis=0)
    lf = jnp.where((lane >= H) & (lane < 2 * H), gt, 0.0)
    ti = lax.broadcasted_iota(I32, (Lp, Lp), 0)
    si = lax.broadcasted_iota(I32, (Lp, Lp), 1)
    causal = si <= ti
    tri = jnp.where(causal, 1.0, 0.0).astype(BF16)
    bc = sum(_mm(tri, p) for p in _split3(lf))
    bc_t = bc.T
    gt_t = gt.T
    ones_blk = jnp.broadcast_to(onehot0, (Lp, LANES)).astype(BF16)

    for h in range(H):
        b_col = bc[:, H + h:H + h + 1]
        b_row = bc_t[H + h:H + h + 1, :]
        i_col = gt[:, h:h + 1]
        i_row = gt_t[h:h + 1, :]
        m_prev = m_ref[h][0:1, 0:1]
        qh = pad_rows(q_ref[:, h * HEAD_DIM:(h + 1) * HEAD_DIM])
        kh = pad_rows(k_ref[:, h * HEAD_DIM:(h + 1) * HEAD_DIM])
        vh = pad_rows(v_ref[:, h * ML_V_DIM:(h + 1) * ML_V_DIM])
        vext = jnp.concatenate([vh, ones_blk], axis=1)
        cext = cext_ref[h]

        log_d = jnp.where(causal, b_col - b_row + i_row, NEG_BIG)
        log_inter = b_col + m_prev
        m_row = jnp.maximum(log_inter, jnp.max(log_d, axis=1, keepdims=True))
        dmat = jnp.exp(log_d - m_row)
        s = lax.dot_general(qh, kh, (((1,), (1,)), ((), ())), preferred_element_type=F32) * dmat
        inter = jnp.exp(log_inter - m_row)
        num = _mm(s.astype(BF16), vext) + inter * _mm(qh, cext.astype(BF16))
        den = num[:, ML_V_DIM:ML_V_DIM + 1]
        hh = num[:L, :ML_V_DIM] / jnp.maximum(jnp.abs(den), jnp.exp(-m_row))[:L]
        hn = hh * lax.rsqrt(jnp.mean(hh * hh, axis=-1, keepdims=True) + EPS)
        cols = slice(h * ML_V_DIM, (h + 1) * ML_V_DIM)
        ogate = 1.0 / (1.0 + jnp.exp(-o_ref[:, cols]))
        out_ref[:, cols] = (ogate * (hn * g_ref[:, cols])).astype(BF16)

        b_last = b_col[Lp - 1:Lp, :]
        log_w = b_last - b_col + i_col
        m_new = jnp.maximum(b_last + m_prev, jnp.max(log_w, axis=0, keepdims=True))
        wk = jnp.exp(log_w - m_new)
        decay = jnp.exp(b_last + m_prev - m_new)
        upd = lax.dot_general(kh, (wk * vext.astype(F32)).astype(BF16), (((0,), (0,)), ((), ())),
                              preferred_element_type=F32)
        cnew = decay * cext + upd
        cext_ref[h] = cnew
        m_ref[h] = jnp.broadcast_to(m_new, (8, LANES))

    @pl.when(c == pl.num_programs(1) - 1)
    def _():
        for h in range(H):
            cf = cext_ref[h]
            c_out_ref[0, h] = cf[:, :ML_V_DIM]
            n_out_ref[0, h] = cf[:, ML_V_DIM:ML_V_DIM + 1]
            m_out_ref[0, h] = m_ref[h][0:1, 0:1]


def _mlstm(qm, km, vm, om, gt, C0, n0, m0, g_ml, B, S, L):
    nc = S // L
    Lp = max(L, LANES)
    H = ML_HEADS
    VW = ML_V_DIM + LANES
    row = lambda w: pl.BlockSpec((L, w), lambda b, c: (b * nc + c, 0))
    st = lambda *tail: pl.BlockSpec((1, H) + tail, lambda b, c: (b, 0, 0, 0))
    return pl.pallas_call(
        functools.partial(_mlstm_kernel, L=L, Lp=Lp),
        grid=(B, nc),
        in_specs=[row(H * HEAD_DIM), row(H * HEAD_DIM), row(H * ML_V_DIM), row(H * ML_V_DIM), row(LANES),
                  st(HEAD_DIM, ML_V_DIM), st(HEAD_DIM, 1), st(1, 1),
                  pl.BlockSpec((1, H * ML_V_DIM), lambda b, c: (0, 0))],
        out_specs=[row(H * ML_V_DIM), st(HEAD_DIM, ML_V_DIM), st(HEAD_DIM, 1), st(1, 1)],
        out_shape=[jax.ShapeDtypeStruct((B * S, H * ML_V_DIM), BF16),
                   jax.ShapeDtypeStruct((B, H, HEAD_DIM, ML_V_DIM), F32),
                   jax.ShapeDtypeStruct((B, H, HEAD_DIM, 1), F32),
                   jax.ShapeDtypeStruct((B, H, 1, 1), F32)],
        scratch_shapes=[pltpu.VMEM((H, HEAD_DIM, VW), F32), pltpu.VMEM((H, 8, LANES), F32)],
        compiler_params=_cparams("parallel", "arbitrary"),
        name="mlstm",
    )(qm, km, vm, om, gt, C0, n0.reshape(B, H, HEAD_DIM, 1), m0.reshape(B, H, 1, 1), g_ml.reshape(1, -1))


def _pack_halves(lo, hi):
    lo_bits = pltpu.bitcast(lo.astype(BF16).astype(F32), jnp.uint32) >> 16
    hi_bits = pltpu.bitcast(hi.astype(BF16).astype(F32), jnp.uint32) & jnp.uint32(0xFFFF0000)
    return lo_bits | hi_bits


def _unpack_halves(w):
    lo = pltpu.bitcast(w << 16, F32).astype(BF16)
    hi = pltpu.bitcast(w & jnp.uint32(0xFFFF0000), F32).astype(BF16)
    return lo, hi


def _outproj_router_kernel(x_ref, sb_ref, ml_ref, wos_ref, wom_ref, g_ref, wr_hi_ref, wr_lo_ref, br_ref, c0_ref,
                           x2_ref, xn_ref, eid_ref, gate_ref, rank_ref, cnt_ref, carry_ref):
    i = pl.program_id(0)

    @pl.when(i == 0)
    def _():
        carry_ref[...] = c0_ref[...]

    x2 = x_ref[...] + _mm(sb_ref[...], wos_ref[...]) + _mm(ml_ref[...], wom_ref[...])
    x2_ref[...] = x2
    xn = _rms(x2, g_ref[...])
    half = xn.shape[1] // 2
    xn_ref[...] = _pack_halves(xn[:, :half], xn[:, half:])
    hi, lo = _split2(xn)
    logits = _mm(hi, wr_hi_ref[...]) + _mm(lo, wr_hi_ref[...]) + _mm(hi, wr_lo_ref[...]) + br_ref[...]
    tm = logits.shape[0]
    lane = lax.broadcasted_iota(I32, (tm, LANES), 1)
    vals, ids = [], []
    cnt = jnp.zeros((tm, LANES), F32)
    for _ in range(TOP_K):
        mx = jnp.max(logits, axis=-1, keepdims=True)
        idx = jnp.min(jnp.where(logits == mx, lane, LANES), axis=-1, keepdims=True)
        sel = lane == idx
        vals.append(mx)
        ids.append(idx)
        logits = jnp.where(sel, -jnp.inf, logits)
        cnt = cnt + jnp.where(sel, 1.0, 0.0)
    es = [jnp.exp(v - vals[0]) for v in vals]
    inv = 1.0 / sum(es)
    ti = lax.broadcasted_iota(I32, (tm, tm), 0)
    si = lax.broadcasted_iota(I32, (tm, tm), 1)
    before = _mm(jnp.where(si < ti, 1.0, 0.0).astype(BF16), cnt.astype(BF16)) + carry_ref[...]
    eid_o = jnp.zeros((tm, LANES), I32)
    gate_o = jnp.zeros((tm, LANES), F32)
    rank_o = jnp.zeros((tm, LANES), I32)
    for k in range(TOP_K):
        rk = jnp.sum(jnp.where(lane == ids[k], before, 0.0), axis=-1, keepdims=True)
        eid_o = jnp.where(lane == k, ids[k], eid_o)
        gate_o = jnp.where(lane == k, es[k] * inv, gate_o)
        rank_o = jnp.where(lane == k, rk.astype(I32), rank_o)
    eid_ref[...] = eid_o
    gate_ref[...] = gate_o
    rank_ref[...] = rank_o
    carry_ref[...] += jnp.sum(cnt, axis=0, keepdims=True)
    cnt_ref[...] = carry_ref[...]


def _outproj_router(x2d, sb_o, ml_o, w_out, norm_g, w_router, b_router, counts_in, tm):
    T, D = x2d.shape
    W = sb_o.shape[1]
    wos = w_out[:W].astype(BF16)
    wom = w_out[W:].astype(BF16)
    wr = jnp.zeros((D, LANES), F32).at[:, :N_EXPERTS].set(w_router)
    wr_hi = wr.astype(BF16)
    wr_lo = (wr - wr_hi.astype(F32)).astype(BF16)
    br = jnp.full((1, LANES), NEG_BIG, F32).at[0, :N_EXPERTS].set(b_router)
    row = lambda w: pl.BlockSpec((tm, w), lambda i: (i, 0))
    return pl.pallas_call(
        _outproj_router_kernel,
        grid=(T // tm,),
        in_specs=[row(D), row(W), row(ml_o.shape[1]), _resident(wos.shape), _resident(wom.shape),
                  _resident((1, D)), _resident(wr_hi.shape), _resident(wr_lo.shape), _resident(br.shape),
                  _resident((1, LANES))],
        out_specs=[row(D), row(D // 2), row(LANES), row(LANES), row(LANES),
                   pl.BlockSpec((1, LANES), lambda i: (0, 0))],
        out_shape=[jax.ShapeDtypeStruct((T, D), F32), jax.ShapeDtypeStruct((T, D // 2), jnp.uint32),
                   jax.ShapeDtypeStruct((T, LANES), I32), jax.ShapeDtypeStruct((T, LANES), F32),
                   jax.ShapeDtypeStruct((T, LANES), I32), jax.ShapeDtypeStruct((1, LANES), F32)],
        scratch_shapes=[pltpu.VMEM((1, LANES), F32)],
        compiler_params=_cparams("arbitrary"),
        name="outproj_router",
    )(x2d, sb_o, ml_o, wos, wom, norm_g.reshape(1, D), wr_hi, wr_lo, br, counts_in)


def _gate_up_kernel(tile_ref, j_ref, e_ref, first_ref, n_ref, x_ref, wg_ref, wl_ref, bg_ref, bl_ref, act_ref,
                    wg_s, wl_s, x_s):
    w = pl.program_id(0)

    @pl.when(w < n_ref[0])
    def _():
        @pl.when(first_ref[w] == 1)
        def _():
            wg_s[...] = wg_ref[0].astype(BF16)
            wl_s[...] = wl_ref[0].astype(BF16)

        half = x_ref.shape[1]
        lo, hi = _unpack_halves(x_ref[...])
        x_s[:, :half] = lo
        x_s[:, half:] = hi
        x = x_s[...]
        glu = jnp.minimum(_mm(x, wg_s[...]) + bg_ref[0], SWIGLU_LIMIT)
        lin = jnp.clip(_mm(x, wl_s[...]) + bl_ref[0], -SWIGLU_LIMIT, SWIGLU_LIMIT)
        act = glu * (1.0 / (1.0 + jnp.exp(-SWIGLU_ALPHA * glu))) * (lin + 1.0)
        act_ref[...] = act.astype(BF16)

    @pl.when(w >= n_ref[0])
    def _():
        act_ref[...] = jnp.zeros_like(act_ref)


def _down_kernel(tile_ref, j_ref, e_ref, first_ref, n_ref, a_ref, w_ref, b_ref, o_ref, w_s):
    w = pl.program_id(0)

    @pl.when(w < n_ref[0])
    def _():
        @pl.when(first_ref[w] == 1)
        def _():
            w_s[...] = w_ref[0].astype(BF16)

        o_ref[...] = _mm(a_ref[...], w_s[...]) + b_ref[0]

    @pl.when(w >= n_ref[0])
    def _():
        o_ref[...] = jnp.zeros_like(o_ref)


def _work_list(tiles_per_expert, nj, n_tiles_max):
    tpe = tiles_per_expert
    tile_start = jnp.cumsum(tpe) - tpe
    item_end = jnp.cumsum(tpe * nj)
    item_start = item_end - tpe * nj
    n_items = item_end[-1:]
    w = jnp.arange(n_tiles_max * nj, dtype=I32)
    wc = jnp.minimum(w, n_items[0] - 1)
    e = jnp.minimum(jnp.sum((wc[:, None] >= item_end[None, :]).astype(I32), axis=1), N_EXPERTS - 1)
    local = wc - item_start[e]
    t = jnp.maximum(tpe[e], 1)
    j = local // t
    il = local - j * t
    real = w < n_items[0]
    first = ((il == 0) & real).astype(I32)
    tail = jnp.maximum(w - n_items[0], 0)
    tile = jnp.where(real, tile_start[e] + il, jnp.sum(tpe) + tail // nj)
    j = jnp.where(real, j, tail % nj)
    return tile.astype(I32), j.astype(I32), e, first, n_items.astype(I32)


def _expert_ffn(xs, tiles_per_expert, w_gu, b_gu, w_dn, b_dn, tm, tn):
    n_rows, half = xs.shape
    D = 2 * half
    F = w_dn.shape[1]
    n_tiles = n_rows // tm
    nj = F // tn
    wl = _work_list(tiles_per_expert, nj, n_tiles)
    act = pl.pallas_call(
        _gate_up_kernel,
        grid_spec=pltpu.PrefetchScalarGridSpec(
            num_scalar_prefetch=5,
            grid=(n_tiles * nj,),
            in_specs=[pl.BlockSpec((tm, half), lambda w, t, j, e, f, n: (t[w], 0)),
                      pl.BlockSpec((1, D, tn), lambda w, t, j, e, f, n: (e[w], 0, j[w])),
                      pl.BlockSpec((1, D, tn), lambda w, t, j, e, f, n: (e[w], 0, nj + j[w])),
                      pl.BlockSpec((1, 1, tn), lambda w, t, j, e, f, n: (e[w], 0, j[w])),
                      pl.BlockSpec((1, 1, tn), lambda w, t, j, e, f, n: (e[w], 0, nj + j[w]))],
            out_specs=pl.BlockSpec((tm, tn), lambda w, t, j, e, f, n: (t[w], j[w])),
            scratch_shapes=[pltpu.VMEM((D, tn), BF16), pltpu.VMEM((D, tn), BF16), pltpu.VMEM((tm, D), BF16)]),
        out_shape=jax.ShapeDtypeStruct((n_rows, F), BF16),
        compiler_params=_cparams("arbitrary"),
        name="expert_gate_up",
    )(*wl, xs, w_gu, w_gu, b_gu, b_gu)

    njd = D // tn
    wl = _work_list(tiles_per_expert, njd, n_tiles)
    return pl.pallas_call(
        _down_kernel,
        grid_spec=pltpu.PrefetchScalarGridSpec(
            num_scalar_prefetch=5,
            grid=(n_tiles * njd,),
            in_specs=[pl.BlockSpec((tm, F), lambda w, t, j, e, f, n: (t[w], 0)),
                      pl.BlockSpec((1, F, tn), lambda w, t, j, e, f, n: (e[w], 0, j[w])),
                      pl.BlockSpec((1, 1, tn), lambda w, t, j, e, f, n: (e[w], 0, j[w]))],
            out_specs=pl.BlockSpec((tm, tn), lambda w, t, j, e, f, n: (t[w], j[w])),
            scratch_shapes=[pltpu.VMEM((F, tn), BF16)]),
        out_shape=jax.ShapeDtypeStruct((n_rows, D), F32),
        compiler_params=_cparams("arbitrary"),
        name="expert_down",
    )(*wl, act, w_dn, b_dn)


def _scatter_kernel(dest_ref, x_ref, xs_in_ref, xs_ref, sem):
    del xs_in_ref
    tm = x_ref.shape[0]

    def issue(r, carry):
        for k in range(TOP_K):
            pltpu.make_async_copy(x_ref.at[pl.ds(r, 1), :], xs_ref.at[pl.ds(dest_ref[r * TOP_K + k], 1), :],
                                  sem).start()
        return carry

    lax.fori_loop(0, tm, issue, 0)
    for _ in range(TOP_K):
        pltpu.make_async_copy(x_ref, xs_ref.at[pl.ds(0, tm), :], sem).wait()


def _scatter_rows(xn2, dest, xs, tm):
    T, half = xn2.shape
    return pl.pallas_call(
        _scatter_kernel,
        grid=(T // tm,),
        in_specs=[pl.BlockSpec((tm * TOP_K,), lambda i: (i,), memory_space=pltpu.SMEM),
                  pl.BlockSpec((tm, half), lambda i: (i, 0)),
                  pl.BlockSpec(memory_space=pl.ANY)],
        out_specs=pl.BlockSpec(memory_space=pl.ANY),
        out_shape=jax.ShapeDtypeStruct(xs.shape, xs.dtype),
        scratch_shapes=[pltpu.SemaphoreType.DMA],
        input_output_aliases={2: 0},
        compiler_params=_cparams("arbitrary"),
        name="scatter_rows",
    )(dest.reshape(T * TOP_K), xn2, xs)


COMBINE_PARTS = 2


def _combine_kernel(dest_ref, x2_ref, gate_ref, g_ref, out_ref, y_ref, buf, sems):
    tm = x2_ref.shape[0]
    rows_per_part = tm // COMBINE_PARTS

    def issue(r, part):
        for k in range(TOP_K):
            pltpu.make_async_copy(out_ref.at[pl.ds(dest_ref[r * TOP_K + k], 1), :], buf.at[k, pl.ds(r, 1), :],
                                  sems.at[part]).start()
        return part

    for part in range(COMBINE_PARTS):
        lax.fori_loop(part * rows_per_part, (part + 1) * rows_per_part, issue, part)
    for part in range(COMBINE_PARTS):
        rows = pl.ds(part * rows_per_part, rows_per_part)
        for k in range(TOP_K):
            pltpu.make_async_copy(out_ref.at[pl.ds(0, rows_per_part), :], buf.at[k, rows, :], sems.at[part]).wait()
        gate = gate_ref[rows, :]
        acc = x2_ref[rows, :]
        for k in range(TOP_K):
            acc = acc + gate[:, k:k + 1] * buf[k, rows, :]
        y_ref[rows, :] = _rms(acc, g_ref[...])


def _combine(x2, gate, dest, out, final_g, tm):
    T, D = x2.shape
    row = lambda w: pl.BlockSpec((tm, w), lambda i: (i, 0))
    return pl.pallas_call(
        _combine_kernel,
        grid=(T // tm,),
        in_specs=[pl.BlockSpec((tm * TOP_K,), lambda i: (i,), memory_space=pltpu.SMEM),
                  row(D), row(LANES), _resident((1, D)), pl.BlockSpec(memory_space=pl.ANY)],
        out_specs=row(D),
        out_shape=jax.ShapeDtypeStruct((T, D), F32),
        scratch_shapes=[pltpu.VMEM((TOP_K, tm, D), F32), pltpu.SemaphoreType.DMA((COMBINE_PARTS,))],
        compiler_params=_cparams("arbitrary"),
        name="combine_norm",
    )(dest.reshape(T * TOP_K), x2, gate, final_g.reshape(1, D), out)


def _tile(n, pref):
    return pref if n % pref == 0 else n


def _mixers(x, past, lw):
    (norm_mix_g, w_in, b_igate, b_fgate, g_sb_out, g_ml_out) = lw
    B, S, D = x.shape
    T = B * S
    x2d = x.reshape(T, D)
    q, kf, kb, vf, vb, qm, km, vm, om, gt = _in_projection(x2d, norm_mix_g, w_in, b_igate, b_fgate, _tile(T, 256))
    if past is None:
        sb_o = _sb_prompt(q, kb, vb, g_sb_out, B, S, R=min(4, S // KEY_BLOCK))
        C0 = jnp.zeros((B, ML_HEADS, HEAD_DIM, ML_V_DIM), F32)
        n0 = jnp.zeros((B, ML_HEADS, HEAD_DIM), F32)
        m0 = jnp.zeros((B, ML_HEADS), F32)
        L = _tile(S, 256)
    else:
        cache_k, cache_v, C0, n0, m0 = past
        sb_o = _sb_sample(q, kb, vb, cache_k, cache_v, g_sb_out, B, S, cache_k.shape[1])
        L = S
    ml_o, C, n, m = _mlstm(qm, km, vm, om, gt, C0, n0, m0, g_ml_out, B, S, L)
    state = (kf.reshape(B, S, SB_HEADS, HEAD_DIM), vf.reshape(B, S, SB_HEADS, HEAD_DIM),
             C, n.reshape(B, ML_HEADS, HEAD_DIM), m.reshape(B, ML_HEADS))
    return (x2d, sb_o, ml_o), state


MOE_ROW_TILE = 512
MOE_COL_TILE = 1024
COMBINE_TILE = 256


def kernel(x_prompt, x_sample, cache_k, cache_v, state_C, state_n, state_m, norm_mix_g, w_in, b_igate, b_fgate,
           g_sb_out, g_ml_out, w_out, norm_ffn_g, w_router, b_router, w_gate_up, b_gate_up, w_down, b_down,
           final_norm_g):
    assert w_in.shape[0] == 1, "single-layer trunk"
    lw = (norm_mix_g[0], w_in[0], b_igate[0], b_fgate[0], g_sb_out[0], g_ml_out[0])
    E = w_gate_up.shape[1]
    D = x_prompt.shape[-1]
    groups = [_mixers(x_prompt, None, lw),
              _mixers(x_sample, (cache_k[0], cache_v[0], state_C[0], state_n[0], state_m[0]), lw)]

    counts = jnp.zeros((1, LANES), F32)
    routed = []
    for (x2d, sb_o, ml_o), _ in groups:
        x2, xn2, eid, gate, rank, counts = _outproj_router(x2d, sb_o, ml_o, w_out[0], norm_ffn_g[0], w_router[0],
                                                           b_router[0], counts, _tile(x2d.shape[0], 512))
        routed.append((x2, xn2, eid, gate, rank))
    tm = MOE_ROW_TILE
    cnt = counts[0, :N_EXPERTS].astype(I32)
    padded = (cnt + tm - 1) // tm * tm
    pstart = jnp.cumsum(padded) - padded
    n_assign = sum(r[0].shape[0] for r in routed) * TOP_K
    n_rows = (-(-n_assign // tm) + N_EXPERTS) * tm
    dests = [pstart[eid[:, :TOP_K]] + rank[:, :TOP_K] for (_, _, eid, _, rank) in routed]

    xs = jnp.zeros((n_rows, D // 2), jnp.uint32)
    for (x2, xn2, _, _, _), dest in zip(routed, dests):
        xs = _scatter_rows(xn2, dest, xs, _tile(xn2.shape[0], 512))
    out = _expert_ffn(xs, padded // tm, w_gate_up[0], b_gate_up[0].reshape(E, 1, -1), w_down[0],
                      b_down[0].reshape(E, 1, -1), tm, MOE_COL_TILE)
    ys = [_combine(x2, gate, dest, out, final_norm_g, _tile(x2.shape[0], COMBINE_TILE))
          for (x2, _, _, gate, _), dest in zip(routed, dests)]

    (kp, vp, Cp, np_, mp), (ks, vs, Cs, ns, ms) = groups[0][1], groups[1][1]
    return (ys[0].reshape(x_prompt.shape), ys[1].reshape(x_sample.shape),
            kp[None], vp[None], Cp[None], np_[None], mp[None], ks[None], vs[None], Cs[None], ns[None], ms[None])
```

```python
import functools

import jax
import jax.numpy as jnp
from jax import lax
from jax.experimental import pallas as pl
from jax.experimental.pallas import tpu as pltpu

F32 = jnp.float32
BF16 = jnp.bfloat16
I32 = jnp.int32

EPS = 1e-6
SB_HEADS = 8
HEAD_DIM = 128
ML_HEADS = 4
ML_V_DIM = 256
N_EXPERTS = 32
TOP_K = 4
SWIGLU_ALPHA = 1.702
SWIGLU_LIMIT = 7.0
LANES = 128
KEY_BLOCK = 128
NEG_BIG = -1e30
EXP_ZERO_BELOW = -105.0
VMEM_LIMIT = 56 * 1024 * 1024


def _cparams(*sem):
    return pltpu.CompilerParams(dimension_semantics=sem, vmem_limit_bytes=VMEM_LIMIT)


def _resident(shape):
    nd = len(shape)
    return pl.BlockSpec(shape, lambda *_: (0,) * nd, pipeline_mode=pl.Buffered(1))


def _rms(x, g):
    return x * lax.rsqrt(jnp.mean(x * x, axis=-1, keepdims=True) + EPS) * g


def _log_sigmoid(z):
    return jnp.minimum(z, 0.0) - jnp.log(1.0 + jnp.exp(-jnp.abs(z)))


def _split2(x):
    hi = x.astype(BF16)
    lo = (x - hi.astype(F32)).astype(BF16)
    return hi, lo


def _split3(x):
    h1 = x.astype(BF16)
    r = x - h1.astype(F32)
    h2 = r.astype(BF16)
    h3 = (r - h2.astype(F32)).astype(BF16)
    return h1, h2, h3


def _mm(a, b):
    return jnp.dot(a, b, preferred_element_type=F32)


def _inproj_kernel(x_ref, g_ref, wq_ref, wk_ref, wv_ref, wqm_ref, wkm_ref, wvm_ref, wom_ref,
                   wg_ref, bg_ref,
                   q_ref, kf_ref, kb_ref, vf_ref, vb_ref, qm_ref, km_ref, vm_ref, om_ref, gt_ref):
    xn = _rms(x_ref[...], g_ref[...]).astype(BF16)
    q_ref[...] = (_mm(xn, wq_ref[...]) * (HEAD_DIM ** -0.5)).astype(BF16)
    k = _mm(xn, wk_ref[...])
    kf_ref[...] = k
    kb_ref[...] = k.astype(BF16)
    v = _mm(xn, wv_ref[...])
    vf_ref[...] = v
    vb_ref[...] = v.astype(BF16)
    qm_ref[...] = _mm(xn, wqm_ref[...]).astype(BF16)
    km_ref[...] = (_mm(xn, wkm_ref[...]) * (HEAD_DIM ** -0.5)).astype(BF16)
    vm_ref[...] = _mm(xn, wvm_ref[...]).astype(BF16)
    om_ref[...] = _mm(xn, wom_ref[...])
    gpre = _mm(xn, wg_ref[...]) + bg_ref[...]
    lane = lax.broadcasted_iota(I32, gpre.shape, 1)
    is_f = (lane >= ML_HEADS) & (lane < 2 * ML_HEADS)
    gt_ref[...] = jnp.where(is_f, _log_sigmoid(gpre), gpre)


def _in_projection(x2d, norm_g, w_in, b_igate, b_fgate, tm):
    T, D = x2d.shape
    sbw = SB_HEADS * HEAD_DIM
    mqk = ML_HEADS * HEAD_DIM
    mlw = ML_HEADS * ML_V_DIM
    w_bf = w_in.astype(BF16)
    o = 0
    w_specs = []
    for width in (sbw, sbw, sbw, mqk, mqk, mlw, mlw):
        assert o % width == 0
        w_specs.append(pl.BlockSpec((D, width), functools.partial(lambda blk, i: (0, blk), o // width),
                                    pipeline_mode=pl.Buffered(1)))
        o += width
    wg = jnp.zeros((D, LANES), F32).at[:, :2 * ML_HEADS].set(w_in[:, o:o + 2 * ML_HEADS]).astype(BF16)
    bg = jnp.zeros((1, LANES), F32).at[0, :ML_HEADS].set(b_igate).at[0, ML_HEADS:2 * ML_HEADS].set(b_fgate)
    row = lambda w: pl.BlockSpec((tm, w), lambda i: (i, 0))
    out_widths = (sbw, sbw, sbw, sbw, sbw, mqk, mqk, mlw, mlw, LANES)
    out_dtypes = (BF16, F32, BF16, F32, BF16, BF16, BF16, BF16, F32, F32)
    return pl.pallas_call(
        _inproj_kernel,
        grid=(T // tm,),
        in_specs=[row(D), _resident((1, D))] + w_specs + [_resident(wg.shape), _resident(bg.shape)],
        out_specs=[row(w) for w in out_widths],
        out_shape=[jax.ShapeDtypeStruct((T, w), dt) for w, dt in zip(out_widths, out_dtypes)],
        compiler_params=_cparams("parallel"),
        name="in_projection",
    )(x2d, norm_g.reshape(1, D), *([w_bf] * len(w_specs)), wg, bg)


def _suffix_matrix():
    j = lax.broadcasted_iota(I32, (KEY_BLOCK, 2 * KEY_BLOCK), 0)
    c = lax.broadcasted_iota(I32, (KEY_BLOCK, 2 * KEY_BLOCK), 1)
    return jnp.where((c >= KEY_BLOCK) | (j > c), 1.0, 0.0).astype(BF16)


def _sb_step(q, k, v, carry, umat, mask):
    R, bq, _ = q.shape
    z = jnp.einsum("rqd,rkd->rqk", q, k, preferred_element_type=F32)
    lp = jnp.log(1.0 + jnp.exp(-jnp.abs(z)))
    log_beta = jnp.minimum(z, 0.0) - lp
    log_stay = log_beta - z
    if mask is not None:
        log_stay = jnp.where(mask, log_stay, 0.0)
    hi, lo = _split2(log_stay)
    st = _mm(hi.reshape(R * bq, KEY_BLOCK), umat) + _mm(lo.reshape(R * bq, KEY_BLOCK), umat)
    st = st.reshape(R, bq, 2 * KEY_BLOCK)
    w = jnp.exp(log_beta + st[:, :, :KEY_BLOCK] + carry)
    if mask is not None:
        w = jnp.where(mask, w, 0.0)
    pv = jnp.einsum("rqk,rkd->rqd", w.astype(BF16), v, preferred_element_type=F32)
    return pv, carry + st[:, :, KEY_BLOCK:]


def _sb_prompt_kernel(q_ref, k_ref, v_ref, g_ref, o_ref, acc_ref, carry_ref, *, R):
    qi = pl.program_id(2)
    blk0 = qi * R
    q = q_ref[...].reshape(R, KEY_BLOCK, HEAD_DIM)
    umat = _suffix_matrix()
    shape3 = (R, KEY_BLOCK, KEY_BLOCK)
    t_io = lax.broadcasted_iota(I32, shape3, 1)
    s_io = lax.broadcasted_iota(I32, shape3, 2)
    r_io = lax.broadcasted_iota(I32, shape3, 0)

    def load(ref, d):
        return jnp.stack([ref[pl.ds(pl.multiple_of(jnp.maximum(blk0 + r - d, 0) * KEY_BLOCK, KEY_BLOCK),
                                    KEY_BLOCK), :] for r in range(R)])

    def penalty(d_next):
        return jnp.where(r_io < d_next - blk0, NEG_BIG, 0.0)

    pv, carry = _sb_step(q, load(k_ref, 0), load(v_ref, 0), jnp.zeros(shape3, F32), umat, s_io < t_io)
    acc_ref[...] = pv
    carry = carry + penalty(1)
    carry_ref[...] = carry

    def cond(state):
        d, mx = state
        return (d < blk0 + R) & (mx > EXP_ZERO_BELOW)

    def body(state):
        d, _ = state
        pv, carry = _sb_step(q, load(k_ref, d), load(v_ref, d), carry_ref[...], umat, None)
        acc_ref[...] += pv
        carry = carry + penalty(d + 1)
        carry_ref[...] = carry
        return d + 1, jnp.max(carry)

    lax.while_loop(cond, body, (jnp.int32(1), jnp.max(carry)))
    a = acc_ref[...]
    out = a * lax.rsqrt(jnp.mean(a * a, axis=-1, keepdims=True) + EPS) * g_ref[...]
    o_ref[...] = out.reshape(R * KEY_BLOCK, HEAD_DIM).astype(BF16)


def _sb_prompt(q, k, v, g_sb, B, S, R):
    tq = R * KEY_BLOCK
    nq = S // tq
    return pl.pallas_call(
        functools.partial(_sb_prompt_kernel, R=R),
        grid=(B, SB_HEADS, nq),
        in_specs=[pl.BlockSpec((tq, HEAD_DIM), lambda b, h, i: (b * nq + i, h)),
                  pl.BlockSpec((S, HEAD_DIM), lambda b, h, i: (b, h)),
                  pl.BlockSpec((S, HEAD_DIM), lambda b, h, i: (b, h)),
                  pl.BlockSpec((1, HEAD_DIM), lambda b, h, i: (0, h))],
        out_specs=pl.BlockSpec((tq, HEAD_DIM), lambda b, h, i: (b * nq + i, h)),
        out_shape=jax.ShapeDtypeStruct((B * S, SB_HEADS * HEAD_DIM), BF16),
        scratch_shapes=[pltpu.VMEM((R, KEY_BLOCK, HEAD_DIM), F32),
                        pltpu.VMEM((R, KEY_BLOCK, KEY_BLOCK), F32)],
        compiler_params=_cparams("parallel", "parallel", "parallel"),
        name="sb_prompt",
    )(q, k, v, g_sb.reshape(1, -1))


def _sb_sample_kernel(q_ref, kn_ref, vn_ref, ck_ref, cv_ref, g_ref, o_ref, acc_ref, carry_ref, *, S, P):
    H = SB_HEADS
    hs = lambda h: slice(h * HEAD_DIM, (h + 1) * HEAD_DIM)
    umat = _suffix_matrix()
    q = jnp.stack([q_ref[:, hs(h)] for h in range(H)])
    pad = jnp.zeros((KEY_BLOCK - S, HEAD_DIM), BF16)

    def new_keys(ref):
        return jnp.stack([jnp.concatenate([ref[:, hs(h)], pad], axis=0) for h in range(H)])

    def past_keys(ref, j):
        start = pl.multiple_of(j * KEY_BLOCK, KEY_BLOCK)
        return jnp.stack([ref[pl.ds(start, KEY_BLOCK), hs(h)].astype(BF16) for h in range(H)])

    shape3 = (H, S, KEY_BLOCK)
    t_io = lax.broadcasted_iota(I32, shape3, 1)
    s_io = lax.broadcasted_iota(I32, shape3, 2)
    pv, carry = _sb_step(q, new_keys(kn_ref), new_keys(vn_ref), jnp.zeros(shape3, F32), umat, s_io < t_io)
    acc_ref[...] = pv
    carry_ref[...] = carry

    def cond(state):
        j, mx = state
        return (j >= 0) & (mx > EXP_ZERO_BELOW)

    def body(state):
        j, _ = state
        pv, carry = _sb_step(q, past_keys(ck_ref, j), past_keys(cv_ref, j), carry_ref[...], umat, None)
        acc_ref[...] += pv
        carry_ref[...] = carry
        return j - 1, jnp.max(carry)

    lax.while_loop(cond, body, (jnp.int32(P // KEY_BLOCK - 1), jnp.max(carry)))
    a = acc_ref[...]
    a = a * lax.rsqrt(jnp.mean(a * a, axis=-1, keepdims=True) + EPS)
    for h in range(H):
        o_ref[:, hs(h)] = (a[h] * g_ref[:, hs(h)]).astype(BF16)


def _sb_sample(q, kn, vn, cache_k, cache_v, g_sb, B, S, P):
    W = SB_HEADS * HEAD_DIM
    row = pl.BlockSpec((S, W), lambda b: (b, 0))
    past = pl.BlockSpec((P, W), lambda b: (b, 0))
    return pl.pallas_call(
        functools.partial(_sb_sample_kernel, S=S, P=P),
        grid=(B,),
        in_specs=[row, row, row, past, past, pl.BlockSpec((1, W), lambda b: (0, 0))],
        out_specs=row,
        out_shape=jax.ShapeDtypeStruct((B * S, W), BF16),
        scratch_shapes=[pltpu.VMEM((SB_HEADS, S, HEAD_DIM), F32),
                        pltpu.VMEM((SB_HEADS, S, KEY_BLOCK), F32)],
        compiler_params=_cparams("parallel"),
        name="sb_sample",
    )(q, kn, vn, cache_k.reshape(B * P, W), cache_v.reshape(B * P, W), g_sb.reshape(1, W))


def _mlstm_kernel(q_ref, k_ref, v_ref, o_ref, gt_ref, c0_ref, n0_ref, m0_ref, g_ref,
                  out_ref, c_out_ref, n_out_ref, m_out_ref, cext_ref, m_ref, *, L, Lp):
    c = pl.program_id(1)
    H = ML_HEADS
    VW = ML_V_DIM + LANES
    lane_row = lax.broadcasted_iota(I32, (1, LANES), 1)
    onehot0 = jnp.where(lane_row == 0, 1.0, 0.0)

    @pl.when(c == 0)
    def _():
        for h in range(H):
            cext_ref[h] = jnp.concatenate([c0_ref[0, h], n0_ref[0, h] * onehot0], axis=1)
            m_ref[h] = jnp.broadcast_to(m0_ref[0, h], (8, LANES))

    def pad_rows(a, fill=0.0):
        if Lp == L:
            return a
        return jnp.concatenate([a, jnp.full((Lp - L, a.shape[1]), fill, a.dtype)], axis=0)

    gt = gt_ref[...]
    lane = lax.broadcasted_iota(I32, (Lp, LANES), 1)
    if Lp != L:
        gt = jnp.concatenate([gt, jnp.broadcast_to(jnp.where(lane_row < H, NEG_BIG, 0.0), (Lp - L, LANES))], axis=0)
    lf = jnp.where((lane >= H) & (lane < 2 * H), gt, 0.0)
    ti = lax.broadcasted_iota(I32, (Lp, Lp), 0)
    si = lax.broadcasted_iota(I32, (Lp, Lp), 1)
    causal = si <= ti
    tri = jnp.where(causal, 1.0, 0.0).astype(BF16)
    bc = sum(_mm(tri, p) for p in _split3(lf))
    bc_t = bc.T
    gt_t = gt.T
    ones_blk = jnp.broadcast_to(onehot0, (Lp, LANES)).astype(BF16)

    for h in range(H):
        b_col = bc[:, H + h:H + h + 1]
        b_row = bc_t[H + h:H + h + 1, :]
        i_col = gt[:, h:h + 1]
        i_row = gt_t[h:h + 1, :]
        m_prev = m_ref[h][0:1, 0:1]
        qh = pad_rows(q_ref[:, h * HEAD_DIM:(h + 1) * HEAD_DIM])
        kh = pad_rows(k_ref[:, h * HEAD_DIM:(h + 1) * HEAD_DIM])
        vh = pad_rows(v_ref[:, h * ML_V_DIM:(h + 1) * ML_V_DIM])
        vext = jnp.concatenate([vh, ones_blk], axis=1)
        cext = cext_ref[h]

        log_d = jnp.where(causal, b_col - b_row + i_row, NEG_BIG)
        log_inter = b_col + m_prev
        m_row = jnp.maximum(log_inter, jnp.max(log_d, axis=1, keepdims=True))
        dmat = jnp.exp(log_d - m_row)
        s = lax.dot_general(qh, kh, (((1,), (1,)), ((), ())), preferred_element_type=F32) * dmat
        inter = jnp.exp(log_inter - m_row)
        num = _mm(s.astype(BF16), vext) + inter * _mm(qh, cext.astype(BF16))
        den = num[:, ML_V_DIM:ML_V_DIM + 1]
        hh = num[:L, :ML_V_DIM] / jnp.maximum(jnp.abs(den), jnp.exp(-m_row))[:L]
        hn = hh * lax.rsqrt(jnp.mean(hh * hh, axis=-1, keepdims=True) + EPS)
        cols = slice(h * ML_V_DIM, (h + 1) * ML_V_DIM)
        ogate = 1.0 / (1.0 + jnp.exp(-o_ref[:, cols]))
        out_ref[:, cols] = (ogate * (hn * g_ref[:, cols])).astype(BF16)

        b_last = b_col[Lp - 1:Lp, :]
        log_w = b_last - b_col + i_col
        m_new = jnp.maximum(b_last + m_prev, jnp.max(log_w, axis=0, keepdims=True))
        wk = jnp.exp(log_w - m_new)
        decay = jnp.exp(b_last + m_prev - m_new)
        upd = lax.dot_general(kh, (wk * vext.astype(F32)).astype(BF16), (((0,), (0,)), ((), ())),
                              preferred_element_type=F32)
        cnew = decay * cext + upd
        cext_ref[h] = cnew
        m_ref[h] = jnp.broadcast_to(m_new, (8, LANES))

    @pl.when(c == pl.num_programs(1) - 1)
    def _():
        for h in range(H):
            cf = cext_ref[h]
            c_out_ref[0, h] = cf[:, :ML_V_DIM]
            n_out_ref[0, h] = cf[:, ML_V_DIM:ML_V_DIM + 1]
            m_out_ref[0, h] = m_ref[h][0:1, 0:1]


def _mlstm(qm, km, vm, om, gt, C0, n0, m0, g_ml, B, S, L):
    nc = S // L
    Lp = max(L, LANES)
    H = ML_HEADS
    VW = ML_V_DIM + LANES
    row = lambda w: pl.BlockSpec((L, w), lambda b, c: (b * nc + c, 0))
    st = lambda *tail: pl.BlockSpec((1, H) + tail, lambda b, c: (b, 0, 0, 0))
    return pl.pallas_call(
        functools.partial(_mlstm_kernel, L=L, Lp=Lp),
        grid=(B, nc),
        in_specs=[row(H * HEAD_DIM), row(H * HEAD_DIM), row(H * ML_V_DIM), row(H * ML_V_DIM), row(LANES),
                  st(HEAD_DIM, ML_V_DIM), st(HEAD_DIM, 1), st(1, 1),
                  pl.BlockSpec((1, H * ML_V_DIM), lambda b, c: (0, 0))],
        out_specs=[row(H * ML_V_DIM), st(HEAD_DIM, ML_V_DIM), st(HEAD_DIM, 1), st(1, 1)],
        out_shape=[jax.ShapeDtypeStruct((B * S, H * ML_V_DIM), BF16),
                   jax.ShapeDtypeStruct((B, H, HEAD_DIM, ML_V_DIM), F32),
                   jax.ShapeDtypeStruct((B, H, HEAD_DIM, 1), F32),
                   jax.ShapeDtypeStruct((B, H, 1, 1), F32)],
        scratch_shapes=[pltpu.VMEM((H, HEAD_DIM, VW), F32), pltpu.VMEM((H, 8, LANES), F32)],
        compiler_params=_cparams("parallel", "arbitrary"),
        name="mlstm",
    )(qm, km, vm, om, gt, C0, n0.reshape(B, H, HEAD_DIM, 1), m0.reshape(B, H, 1, 1), g_ml.reshape(1, -1))


def _pack_halves(lo, hi):
    lo_bits = pltpu.bitcast(lo.astype(BF16).astype(F32), jnp.uint32) >> 16
    hi_bits = pltpu.bitcast(hi.astype(BF16).astype(F32), jnp.uint32) & jnp.uint32(0xFFFF0000)
    return lo_bits | hi_bits


def _unpack_halves(w):
    lo = pltpu.bitcast(w << 16, F32).astype(BF16)
    hi = pltpu.bitcast(w & jnp.uint32(0xFFFF0000), F32).astype(BF16)
    return lo, hi


def _outproj_router_kernel(x_ref, sb_ref, ml_ref, wos_ref, wom_ref, g_ref, wr_hi_ref, wr_lo_ref, br_ref, c0_ref,
                           x2_ref, xn_ref, eid_ref, gate_ref, rank_ref, cnt_ref, carry_ref):
    i = pl.program_id(0)

    @pl.when(i == 0)
    def _():
        carry_ref[...] = c0_ref[...]

    x2 = x_ref[...] + _mm(sb_ref[...], wos_ref[...]) + _mm(ml_ref[...], wom_ref[...])
    x2_ref[...] = x2
    xn = _rms(x2, g_ref[...])
    half = xn.shape[1] // 2
    xn_ref[...] = _pack_halves(xn[:, :half], xn[:, half:])
    hi, lo = _split2(xn)
    logits = _mm(hi, wr_hi_ref[...]) + _mm(lo, wr_hi_ref[...]) + _mm(hi, wr_lo_ref[...]) + br_ref[...]
    tm = logits.shape[0]
    lane = lax.broadcasted_iota(I32, (tm, LANES), 1)
    vals, ids = [], []
    cnt = jnp.zeros((tm, LANES), F32)
    for _ in range(TOP_K):
        mx = jnp.max(logits, axis=-1, keepdims=True)
        idx = jnp.min(jnp.where(logits == mx, lane, LANES), axis=-1, keepdims=True)
        sel = lane == idx
        vals.append(mx)
        ids.append(idx)
        logits = jnp.where(sel, -jnp.inf, logits)
        cnt = cnt + jnp.where(sel, 1.0, 0.0)
    es = [jnp.exp(v - vals[0]) for v in vals]
    inv = 1.0 / sum(es)
    ti = lax.broadcasted_iota(I32, (tm, tm), 0)
    si = lax.broadcasted_iota(I32, (tm, tm), 1)
    before = _mm(jnp.where(si < ti, 1.0, 0.0).astype(BF16), cnt.astype(BF16)) + carry_ref[...]
    eid_o = jnp.zeros((tm, LANES), I32)
    gate_o = jnp.zeros((tm, LANES), F32)
    rank_o = jnp.zeros((tm, LANES), I32)
    for k in range(TOP_K):
        rk = jnp.sum(jnp.where(lane == ids[k], before, 0.0), axis=-1, keepdims=True)
        eid_o = jnp.where(lane == k, ids[k], eid_o)
        gate_o = jnp.where(lane == k, es[k] * inv, gate_o)
        rank_o = jnp.where(lane == k, rk.astype(I32), rank_o)
    eid_ref[...] = eid_o
    gate_ref[...] = gate_o
    rank_ref[...] = rank_o
    carry_ref[...] += jnp.sum(cnt, axis=0, keepdims=True)
    cnt_ref[...] = carry_ref[...]


def _outproj_router(x2d, sb_o, ml_o, w_out, norm_g, w_router, b_router, counts_in, tm):
    T, D = x2d.shape
    W = sb_o.shape[1]
    assert w_out.shape[0] == 2 * W
    w_bf = w_out.astype(BF16)
    half_w = lambda blk: pl.BlockSpec((W, D), lambda i: (blk, 0), pipeline_mode=pl.Buffered(1))
    wr = jnp.zeros((D, LANES), F32).at[:, :N_EXPERTS].set(w_router)
    wr_hi = wr.astype(BF16)
    wr_lo = (wr - wr_hi.astype(F32)).astype(BF16)
    br = jnp.full((1, LANES), NEG_BIG, F32).at[0, :N_EXPERTS].set(b_router)
    row = lambda w: pl.BlockSpec((tm, w), lambda i: (i, 0))
    return pl.pallas_call(
        _outproj_router_kernel,
        grid=(T // tm,),
        in_specs=[row(D), row(W), row(W), half_w(0), half_w(1),
                  _resident((1, D)), _resident(wr_hi.shape), _resident(wr_lo.shape), _resident(br.shape),
                  _resident((1, LANES))],
        out_specs=[row(D), row(D // 2), row(LANES), row(LANES), row(LANES),
                   pl.BlockSpec((1, LANES), lambda i: (0, 0))],
        out_shape=[jax.ShapeDtypeStruct((T, D), F32), jax.ShapeDtypeStruct((T, D // 2), jnp.uint32),
                   jax.ShapeDtypeStruct((T, LANES), I32), jax.ShapeDtypeStruct((T, LANES), F32),
                   jax.ShapeDtypeStruct((T, LANES), I32), jax.ShapeDtypeStruct((1, LANES), F32)],
        scratch_shapes=[pltpu.VMEM((1, LANES), F32)],
        compiler_params=_cparams("arbitrary"),
        name="outproj_router",
    )(x2d, sb_o, ml_o, w_bf, w_bf, norm_g.reshape(1, D), wr_hi, wr_lo, br, counts_in)


def _gate_up_kernel(tile_ref, j_ref, e_ref, first_ref, n_ref, x_ref, wg_ref, wl_ref, bg_ref, bl_ref, act_ref,
                    wg_s, wl_s, x_s):
    w = pl.program_id(0)

    @pl.when(w < n_ref[0])
    def _():
        @pl.when(first_ref[w] == 1)
        def _():
            wg_s[...] = wg_ref[0].astype(BF16)
            wl_s[...] = wl_ref[0].astype(BF16)

        half = x_ref.shape[1]
        lo, hi = _unpack_halves(x_ref[...])
        x_s[:, :half] = lo
        x_s[:, half:] = hi
        x = x_s[...]
        glu = jnp.minimum(_mm(x, wg_s[...]) + bg_ref[0], SWIGLU_LIMIT)
        lin = jnp.clip(_mm(x, wl_s[...]) + bl_ref[0], -SWIGLU_LIMIT, SWIGLU_LIMIT)
        act = glu * (1.0 / (1.0 + jnp.exp(-SWIGLU_ALPHA * glu))) * (lin + 1.0)
        act_ref[...] = act.astype(BF16)

    @pl.when(w >= n_ref[0])
    def _():
        act_ref[...] = jnp.zeros_like(act_ref)


def _down_kernel(tile_ref, j_ref, e_ref, first_ref, n_ref, a_ref, w_ref, b_ref, o_ref, w_s):
    w = pl.program_id(0)

    @pl.when(w < n_ref[0])
    def _():
        @pl.when(first_ref[w] == 1)
        def _():
            w_s[...] = w_ref[0].astype(BF16)

        out = _mm(a_ref[...], w_s[...]) + b_ref[0]
        half = out.shape[1] // 2
        o_ref[...] = _pack_halves(out[:, :half], out[:, half:])

    @pl.when(w >= n_ref[0])
    def _():
        o_ref[...] = jnp.zeros_like(o_ref)


def _work_list(tiles_per_expert, nj, n_tiles_max):
    tpe = tiles_per_expert
    tile_start = jnp.cumsum(tpe) - tpe
    item_end = jnp.cumsum(tpe * nj)
    item_start = item_end - tpe * nj
    n_items = item_end[-1:]
    w = jnp.arange(n_tiles_max * nj, dtype=I32)
    wc = jnp.minimum(w, n_items[0] - 1)
    e = jnp.minimum(jnp.sum((wc[:, None] >= item_end[None, :]).astype(I32), axis=1), N_EXPERTS - 1)
    local = wc - item_start[e]
    t = jnp.maximum(tpe[e], 1)
    j = local // t
    il = local - j * t
    real = w < n_items[0]
    first = ((il == 0) & real).astype(I32)
    tail = jnp.maximum(w - n_items[0], 0)
    tile = jnp.where(real, tile_start[e] + il, jnp.sum(tpe) + tail // nj)
    j = jnp.where(real, j, tail % nj)
    return tile.astype(I32), j.astype(I32), e, first, n_items.astype(I32)


def _expert_ffn(xs, tiles_per_expert, w_gu, b_gu, w_dn, b_dn, tm, tn):
    n_rows, half = xs.shape
    D = 2 * half
    F = w_dn.shape[1]
    n_tiles = n_rows // tm
    nj = F // tn
    wl = _work_list(tiles_per_expert, nj, n_tiles)
    act = pl.pallas_call(
        _gate_up_kernel,
        grid_spec=pltpu.PrefetchScalarGridSpec(
            num_scalar_prefetch=5,
            grid=(n_tiles * nj,),
            in_specs=[pl.BlockSpec((tm, half), lambda w, t, j, e, f, n: (t[w], 0)),
                      pl.BlockSpec((1, D, tn), lambda w, t, j, e, f, n: (e[w], 0, j[w])),
                      pl.BlockSpec((1, D, tn), lambda w, t, j, e, f, n: (e[w], 0, nj + j[w])),
                      pl.BlockSpec((1, 1, tn), lambda w, t, j, e, f, n: (e[w], 0, j[w])),
                      pl.BlockSpec((1, 1, tn), lambda w, t, j, e, f, n: (e[w], 0, nj + j[w]))],
            out_specs=pl.BlockSpec((tm, tn), lambda w, t, j, e, f, n: (t[w], j[w])),
            scratch_shapes=[pltpu.VMEM((D, tn), BF16), pltpu.VMEM((D, tn), BF16), pltpu.VMEM((tm, D), BF16)]),
        out_shape=jax.ShapeDtypeStruct((n_rows, F), BF16),
        compiler_params=_cparams("arbitrary"),
        name="expert_gate_up",
    )(*wl, xs, w_gu, w_gu, b_gu, b_gu)

    njd = D // tn
    wl = _work_list(tiles_per_expert, njd, n_tiles)
    return pl.pallas_call(
        _down_kernel,
        grid_spec=pltpu.PrefetchScalarGridSpec(
            num_scalar_prefetch=5,
            grid=(n_tiles * njd,),
            in_specs=[pl.BlockSpec((tm, F), lambda w, t, j, e, f, n: (t[w], 0)),
                      pl.BlockSpec((1, F, tn), lambda w, t, j, e, f, n: (e[w], 0, j[w])),
                      pl.BlockSpec((1, 1, tn), lambda w, t, j, e, f, n: (e[w], 0, j[w]))],
            out_specs=pl.BlockSpec((tm, tn // 2), lambda w, t, j, e, f, n: (t[w], j[w])),
            scratch_shapes=[pltpu.VMEM((F, tn), BF16)]),
        out_shape=jax.ShapeDtypeStruct((n_rows, D // 2), jnp.uint32),
        compiler_params=_cparams("arbitrary"),
        name="expert_down",
    )(*wl, act, w_dn, b_dn)


def _scatter_kernel(zero_ref, dest_ref, xa_ref, xb_ref, xs_ref, zbuf, sem, zsem, *, tiles_a, slot_tile):
    i = pl.program_id(0)
    tm = xa_ref.shape[0]

    @pl.when(i == 0)
    def _():
        zbuf[...] = jnp.zeros_like(zbuf)

        def tile_copy(z):
            start = pl.multiple_of(zero_ref[1 + z] * slot_tile, slot_tile)
            return pltpu.make_async_copy(zbuf, xs_ref.at[pl.ds(start, slot_tile), :], zsem)

        def start(z, carry):
            tile_copy(z).start()
            return carry

        def wait(z, carry):
            tile_copy(z).wait()
            return carry

        lax.fori_loop(0, zero_ref[0], start, 0)
        lax.fori_loop(0, zero_ref[0], wait, 0)

    def scatter_from(x_ref):
        def issue(r, carry):
            for k in range(TOP_K):
                pltpu.make_async_copy(x_ref.at[pl.ds(r, 1), :], xs_ref.at[pl.ds(dest_ref[r * TOP_K + k], 1), :],
                                      sem).start()
            return carry

        lax.fori_loop(0, tm, issue, 0)
        for _ in range(TOP_K):
            pltpu.make_async_copy(x_ref, xs_ref.at[pl.ds(0, tm), :], sem).wait()

    @pl.when(i < tiles_a)
    def _():
        scatter_from(xa_ref)

    @pl.when(i >= tiles_a)
    def _():
        scatter_from(xb_ref)


def _scatter_rows(xa, xb, dest, zero_list, n_rows, tm, slot_tile):
    Ta, half = xa.shape
    Tb = xb.shape[0]
    tiles_a, tiles_b = Ta // tm, Tb // tm
    return pl.pallas_call(
        functools.partial(_scatter_kernel, tiles_a=tiles_a, slot_tile=slot_tile),
        grid_spec=pltpu.PrefetchScalarGridSpec(
            num_scalar_prefetch=1,
            grid=(tiles_a + tiles_b,),
            in_specs=[pl.BlockSpec((tm * TOP_K,), lambda i, z: (i,), memory_space=pltpu.SMEM),
                      pl.BlockSpec((tm, half), lambda i, z: (jnp.minimum(i, tiles_a - 1), 0)),
                      pl.BlockSpec((tm, half), lambda i, z: (jnp.maximum(i - tiles_a, 0), 0))],
            out_specs=pl.BlockSpec(memory_space=pl.ANY),
            scratch_shapes=[pltpu.VMEM((slot_tile, half), jnp.uint32), pltpu.SemaphoreType.DMA,
                            pltpu.SemaphoreType.DMA]),
        out_shape=jax.ShapeDtypeStruct((n_rows, half), jnp.uint32),
        compiler_params=_cparams("arbitrary"),
        name="scatter_rows",
    )(zero_list, dest.reshape((Ta + Tb) * TOP_K), xa, xb)


COMBINE_PARTS = 2


def _combine_kernel(dest_ref, x2_ref, gate_ref, g_ref, out_ref, y_ref, buf, sems, *, col_tile):
    tm, D = x2_ref.shape
    hw = col_tile // 2

    def unpack(w):
        lo = pltpu.bitcast(w << 16, F32)
        hi = pltpu.bitcast(w & jnp.uint32(0xFFFF0000), F32)
        parts = []
        for j in range(D // col_tile):
            parts += [lo[:, j * hw:(j + 1) * hw], hi[:, j * hw:(j + 1) * hw]]
        return jnp.concatenate(parts, axis=1)

    rows_per_part = tm // COMBINE_PARTS

    def issue(r, part):
        for k in range(TOP_K):
            pltpu.make_async_copy(out_ref.at[pl.ds(dest_ref[r * TOP_K + k], 1), :], buf.at[k, pl.ds(r, 1), :],
                                  sems.at[part]).start()
        return part

    for part in range(COMBINE_PARTS):
        lax.fori_loop(part * rows_per_part, (part + 1) * rows_per_part, issue, part)
    for part in range(COMBINE_PARTS):
        rows = pl.ds(part * rows_per_part, rows_per_part)
        for k in range(TOP_K):
            pltpu.make_async_copy(out_ref.at[pl.ds(0, rows_per_part), :], buf.at[k, rows, :], sems.at[part]).wait()
        gate = gate_ref[rows, :]
        acc = x2_ref[rows, :]
        for k in range(TOP_K):
            acc = acc + gate[:, k:k + 1] * unpack(buf[k, rows, :])
        y_ref[rows, :] = _rms(acc, g_ref[...])


def _combine(x2, gate, dest, out, final_g, tm, col_tile):
    T, D = x2.shape
    row = lambda w: pl.BlockSpec((tm, w), lambda i: (i, 0))
    return pl.pallas_call(
        functools.partial(_combine_kernel, col_tile=col_tile),
        grid=(T // tm,),
        in_specs=[pl.BlockSpec((tm * TOP_K,), lambda i: (i,), memory_space=pltpu.SMEM),
                  row(D), row(LANES), _resident((1, D)), pl.BlockSpec(memory_space=pl.ANY)],
        out_specs=row(D),
        out_shape=jax.ShapeDtypeStruct((T, D), F32),
        scratch_shapes=[pltpu.VMEM((TOP_K, tm, D // 2), jnp.uint32), pltpu.SemaphoreType.DMA((COMBINE_PARTS,))],
        compiler_params=_cparams("arbitrary"),
        name="combine_norm",
    )(dest.reshape(T * TOP_K), x2, gate, final_g.reshape(1, D), out)


def _tile(n, pref):
    return pref if n % pref == 0 else n


def _mixers(x, past, lw):
    (norm_mix_g, w_in, b_igate, b_fgate, g_sb_out, g_ml_out) = lw
    B, S, D = x.shape
    T = B * S
    x2d = x.reshape(T, D)
    q, kf, kb, vf, vb, qm, km, vm, om, gt = _in_projection(x2d, norm_mix_g, w_in, b_igate, b_fgate, _tile(T, 256))
    if past is None:
        sb_o = _sb_prompt(q, kb, vb, g_sb_out, B, S, R=min(8, S // KEY_BLOCK))
        C0 = jnp.zeros((B, ML_HEADS, HEAD_DIM, ML_V_DIM), F32)
        n0 = jnp.zeros((B, ML_HEADS, HEAD_DIM), F32)
        m0 = jnp.zeros((B, ML_HEADS), F32)
        L = _tile(S, 256)
    else:
        cache_k, cache_v, C0, n0, m0 = past
        sb_o = _sb_sample(q, kb, vb, cache_k, cache_v, g_sb_out, B, S, cache_k.shape[1])
        L = S
    ml_o, C, n, m = _mlstm(qm, km, vm, om, gt, C0, n0, m0, g_ml_out, B, S, L)
    state = (kf.reshape(B, S, SB_HEADS, HEAD_DIM), vf.reshape(B, S, SB_HEADS, HEAD_DIM),
             C, n.reshape(B, ML_HEADS, HEAD_DIM), m.reshape(B, ML_HEADS))
    return (x2d, sb_o, ml_o), state


MOE_ROW_TILE = 512
MOE_COL_TILE = 1024
COMBINE_TILE = 256


def kernel(x_prompt, x_sample, cache_k, cache_v, state_C, state_n, state_m, norm_mix_g, w_in, b_igate, b_fgate,
           g_sb_out, g_ml_out, w_out, norm_ffn_g, w_router, b_router, w_gate_up, b_gate_up, w_down, b_down,
           final_norm_g):
    assert w_in.shape[0] == 1, "single-layer trunk"
    lw = (norm_mix_g[0], w_in[0], b_igate[0], b_fgate[0], g_sb_out[0], g_ml_out[0])
    E = w_gate_up.shape[1]
    D = x_prompt.shape[-1]
    groups = [_mixers(x_prompt, None, lw),
              _mixers(x_sample, (cache_k[0], cache_v[0], state_C[0], state_n[0], state_m[0]), lw)]

    counts = jnp.zeros((1, LANES), F32)
    routed = []
    for (x2d, sb_o, ml_o), _ in groups:
        x2, xn2, eid, gate, rank, counts = _outproj_router(x2d, sb_o, ml_o, w_out[0], norm_ffn_g[0], w_router[0],
                                                           b_router[0], counts, _tile(x2d.shape[0], 512))
        routed.append((x2, xn2, eid, gate, rank))
    tm = MOE_ROW_TILE
    cnt = counts[0, :N_EXPERTS].astype(I32)
    padded = (cnt + tm - 1) // tm * tm
    pstart = jnp.cumsum(padded) - padded
    n_assign = sum(r[0].shape[0] for r in routed) * TOP_K
    n_rows = (-(-n_assign // tm) + N_EXPERTS) * tm
    dests = [pstart[eid[:, :TOP_K]] + rank[:, :TOP_K] for (_, _, eid, _, rank) in routed]

    tpe = padded // tm
    n_tiles = n_rows // tm
    tile_ids = jnp.arange(n_tiles, dtype=I32)
    last_of_expert = jnp.any((tile_ids[:, None] == (jnp.cumsum(tpe) - 1)[None, :]) & (tpe[None, :] > 0), axis=1)
    needs_zero = last_of_expert | (tile_ids >= jnp.sum(tpe))
    zero_list = jnp.concatenate([jnp.sum(needs_zero.astype(I32))[None],
                                 jnp.nonzero(needs_zero, size=n_tiles, fill_value=0)[0].astype(I32)])
    xs = _scatter_rows(routed[0][1], routed[1][1], jnp.concatenate(dests, axis=0), zero_list, n_rows,
                       _tile(routed[1][1].shape[0], 512), tm)
    out = _expert_ffn(xs, tpe, w_gate_up[0], b_gate_up[0].reshape(E, 1, -1), w_down[0],
                      b_down[0].reshape(E, 1, -1), tm, MOE_COL_TILE)
    ys = [_combine(x2, gate, dest, out, final_norm_g, _tile(x2.shape[0], COMBINE_TILE), MOE_COL_TILE)
          for (x2, _, _, gate, _), dest in zip(routed, dests)]

    (kp, vp, Cp, np_, mp), (ks, vs, Cs, ns, ms) = groups[0][1], groups[1][1]
    return (ys[0].reshape(x_prompt.shape), ys[1].reshape(x_sample.shape),
            kp[None], vp[None], Cp[None], np_[None], mp[None], ks[None], vs[None], Cs[None], ns[None], ms[None])
```

```python
import functools

import jax
import jax.numpy as jnp
from jax import lax
from jax.experimental import pallas as pl
from jax.experimental.pallas import tpu as pltpu

F32 = jnp.float32
BF16 = jnp.bfloat16
I32 = jnp.int32

EPS = 1e-6
SB_HEADS = 8
HEAD_DIM = 128
ML_HEADS = 4
ML_V_DIM = 256
N_EXPERTS = 32
TOP_K = 4
SWIGLU_ALPHA = 1.702
SWIGLU_LIMIT = 7.0
LANES = 128
KEY_BLOCK = 128
NEG_BIG = -1e30
EXP_ZERO_BELOW = -105.0
VMEM_LIMIT = 56 * 1024 * 1024


def _cparams(*sem, vmem_limit=VMEM_LIMIT):
    return pltpu.CompilerParams(dimension_semantics=sem, vmem_limit_bytes=vmem_limit)


def _resident(shape):
    nd = len(shape)
    return pl.BlockSpec(shape, lambda *_: (0,) * nd, pipeline_mode=pl.Buffered(1))


def _rms(x, g):
    return x * lax.rsqrt(jnp.mean(x * x, axis=-1, keepdims=True) + EPS) * g


def _log_sigmoid(z):
    return jnp.minimum(z, 0.0) - jnp.log(1.0 + jnp.exp(-jnp.abs(z)))


def _split2(x):
    hi = x.astype(BF16)
    lo = (x - hi.astype(F32)).astype(BF16)
    return hi, lo


def _split3(x):
    h1 = x.astype(BF16)
    r = x - h1.astype(F32)
    h2 = r.astype(BF16)
    h3 = (r - h2.astype(F32)).astype(BF16)
    return h1, h2, h3


def _mm(a, b):
    return jnp.dot(a, b, preferred_element_type=F32)


def _inproj_kernel(x_ref, g_ref, wq_ref, wk_ref, wv_ref, wqm_ref, wkm_ref, wvm_ref, wom_ref,
                   wg_ref, bg_ref,
                   q_ref, kf_ref, kb_ref, vf_ref, vb_ref, qm_ref, km_ref, vm_ref, om_ref, gt_ref):
    xn = _rms(x_ref[...], g_ref[...]).astype(BF16)
    q_ref[...] = (_mm(xn, wq_ref[...]) * (HEAD_DIM ** -0.5)).astype(BF16)
    k = _mm(xn, wk_ref[...])
    kf_ref[...] = k
    kb_ref[...] = k.astype(BF16)
    v = _mm(xn, wv_ref[...])
    vf_ref[...] = v
    vb_ref[...] = v.astype(BF16)
    qm_ref[...] = _mm(xn, wqm_ref[...]).astype(BF16)
    km_ref[...] = (_mm(xn, wkm_ref[...]) * (HEAD_DIM ** -0.5)).astype(BF16)
    vm_ref[...] = _mm(xn, wvm_ref[...]).astype(BF16)
    om_ref[...] = _mm(xn, wom_ref[...])
    gpre = _mm(xn, wg_ref[...]) + bg_ref[...]
    lane = lax.broadcasted_iota(I32, gpre.shape, 1)
    is_f = (lane >= ML_HEADS) & (lane < 2 * ML_HEADS)
    gt_ref[...] = jnp.where(is_f, _log_sigmoid(gpre), gpre)


def _in_projection(x2d, norm_g, w_in, b_igate, b_fgate, tm):
    T, D = x2d.shape
    sbw = SB_HEADS * HEAD_DIM
    mqk = ML_HEADS * HEAD_DIM
    mlw = ML_HEADS * ML_V_DIM
    w_bf = w_in.astype(BF16)
    o = 0
    w_specs = []
    for width in (sbw, sbw, sbw, mqk, mqk, mlw, mlw):
        assert o % width == 0
        w_specs.append(pl.BlockSpec((D, width), functools.partial(lambda blk, i: (0, blk), o // width),
                                    pipeline_mode=pl.Buffered(1)))
        o += width
    wg = jnp.zeros((D, LANES), F32).at[:, :2 * ML_HEADS].set(w_in[:, o:o + 2 * ML_HEADS]).astype(BF16)
    bg = jnp.zeros((1, LANES), F32).at[0, :ML_HEADS].set(b_igate).at[0, ML_HEADS:2 * ML_HEADS].set(b_fgate)
    row = lambda w: pl.BlockSpec((tm, w), lambda i: (i, 0))
    out_widths = (sbw, sbw, sbw, sbw, sbw, mqk, mqk, mlw, mlw, LANES)
    out_dtypes = (BF16, F32, BF16, F32, BF16, BF16, BF16, BF16, F32, F32)
    return pl.pallas_call(
        _inproj_kernel,
        grid=(T // tm,),
        in_specs=[row(D), _resident((1, D))] + w_specs + [_resident(wg.shape), _resident(bg.shape)],
        out_specs=[row(w) for w in out_widths],
        out_shape=[jax.ShapeDtypeStruct((T, w), dt) for w, dt in zip(out_widths, out_dtypes)],
        compiler_params=_cparams("parallel"),
        name="in_projection",
    )(x2d, norm_g.reshape(1, D), *([w_bf] * len(w_specs)), wg, bg)


def _suffix_matrix():
    j = lax.broadcasted_iota(I32, (KEY_BLOCK, 2 * KEY_BLOCK), 0)
    c = lax.broadcasted_iota(I32, (KEY_BLOCK, 2 * KEY_BLOCK), 1)
    return jnp.where((c >= KEY_BLOCK) | (j > c), 1.0, 0.0).astype(BF16)


def _sb_step(q, k, v, carry, umat, mask):
    R, bq, _ = q.shape
    z = jnp.einsum("rqd,rkd->rqk", q, k, preferred_element_type=F32)
    lp = jnp.log(1.0 + jnp.exp(-jnp.abs(z)))
    log_beta = jnp.minimum(z, 0.0) - lp
    log_stay = log_beta - z
    if mask is not None:
        log_stay = jnp.where(mask, log_stay, 0.0)
    hi, lo = _split2(log_stay)
    st = _mm(hi.reshape(R * bq, KEY_BLOCK), umat) + _mm(lo.reshape(R * bq, KEY_BLOCK), umat)
    st = st.reshape(R, bq, 2 * KEY_BLOCK)
    w = jnp.exp(log_beta + st[:, :, :KEY_BLOCK] + carry)
    if mask is not None:
        w = jnp.where(mask, w, 0.0)
    pv = jnp.einsum("rqk,rkd->rqd", w.astype(BF16), v, preferred_element_type=F32)
    return pv, carry + st[:, :, KEY_BLOCK:]


def _sb_prompt_kernel(q_ref, k_ref, v_ref, g_ref, o_ref, acc_ref, carry_ref, *, R):
    qi = pl.program_id(2)
    blk0 = qi * R
    q = q_ref[...].reshape(R, KEY_BLOCK, HEAD_DIM)
    umat = _suffix_matrix()
    shape3 = (R, KEY_BLOCK, KEY_BLOCK)
    t_io = lax.broadcasted_iota(I32, shape3, 1)
    s_io = lax.broadcasted_iota(I32, shape3, 2)
    r_io = lax.broadcasted_iota(I32, shape3, 0)

    def load(ref, d):
        return jnp.stack([ref[pl.ds(pl.multiple_of(jnp.maximum(blk0 + r - d, 0) * KEY_BLOCK, KEY_BLOCK),
                                    KEY_BLOCK), :] for r in range(R)])

    def penalty(d_next):
        return jnp.where(r_io < d_next - blk0, NEG_BIG, 0.0)

    pv, carry = _sb_step(q, load(k_ref, 0), load(v_ref, 0), jnp.zeros(shape3, F32), umat, s_io < t_io)
    acc_ref[...] = pv
    carry = carry + penalty(1)
    carry_ref[...] = carry

    def cond(state):
        d, mx = state
        return (d < blk0 + R) & (mx > EXP_ZERO_BELOW)

    def body(state):
        d, _ = state
        pv, carry = _sb_step(q, load(k_ref, d), load(v_ref, d), carry_ref[...], umat, None)
        acc_ref[...] += pv
        carry = carry + penalty(d + 1)
        carry_ref[...] = carry
        return d + 1, jnp.max(carry)

    lax.while_loop(cond, body, (jnp.int32(1), jnp.max(carry)))
    a = acc_ref[...]
    out = a * lax.rsqrt(jnp.mean(a * a, axis=-1, keepdims=True) + EPS) * g_ref[...]
    o_ref[...] = out.reshape(R * KEY_BLOCK, HEAD_DIM).astype(BF16)


def _sb_prompt(q, k, v, g_sb, B, S, R):
    tq = R * KEY_BLOCK
    nq = S // tq
    return pl.pallas_call(
        functools.partial(_sb_prompt_kernel, R=R),
        grid=(B, SB_HEADS, nq),
        in_specs=[pl.BlockSpec((tq, HEAD_DIM), lambda b, h, i: (b * nq + i, h)),
                  pl.BlockSpec((S, HEAD_DIM), lambda b, h, i: (b, h)),
                  pl.BlockSpec((S, HEAD_DIM), lambda b, h, i: (b, h)),
                  pl.BlockSpec((1, HEAD_DIM), lambda b, h, i: (0, h))],
        out_specs=pl.BlockSpec((tq, HEAD_DIM), lambda b, h, i: (b * nq + i, h)),
        out_shape=jax.ShapeDtypeStruct((B * S, SB_HEADS * HEAD_DIM), BF16),
        scratch_shapes=[pltpu.VMEM((R, KEY_BLOCK, HEAD_DIM), F32),
                        pltpu.VMEM((R, KEY_BLOCK, KEY_BLOCK), F32)],
        compiler_params=_cparams("parallel", "parallel", "parallel"),
        name="sb_prompt",
    )(q, k, v, g_sb.reshape(1, -1))


def _sb_sample_kernel(q_ref, kn_ref, vn_ref, ck_ref, cv_ref, g_ref, o_ref, acc_ref, carry_ref, *, S, P):
    H = SB_HEADS
    hs = lambda h: slice(h * HEAD_DIM, (h + 1) * HEAD_DIM)
    umat = _suffix_matrix()
    q = jnp.stack([q_ref[:, hs(h)] for h in range(H)])
    pad = jnp.zeros((KEY_BLOCK - S, HEAD_DIM), BF16)

    def new_keys(ref):
        return jnp.stack([jnp.concatenate([ref[:, hs(h)], pad], axis=0) for h in range(H)])

    def past_keys(ref, j):
        start = pl.multiple_of(j * KEY_BLOCK, KEY_BLOCK)
        return jnp.stack([ref[pl.ds(start, KEY_BLOCK), hs(h)].astype(BF16) for h in range(H)])

    shape3 = (H, S, KEY_BLOCK)
    t_io = lax.broadcasted_iota(I32, shape3, 1)
    s_io = lax.broadcasted_iota(I32, shape3, 2)
    pv, carry = _sb_step(q, new_keys(kn_ref), new_keys(vn_ref), jnp.zeros(shape3, F32), umat, s_io < t_io)
    acc_ref[...] = pv
    carry_ref[...] = carry

    def cond(state):
        j, mx = state
        return (j >= 0) & (mx > EXP_ZERO_BELOW)

    def body(state):
        j, _ = state
        pv, carry = _sb_step(q, past_keys(ck_ref, j), past_keys(cv_ref, j), carry_ref[...], umat, None)
        acc_ref[...] += pv
        carry_ref[...] = carry
        return j - 1, jnp.max(carry)

    lax.while_loop(cond, body, (jnp.int32(P // KEY_BLOCK - 1), jnp.max(carry)))
    a = acc_ref[...]
    a = a * lax.rsqrt(jnp.mean(a * a, axis=-1, keepdims=True) + EPS)
    for h in range(H):
        o_ref[:, hs(h)] = (a[h] * g_ref[:, hs(h)]).astype(BF16)


def _sb_sample(q, kn, vn, cache_k, cache_v, g_sb, B, S, P):
    W = SB_HEADS * HEAD_DIM
    row = pl.BlockSpec((S, W), lambda b: (b, 0))
    past = pl.BlockSpec((P, W), lambda b: (b, 0))
    return pl.pallas_call(
        functools.partial(_sb_sample_kernel, S=S, P=P),
        grid=(B,),
        in_specs=[row, row, row, past, past, pl.BlockSpec((1, W), lambda b: (0, 0))],
        out_specs=row,
        out_shape=jax.ShapeDtypeStruct((B * S, W), BF16),
        scratch_shapes=[pltpu.VMEM((SB_HEADS, S, HEAD_DIM), F32),
                        pltpu.VMEM((SB_HEADS, S, KEY_BLOCK), F32)],
        compiler_params=_cparams("parallel"),
        name="sb_sample",
    )(q, kn, vn, cache_k.reshape(B * P, W), cache_v.reshape(B * P, W), g_sb.reshape(1, W))


def _mlstm_kernel(q_ref, k_ref, v_ref, o_ref, gt_ref, c0_ref, n0_ref, m0_ref, g_ref,
                  out_ref, c_out_ref, n_out_ref, m_out_ref, cext_ref, m_ref, *, L, Lp):
    c = pl.program_id(1)
    H = ML_HEADS
    VW = ML_V_DIM + LANES
    lane_row = lax.broadcasted_iota(I32, (1, LANES), 1)
    onehot0 = jnp.where(lane_row == 0, 1.0, 0.0)

    @pl.when(c == 0)
    def _():
        for h in range(H):
            cext_ref[h] = jnp.concatenate([c0_ref[0, h], n0_ref[0, h] * onehot0], axis=1)
            m_ref[h] = jnp.broadcast_to(m0_ref[0, h], (8, LANES))

    def pad_rows(a, fill=0.0):
        if Lp == L:
            return a
        return jnp.concatenate([a, jnp.full((Lp - L, a.shape[1]), fill, a.dtype)], axis=0)

    gt = gt_ref[...]
    lane = lax.broadcasted_iota(I32, (Lp, LANES), 1)
    if Lp != L:
        gt = jnp.concatenate([gt, jnp.broadcast_to(jnp.where(lane_row < H, NEG_BIG, 0.0), (Lp - L, LANES))], axis=0)
    lf = jnp.where((lane >= H) & (lane < 2 * H), gt, 0.0)
    ti = lax.broadcasted_iota(I32, (Lp, Lp), 0)
    si = lax.broadcasted_iota(I32, (Lp, Lp), 1)
    causal = si <= ti
    tri = jnp.where(causal, 1.0, 0.0).astype(BF16)
    bc = sum(_mm(tri, p) for p in _split3(lf))
    bc_t = bc.T
    gt_t = gt.T
    ones_blk = jnp.broadcast_to(onehot0, (Lp, LANES)).astype(BF16)

    for h in range(H):
        b_col = bc[:, H + h:H + h + 1]
        b_row = bc_t[H + h:H + h + 1, :]
        i_col = gt[:, h:h + 1]
        i_row = gt_t[h:h + 1, :]
        m_prev = m_ref[h][0:1, 0:1]
        qh = pad_rows(q_ref[:, h * HEAD_DIM:(h + 1) * HEAD_DIM])
        kh = pad_rows(k_ref[:, h * HEAD_DIM:(h + 1) * HEAD_DIM])
        vh = pad_rows(v_ref[:, h * ML_V_DIM:(h + 1) * ML_V_DIM])
        vext = jnp.concatenate([vh, ones_blk], axis=1)
        cext = cext_ref[h]

        log_d = jnp.where(causal, b_col - b_row + i_row, NEG_BIG)
        log_inter = b_col + m_prev
        m_row = jnp.maximum(log_inter, jnp.max(log_d, axis=1, keepdims=True))
        dmat = jnp.exp(log_d - m_row)
        s = lax.dot_general(qh, kh, (((1,), (1,)), ((), ())), preferred_element_type=F32) * dmat
        inter = jnp.exp(log_inter - m_row)
        num = _mm(s.astype(BF16), vext) + inter * _mm(qh, cext.astype(BF16))
        den = num[:, ML_V_DIM:ML_V_DIM + 1]
        hh = num[:L, :ML_V_DIM] / jnp.maximum(jnp.abs(den), jnp.exp(-m_row))[:L]
        hn = hh * lax.rsqrt(jnp.mean(hh * hh, axis=-1, keepdims=True) + EPS)
        cols = slice(h * ML_V_DIM, (h + 1) * ML_V_DIM)
        ogate = 1.0 / (1.0 + jnp.exp(-o_ref[:, cols]))
        out_ref[:, cols] = (ogate * (hn * g_ref[:, cols])).astype(BF16)

        b_last = b_col[Lp - 1:Lp, :]
        log_w = b_last - b_col + i_col
        m_new = jnp.maximum(b_last + m_prev, jnp.max(log_w, axis=0, keepdims=True))
        wk = jnp.exp(log_w - m_new)
        decay = jnp.exp(b_last + m_prev - m_new)
        upd = lax.dot_general(kh, (wk * vext.astype(F32)).astype(BF16), (((0,), (0,)), ((), ())),
                              preferred_element_type=F32)
        cnew = decay * cext + upd
        cext_ref[h] = cnew
        m_ref[h] = jnp.broadcast_to(m_new, (8, LANES))

    @pl.when(c == pl.num_programs(1) - 1)
    def _():
        for h in range(H):
            cf = cext_ref[h]
            c_out_ref[0, h] = cf[:, :ML_V_DIM]
            n_out_ref[0, h] = cf[:, ML_V_DIM:ML_V_DIM + 1]
            m_out_ref[0, h] = m_ref[h][0:1, 0:1]


def _mlstm(qm, km, vm, om, gt, C0, n0, m0, g_ml, B, S, L):
    nc = S // L
    Lp = max(L, LANES)
    H = ML_HEADS
    VW = ML_V_DIM + LANES
    row = lambda w: pl.BlockSpec((L, w), lambda b, c: (b * nc + c, 0))
    st = lambda *tail: pl.BlockSpec((1, H) + tail, lambda b, c: (b, 0, 0, 0))
    return pl.pallas_call(
        functools.partial(_mlstm_kernel, L=L, Lp=Lp),
        grid=(B, nc),
        in_specs=[row(H * HEAD_DIM), row(H * HEAD_DIM), row(H * ML_V_DIM), row(H * ML_V_DIM), row(LANES),
                  st(HEAD_DIM, ML_V_DIM), st(HEAD_DIM, 1), st(1, 1),
                  pl.BlockSpec((1, H * ML_V_DIM), lambda b, c: (0, 0))],
        out_specs=[row(H * ML_V_DIM), st(HEAD_DIM, ML_V_DIM), st(HEAD_DIM, 1), st(1, 1)],
        out_shape=[jax.ShapeDtypeStruct((B * S, H * ML_V_DIM), BF16),
                   jax.ShapeDtypeStruct((B, H, HEAD_DIM, ML_V_DIM), F32),
                   jax.ShapeDtypeStruct((B, H, HEAD_DIM, 1), F32),
                   jax.ShapeDtypeStruct((B, H, 1, 1), F32)],
        scratch_shapes=[pltpu.VMEM((H, HEAD_DIM, VW), F32), pltpu.VMEM((H, 8, LANES), F32)],
        compiler_params=_cparams("parallel", "arbitrary"),
        name="mlstm",
    )(qm, km, vm, om, gt, C0, n0.reshape(B, H, HEAD_DIM, 1), m0.reshape(B, H, 1, 1), g_ml.reshape(1, -1))


def _pack_halves(lo, hi):
    lo_bits = pltpu.bitcast(lo.astype(BF16).astype(F32), jnp.uint32) >> 16
    hi_bits = pltpu.bitcast(hi.astype(BF16).astype(F32), jnp.uint32) & jnp.uint32(0xFFFF0000)
    return lo_bits | hi_bits


def _unpack_halves(w):
    lo = pltpu.bitcast(w << 16, F32).astype(BF16)
    hi = pltpu.bitcast(w & jnp.uint32(0xFFFF0000), F32).astype(BF16)
    return lo, hi


def _outproj_router_kernel(x_ref, sb_ref, ml_ref, wos_ref, wom_ref, g_ref, wr_hi_ref, wr_lo_ref, br_ref, c0_ref,
                           x2_ref, xn_ref, eid_ref, gate_ref, rank_ref, cnt_ref, carry_ref):
    i = pl.program_id(0)

    @pl.when(i == 0)
    def _():
        carry_ref[...] = c0_ref[...]

    x2 = x_ref[...] + _mm(sb_ref[...], wos_ref[...]) + _mm(ml_ref[...], wom_ref[...])
    x2_ref[...] = x2
    xn = _rms(x2, g_ref[...])
    half = xn.shape[1] // 2
    xn_ref[...] = _pack_halves(xn[:, :half], xn[:, half:])
    hi, lo = _split2(xn)
    logits = _mm(hi, wr_hi_ref[...]) + _mm(lo, wr_hi_ref[...]) + _mm(hi, wr_lo_ref[...]) + br_ref[...]
    tm = logits.shape[0]
    lane = lax.broadcasted_iota(I32, (tm, LANES), 1)
    vals, ids = [], []
    cnt = jnp.zeros((tm, LANES), F32)
    for _ in range(TOP_K):
        mx = jnp.max(logits, axis=-1, keepdims=True)
        idx = jnp.min(jnp.where(logits == mx, lane, LANES), axis=-1, keepdims=True)
        sel = lane == idx
        vals.append(mx)
        ids.append(idx)
        logits = jnp.where(sel, -jnp.inf, logits)
        cnt = cnt + jnp.where(sel, 1.0, 0.0)
    es = [jnp.exp(v - vals[0]) for v in vals]
    inv = 1.0 / sum(es)
    ti = lax.broadcasted_iota(I32, (tm, tm), 0)
    si = lax.broadcasted_iota(I32, (tm, tm), 1)
    before = _mm(jnp.where(si < ti, 1.0, 0.0).astype(BF16), cnt.astype(BF16)) + carry_ref[...]
    eid_o = jnp.zeros((tm, LANES), I32)
    gate_o = jnp.zeros((tm, LANES), F32)
    rank_o = jnp.zeros((tm, LANES), I32)
    for k in range(TOP_K):
        rk = jnp.sum(jnp.where(lane == ids[k], before, 0.0), axis=-1, keepdims=True)
        eid_o = jnp.where(lane == k, ids[k], eid_o)
        gate_o = jnp.where(lane == k, es[k] * inv, gate_o)
        rank_o = jnp.where(lane == k, rk.astype(I32), rank_o)
    eid_ref[...] = eid_o
    gate_ref[...] = gate_o
    rank_ref[...] = rank_o
    carry_ref[...] += jnp.sum(cnt, axis=0, keepdims=True)
    cnt_ref[...] = carry_ref[...]


def _outproj_router(x2d, sb_o, ml_o, w_out, norm_g, w_router, b_router, counts_in, tm):
    T, D = x2d.shape
    W = sb_o.shape[1]
    assert w_out.shape[0] == 2 * W
    w_bf = w_out.astype(BF16)
    half_w = lambda blk: pl.BlockSpec((W, D), lambda i: (blk, 0), pipeline_mode=pl.Buffered(1))
    wr = jnp.zeros((D, LANES), F32).at[:, :N_EXPERTS].set(w_router)
    wr_hi = wr.astype(BF16)
    wr_lo = (wr - wr_hi.astype(F32)).astype(BF16)
    br = jnp.full((1, LANES), NEG_BIG, F32).at[0, :N_EXPERTS].set(b_router)
    row = lambda w: pl.BlockSpec((tm, w), lambda i: (i, 0))
    return pl.pallas_call(
        _outproj_router_kernel,
        grid=(T // tm,),
        in_specs=[row(D), row(W), row(W), half_w(0), half_w(1),
                  _resident((1, D)), _resident(wr_hi.shape), _resident(wr_lo.shape), _resident(br.shape),
                  _resident((1, LANES))],
        out_specs=[row(D), row(D // 2), row(LANES), row(LANES), row(LANES),
                   pl.BlockSpec((1, LANES), lambda i: (0, 0))],
        out_shape=[jax.ShapeDtypeStruct((T, D), F32), jax.ShapeDtypeStruct((T, D // 2), jnp.uint32),
                   jax.ShapeDtypeStruct((T, LANES), I32), jax.ShapeDtypeStruct((T, LANES), F32),
                   jax.ShapeDtypeStruct((T, LANES), I32), jax.ShapeDtypeStruct((1, LANES), F32)],
        scratch_shapes=[pltpu.VMEM((1, LANES), F32)],
        compiler_params=_cparams("arbitrary"),
        name="outproj_router",
    )(x2d, sb_o, ml_o, w_bf, w_bf, norm_g.reshape(1, D), wr_hi, wr_lo, br, counts_in)


ROW_BLOCK = 128
IT_TILE, IT_COL, IT_EXPERT, IT_FIRST, IT_BLOCKS, IT_NEXT_EXPERT, IT_NEXT_COL = range(7)


def _segment_weights(it_ref, w, copies, cast):
    @pl.when(it_ref[IT_FIRST, w] == 1)
    def _():
        @pl.when(w == 0)
        def _():
            for c in copies(it_ref[IT_EXPERT, w], it_ref[IT_COL, w]):
                c.start()

        for c in copies(it_ref[IT_EXPERT, w], it_ref[IT_COL, w]):
            c.wait()
        cast()

        @pl.when(it_ref[IT_NEXT_EXPERT, w] >= 0)
        def _():
            for c in copies(it_ref[IT_NEXT_EXPERT, w], it_ref[IT_NEXT_COL, w]):
                c.start()


def _for_filled_rows(blocks, tm, compute):
    for lvl in range(1, tm // ROW_BLOCK + 1):
        @pl.when(blocks == lvl)
        def _():
            compute(lvl * ROW_BLOCK)


def _gate_up_kernel(it_ref, n_ref, x_ref, w_hbm, bg_ref, bl_ref, act_ref, stage, wg_s, wl_s, x_s, sems, *, nj):
    w = pl.program_id(0)
    tm, half = x_ref.shape
    tn = act_ref.shape[1]

    def copies(e, j):
        return [pltpu.make_async_copy(w_hbm.at[e, :, pl.ds(pl.multiple_of((h * nj + j) * tn, tn), tn)],
                                      stage.at[h], sems.at[h]) for h in range(2)]

    def cast():
        wg_s[...] = stage[0].astype(BF16)
        wl_s[...] = stage[1].astype(BF16)

    @pl.when(w < n_ref[0])
    def _():
        _segment_weights(it_ref, w, copies, cast)

        def compute(rows):
            lo, hi = _unpack_halves(x_ref[:rows, :])
            x_s[:rows, :half] = lo
            x_s[:rows, half:] = hi
            x = x_s[:rows, :]
            glu = jnp.minimum(_mm(x, wg_s[...]) + bg_ref[0], SWIGLU_LIMIT)
            lin = jnp.clip(_mm(x, wl_s[...]) + bl_ref[0], -SWIGLU_LIMIT, SWIGLU_LIMIT)
            act = glu * (1.0 / (1.0 + jnp.exp(-SWIGLU_ALPHA * glu))) * (lin + 1.0)
            act_ref[:rows, :] = act.astype(BF16)
            if rows < tm:
                act_ref[rows:, :] = jnp.zeros((tm - rows, tn), BF16)

        _for_filled_rows(it_ref[IT_BLOCKS, w], tm, compute)

    @pl.when(w >= n_ref[0])
    def _():
        act_ref[...] = jnp.zeros_like(act_ref)


def _down_kernel(it_ref, n_ref, a_ref, w_hbm, b_ref, o_ref, stage, w_s, sem):
    w = pl.program_id(0)
    tm = a_ref.shape[0]
    tn = w_s.shape[1]

    def copies(e, j):
        return [pltpu.make_async_copy(w_hbm.at[e, :, pl.ds(pl.multiple_of(j * tn, tn), tn)], stage, sem)]

    def cast():
        w_s[...] = stage[...].astype(BF16)

    @pl.when(w < n_ref[0])
    def _():
        _segment_weights(it_ref, w, copies, cast)

        def compute(rows):
            out = _mm(a_ref[:rows, :], w_s[...]) + b_ref[0]
            o_ref[:rows, :] = _pack_halves(out[:, :tn // 2], out[:, tn // 2:])
            if rows < tm:
                o_ref[rows:, :] = jnp.zeros((tm - rows, tn // 2), jnp.uint32)

        _for_filled_rows(it_ref[IT_BLOCKS, w], tm, compute)

    @pl.when(w >= n_ref[0])
    def _():
        o_ref[...] = jnp.zeros_like(o_ref)


def _work_list(counts, tm, nj, n_tiles_max):
    tpe = (counts + tm - 1) // tm
    tile_start = jnp.cumsum(tpe) - tpe
    item_end = jnp.cumsum(tpe * nj)
    item_start = item_end - tpe * nj
    n_items = item_end[-1:]
    w = jnp.arange(n_tiles_max * nj, dtype=I32)
    wc = jnp.minimum(w, n_items[0] - 1)
    e = jnp.minimum(jnp.sum((wc[:, None] >= item_end[None, :]).astype(I32), axis=1), N_EXPERTS - 1)
    local = wc - item_start[e]
    t = jnp.maximum(tpe[e], 1)
    j = local // t
    il = local - j * t
    real = w < n_items[0]
    first = ((il == 0) & real).astype(I32)
    tail = jnp.maximum(w - n_items[0], 0)
    tile = jnp.where(real, tile_start[e] + il, jnp.sum(tpe) + tail // nj)
    j = jnp.where(real, j, tail % nj)
    blocks = (jnp.clip(counts[e] - il * tm, 1, tm) + ROW_BLOCK - 1) // ROW_BLOCK
    ids = jnp.where(tpe > 0, jnp.arange(N_EXPERTS, dtype=I32), N_EXPERTS)
    later = jnp.concatenate([lax.cummin(ids[::-1])[::-1][1:], jnp.full((1,), N_EXPERTS, I32)])
    next_nonempty = jnp.where(later < N_EXPERTS, later, -1)
    last_col = j + 1 >= nj
    next_e = jnp.where(last_col, next_nonempty[e], e)
    next_j = jnp.where(last_col, 0, j + 1)
    table = jnp.stack([tile, j, e, first, blocks, next_e, next_j]).astype(I32)
    return table, n_items.astype(I32)


def _expert_ffn(xs, counts, w_gu, b_gu, w_dn, b_dn, tm, tn):
    n_rows, half = xs.shape
    D = 2 * half
    F = w_dn.shape[1]
    n_tiles = n_rows // tm
    nj = F // tn
    row_tile = lambda w, it, n: (it[IT_TILE, w], 0)
    out_tile = lambda w, it, n: (it[IT_TILE, w], it[IT_COL, w])
    act = pl.pallas_call(
        functools.partial(_gate_up_kernel, nj=nj),
        grid_spec=pltpu.PrefetchScalarGridSpec(
            num_scalar_prefetch=2,
            grid=(n_tiles * nj,),
            in_specs=[pl.BlockSpec((tm, half), row_tile),
                      pl.BlockSpec(memory_space=pl.ANY),
                      pl.BlockSpec((1, 1, tn), lambda w, it, n: (it[IT_EXPERT, w], 0, it[IT_COL, w])),
                      pl.BlockSpec((1, 1, tn), lambda w, it, n: (it[IT_EXPERT, w], 0, nj + it[IT_COL, w]))],
            out_specs=pl.BlockSpec((tm, tn), out_tile),
            scratch_shapes=[pltpu.VMEM((2, D, tn), F32), pltpu.VMEM((D, tn), BF16), pltpu.VMEM((D, tn), BF16),
                            pltpu.VMEM((tm, D), BF16), pltpu.SemaphoreType.DMA((2,))]),
        out_shape=jax.ShapeDtypeStruct((n_rows, F), BF16),
        compiler_params=_cparams("arbitrary"),
        name="expert_gate_up",
    )(*_work_list(counts, tm, nj, n_tiles), xs, w_gu, b_gu, b_gu)

    njd = D // tn
    return pl.pallas_call(
        _down_kernel,
        grid_spec=pltpu.PrefetchScalarGridSpec(
            num_scalar_prefetch=2,
            grid=(n_tiles * njd,),
            in_specs=[pl.BlockSpec((tm, F), row_tile),
                      pl.BlockSpec(memory_space=pl.ANY),
                      pl.BlockSpec((1, 1, tn), lambda w, it, n: (it[IT_EXPERT, w], 0, it[IT_COL, w]))],
            out_specs=pl.BlockSpec((tm, tn // 2), out_tile),
            scratch_shapes=[pltpu.VMEM((F, tn), F32), pltpu.VMEM((F, tn), BF16), pltpu.SemaphoreType.DMA]),
        out_shape=jax.ShapeDtypeStruct((n_rows, D // 2), jnp.uint32),
        compiler_params=_cparams("arbitrary"),
        name="expert_down",
    )(*_work_list(counts, tm, njd, n_tiles), act, w_dn, b_dn)


def _scatter_kernel(zero_ref, dest_ref, xa_ref, xb_ref, xs_ref, zbuf, sem, zsem, *, tiles_a, slot_tile):
    i = pl.program_id(0)
    tm = xa_ref.shape[0]

    @pl.when(i == 0)
    def _():
        zbuf[...] = jnp.zeros_like(zbuf)

        def tile_copy(z):
            start = pl.multiple_of(zero_ref[1 + z] * slot_tile, slot_tile)
            return pltpu.make_async_copy(zbuf, xs_ref.at[pl.ds(start, slot_tile), :], zsem)

        def start(z, carry):
            tile_copy(z).start()
            return carry

        def wait(z, carry):
            tile_copy(z).wait()
            return carry

        lax.fori_loop(0, zero_ref[0], start, 0)
        lax.fori_loop(0, zero_ref[0], wait, 0)

    def scatter_from(x_ref):
        def issue(r, carry):
            for k in range(TOP_K):
                pltpu.make_async_copy(x_ref.at[pl.ds(r, 1), :], xs_ref.at[pl.ds(dest_ref[r * TOP_K + k], 1), :],
                                      sem).start()
            return carry

        lax.fori_loop(0, tm, issue, 0, unroll=8)
        for _ in range(TOP_K):
            pltpu.make_async_copy(x_ref, xs_ref.at[pl.ds(0, tm), :], sem).wait()

    @pl.when(i < tiles_a)
    def _():
        scatter_from(xa_ref)

    @pl.when(i >= tiles_a)
    def _():
        scatter_from(xb_ref)


def _scatter_rows(xa, xb, dest, zero_list, n_rows, tm, slot_tile):
    Ta, half = xa.shape
    Tb = xb.shape[0]
    tiles_a, tiles_b = Ta // tm, Tb // tm
    return pl.pallas_call(
        functools.partial(_scatter_kernel, tiles_a=tiles_a, slot_tile=slot_tile),
        grid_spec=pltpu.PrefetchScalarGridSpec(
            num_scalar_prefetch=1,
            grid=(tiles_a + tiles_b,),
            in_specs=[pl.BlockSpec((tm * TOP_K,), lambda i, z: (i,), memory_space=pltpu.SMEM),
                      pl.BlockSpec((tm, half), lambda i, z: (jnp.minimum(i, tiles_a - 1), 0)),
                      pl.BlockSpec((tm, half), lambda i, z: (jnp.maximum(i - tiles_a, 0), 0))],
            out_specs=pl.BlockSpec(memory_space=pl.ANY),
            scratch_shapes=[pltpu.VMEM((slot_tile, half), jnp.uint32), pltpu.SemaphoreType.DMA,
                            pltpu.SemaphoreType.DMA]),
        out_shape=jax.ShapeDtypeStruct((n_rows, half), jnp.uint32),
        compiler_params=_cparams("arbitrary"),
        name="scatter_rows",
    )(zero_list, dest.reshape((Ta + Tb) * TOP_K), xa, xb)


COMBINE_PARTS = 2


def _combine_kernel(dest_ref, x2_ref, gate_ref, g_ref, out_ref, y_ref, buf, sems, *, col_tile):
    tm, D = x2_ref.shape
    hw = col_tile // 2

    def unpack(w):
        lo = pltpu.bitcast(w << 16, F32)
        hi = pltpu.bitcast(w & jnp.uint32(0xFFFF0000), F32)
        parts = []
        for j in range(D // col_tile):
            parts += [lo[:, j * hw:(j + 1) * hw], hi[:, j * hw:(j + 1) * hw]]
        return jnp.concatenate(parts, axis=1)

    rows_per_part = tm // COMBINE_PARTS

    def issue(r, part):
        for k in range(TOP_K):
            pltpu.make_async_copy(out_ref.at[pl.ds(dest_ref[r * TOP_K + k], 1), :], buf.at[k, pl.ds(r, 1), :],
                                  sems.at[part]).start()
        return part

    for part in range(COMBINE_PARTS):
        lax.fori_loop(part * rows_per_part, (part + 1) * rows_per_part, issue, part, unroll=8)
    for part in range(COMBINE_PARTS):
        rows = pl.ds(part * rows_per_part, rows_per_part)
        for k in range(TOP_K):
            pltpu.make_async_copy(out_ref.at[pl.ds(0, rows_per_part), :], buf.at[k, rows, :], sems.at[part]).wait()
        gate = gate_ref[rows, :]
        acc = x2_ref[rows, :]
        for k in range(TOP_K):
            acc = acc + gate[:, k:k + 1] * unpack(buf[k, rows, :])
        y_ref[rows, :] = _rms(acc, g_ref[...])


def _combine(x2, gate, dest, out, final_g, tm, col_tile):
    T, D = x2.shape
    row = lambda w: pl.BlockSpec((tm, w), lambda i: (i, 0))
    return pl.pallas_call(
        functools.partial(_combine_kernel, col_tile=col_tile),
        grid=(T // tm,),
        in_specs=[pl.BlockSpec((tm * TOP_K,), lambda i: (i,), memory_space=pltpu.SMEM),
                  row(D), row(LANES), _resident((1, D)), pl.BlockSpec(memory_space=pl.ANY)],
        out_specs=row(D),
        out_shape=jax.ShapeDtypeStruct((T, D), F32),
        scratch_shapes=[pltpu.VMEM((TOP_K, tm, D // 2), jnp.uint32), pltpu.SemaphoreType.DMA((COMBINE_PARTS,))],
        compiler_params=_cparams("arbitrary"),
        name="combine_norm",
    )(dest.reshape(T * TOP_K), x2, gate, final_g.reshape(1, D), out)


def _tile(n, pref):
    return pref if n % pref == 0 else n


def _mixers(x, past, lw):
    (norm_mix_g, w_in, b_igate, b_fgate, g_sb_out, g_ml_out) = lw
    B, S, D = x.shape
    T = B * S
    x2d = x.reshape(T, D)
    q, kf, kb, vf, vb, qm, km, vm, om, gt = _in_projection(x2d, norm_mix_g, w_in, b_igate, b_fgate, _tile(T, 256))
    if past is None:
        sb_o = _sb_prompt(q, kb, vb, g_sb_out, B, S, R=min(8, S // KEY_BLOCK))
        C0 = jnp.zeros((B, ML_HEADS, HEAD_DIM, ML_V_DIM), F32)
        n0 = jnp.zeros((B, ML_HEADS, HEAD_DIM), F32)
        m0 = jnp.zeros((B, ML_HEADS), F32)
        L = _tile(S, 256)
    else:
        cache_k, cache_v, C0, n0, m0 = past
        sb_o = _sb_sample(q, kb, vb, cache_k, cache_v, g_sb_out, B, S, cache_k.shape[1])
        L = S
    ml_o, C, n, m = _mlstm(qm, km, vm, om, gt, C0, n0, m0, g_ml_out, B, S, L)
    state = (kf.reshape(B, S, SB_HEADS, HEAD_DIM), vf.reshape(B, S, SB_HEADS, HEAD_DIM),
             C, n.reshape(B, ML_HEADS, HEAD_DIM), m.reshape(B, ML_HEADS))
    return (x2d, sb_o, ml_o), state


MOE_ROW_TILE = 512
MOE_COL_TILE = 1024
COMBINE_TILE = 256


def kernel(x_prompt, x_sample, cache_k, cache_v, state_C, state_n, state_m, norm_mix_g, w_in, b_igate, b_fgate,
           g_sb_out, g_ml_out, w_out, norm_ffn_g, w_router, b_router, w_gate_up, b_gate_up, w_down, b_down,
           final_norm_g):
    assert w_in.shape[0] == 1, "single-layer trunk"
    lw = (norm_mix_g[0], w_in[0], b_igate[0], b_fgate[0], g_sb_out[0], g_ml_out[0])
    E = w_gate_up.shape[1]
    D = x_prompt.shape[-1]
    groups = [_mixers(x_prompt, None, lw),
              _mixers(x_sample, (cache_k[0], cache_v[0], state_C[0], state_n[0], state_m[0]), lw)]

    counts = jnp.zeros((1, LANES), F32)
    routed = []
    for (x2d, sb_o, ml_o), _ in groups:
        x2, xn2, eid, gate, rank, counts = _outproj_router(x2d, sb_o, ml_o, w_out[0], norm_ffn_g[0], w_router[0],
                                                           b_router[0], counts, _tile(x2d.shape[0], 512))
        routed.append((x2, xn2, eid, gate, rank))
    tm = MOE_ROW_TILE
    cnt = counts[0, :N_EXPERTS].astype(I32)
    padded = (cnt + tm - 1) // tm * tm
    pstart = jnp.cumsum(padded) - padded
    n_assign = sum(r[0].shape[0] for r in routed) * TOP_K
    n_rows = (-(-n_assign // tm) + N_EXPERTS) * tm
    dests = [pstart[eid[:, :TOP_K]] + rank[:, :TOP_K] for (_, _, eid, _, rank) in routed]

    tpe = padded // tm
    n_tiles = n_rows // tm
    tile_ids = jnp.arange(n_tiles, dtype=I32)
    last_of_expert = jnp.any((tile_ids[:, None] == (jnp.cumsum(tpe) - 1)[None, :]) & (tpe[None, :] > 0), axis=1)
    needs_zero = last_of_expert | (tile_ids >= jnp.sum(tpe))
    zero_list = jnp.concatenate([jnp.sum(needs_zero.astype(I32))[None],
                                 jnp.nonzero(needs_zero, size=n_tiles, fill_value=0)[0].astype(I32)])
    xs = _scatter_rows(routed[0][1], routed[1][1], jnp.concatenate(dests, axis=0), zero_list, n_rows,
                       _tile(routed[1][1].shape[0], 512), tm)
    out = _expert_ffn(xs, cnt, w_gate_up[0], b_gate_up[0].reshape(E, 1, -1), w_down[0],
                      b_down[0].reshape(E, 1, -1), tm, MOE_COL_TILE)
    ys = [_combine(x2, gate, dest, out, final_norm_g, _tile(x2.shape[0], COMBINE_TILE), MOE_COL_TILE)
          for (x2, _, _, gate, _), dest in zip(routed, dests)]

    (kp, vp, Cp, np_, mp), (ks, vs, Cs, ns, ms) = groups[0][1], groups[1][1]
    return (ys[0].reshape(x_prompt.shape), ys[1].reshape(x_sample.shape),
            kp[None], vp[None], Cp[None], np_[None], mp[None], ks[None], vs[None], Cs[None], ns[None], ms[None])
```

```python
import functools

import jax
import jax.numpy as jnp
from jax import lax
from jax.experimental import pallas as pl
from jax.experimental.pallas import tpu as pltpu

F32 = jnp.float32
BF16 = jnp.bfloat16
I32 = jnp.int32

EPS = 1e-6
SB_HEADS = 8
HEAD_DIM = 128
ML_HEADS = 4
ML_V_DIM = 256
N_EXPERTS = 32
TOP_K = 4
SWIGLU_ALPHA = 1.702
SWIGLU_LIMIT = 7.0
LANES = 128
KEY_BLOCK = 128
NEG_BIG = -1e30
EXP_ZERO_BELOW = -105.0
VMEM_LIMIT = 56 * 1024 * 1024


def _cparams(*sem, vmem_limit=VMEM_LIMIT):
    return pltpu.CompilerParams(dimension_semantics=sem, vmem_limit_bytes=vmem_limit)


def _resident(shape):
    nd = len(shape)
    return pl.BlockSpec(shape, lambda *_: (0,) * nd, pipeline_mode=pl.Buffered(1))


def _rms(x, g):
    return x * lax.rsqrt(jnp.mean(x * x, axis=-1, keepdims=True) + EPS) * g


def _log_sigmoid(z):
    return jnp.minimum(z, 0.0) - jnp.log(1.0 + jnp.exp(-jnp.abs(z)))


def _split2(x):
    hi = x.astype(BF16)
    lo = (x - hi.astype(F32)).astype(BF16)
    return hi, lo


def _split3(x):
    h1 = x.astype(BF16)
    r = x - h1.astype(F32)
    h2 = r.astype(BF16)
    h3 = (r - h2.astype(F32)).astype(BF16)
    return h1, h2, h3


def _mm(a, b):
    return jnp.dot(a, b, preferred_element_type=F32)


def _inproj_kernel(x_ref, g_ref, wq_ref, wk_ref, wv_ref, wqm_ref, wkm_ref, wvm_ref, wom_ref,
                   wg_ref, bg_ref,
                   q_ref, kf_ref, kb_ref, vf_ref, vb_ref, qm_ref, km_ref, vm_ref, om_ref, gt_ref):
    xn = _rms(x_ref[...], g_ref[...]).astype(BF16)
    q_ref[...] = (_mm(xn, wq_ref[...]) * (HEAD_DIM ** -0.5)).astype(BF16)
    k = _mm(xn, wk_ref[...])
    kf_ref[...] = k
    kb_ref[...] = k.astype(BF16)
    v = _mm(xn, wv_ref[...])
    vf_ref[...] = v
    vb_ref[...] = v.astype(BF16)
    qm_ref[...] = _mm(xn, wqm_ref[...]).astype(BF16)
    km_ref[...] = (_mm(xn, wkm_ref[...]) * (HEAD_DIM ** -0.5)).astype(BF16)
    vm_ref[...] = _mm(xn, wvm_ref[...]).astype(BF16)
    om_ref[...] = _mm(xn, wom_ref[...])
    gpre = _mm(xn, wg_ref[...]) + bg_ref[...]
    lane = lax.broadcasted_iota(I32, gpre.shape, 1)
    is_f = (lane >= ML_HEADS) & (lane < 2 * ML_HEADS)
    gt_ref[...] = jnp.where(is_f, _log_sigmoid(gpre), gpre)


def _in_projection(x2d, norm_g, w_in, b_igate, b_fgate, tm):
    T, D = x2d.shape
    sbw = SB_HEADS * HEAD_DIM
    mqk = ML_HEADS * HEAD_DIM
    mlw = ML_HEADS * ML_V_DIM
    w_bf = w_in.astype(BF16)
    o = 0
    w_specs = []
    for width in (sbw, sbw, sbw, mqk, mqk, mlw, mlw):
        assert o % width == 0
        w_specs.append(pl.BlockSpec((D, width), functools.partial(lambda blk, i: (0, blk), o // width),
                                    pipeline_mode=pl.Buffered(1)))
        o += width
    wg = jnp.zeros((D, LANES), F32).at[:, :2 * ML_HEADS].set(w_in[:, o:o + 2 * ML_HEADS]).astype(BF16)
    bg = jnp.zeros((1, LANES), F32).at[0, :ML_HEADS].set(b_igate).at[0, ML_HEADS:2 * ML_HEADS].set(b_fgate)
    row = lambda w: pl.BlockSpec((tm, w), lambda i: (i, 0))
    out_widths = (sbw, sbw, sbw, sbw, sbw, mqk, mqk, mlw, mlw, LANES)
    out_dtypes = (BF16, F32, BF16, F32, BF16, BF16, BF16, BF16, F32, F32)
    return pl.pallas_call(
        _inproj_kernel,
        grid=(T // tm,),
        in_specs=[row(D), _resident((1, D))] + w_specs + [_resident(wg.shape), _resident(bg.shape)],
        out_specs=[row(w) for w in out_widths],
        out_shape=[jax.ShapeDtypeStruct((T, w), dt) for w, dt in zip(out_widths, out_dtypes)],
        compiler_params=_cparams("parallel"),
        name="in_projection",
    )(x2d, norm_g.reshape(1, D), *([w_bf] * len(w_specs)), wg, bg)


def _suffix_matrix():
    j = lax.broadcasted_iota(I32, (KEY_BLOCK, 2 * KEY_BLOCK), 0)
    c = lax.broadcasted_iota(I32, (KEY_BLOCK, 2 * KEY_BLOCK), 1)
    return jnp.where((c >= KEY_BLOCK) | (j > c), 1.0, 0.0).astype(BF16)


def _sb_step(q, k, v, carry, umat, mask):
    R, bq, _ = q.shape
    z = jnp.einsum("rqd,rkd->rqk", q, k, preferred_element_type=F32)
    lp = jnp.log(1.0 + jnp.exp(-jnp.abs(z)))
    log_beta = jnp.minimum(z, 0.0) - lp
    log_stay = log_beta - z
    if mask is not None:
        log_stay = jnp.where(mask, log_stay, 0.0)
    hi, lo = _split2(log_stay)
    st = _mm(hi.reshape(R * bq, KEY_BLOCK), umat) + _mm(lo.reshape(R * bq, KEY_BLOCK), umat)
    st = st.reshape(R, bq, 2 * KEY_BLOCK)
    w = jnp.exp(log_beta + st[:, :, :KEY_BLOCK] + carry)
    if mask is not None:
        w = jnp.where(mask, w, 0.0)
    pv = jnp.einsum("rqk,rkd->rqd", w.astype(BF16), v, preferred_element_type=F32)
    return pv, carry + st[:, :, KEY_BLOCK:]


def _sb_prompt_kernel(q_ref, k_ref, v_ref, g_ref, o_ref, acc_ref, carry_ref, *, R):
    qi = pl.program_id(2)
    blk0 = qi * R
    q = q_ref[...].reshape(R, KEY_BLOCK, HEAD_DIM)
    umat = _suffix_matrix()
    shape3 = (R, KEY_BLOCK, KEY_BLOCK)
    t_io = lax.broadcasted_iota(I32, shape3, 1)
    s_io = lax.broadcasted_iota(I32, shape3, 2)
    r_io = lax.broadcasted_iota(I32, shape3, 0)

    def load(ref, d):
        return jnp.stack([ref[pl.ds(pl.multiple_of(jnp.maximum(blk0 + r - d, 0) * KEY_BLOCK, KEY_BLOCK),
                                    KEY_BLOCK), :] for r in range(R)])

    def penalty(d_next):
        return jnp.where(r_io < d_next - blk0, NEG_BIG, 0.0)

    pv, carry = _sb_step(q, load(k_ref, 0), load(v_ref, 0), jnp.zeros(shape3, F32), umat, s_io < t_io)
    acc_ref[...] = pv
    carry = carry + penalty(1)
    carry_ref[...] = carry

    def cond(state):
        d, mx = state
        return (d < blk0 + R) & (mx > EXP_ZERO_BELOW)

    def body(state):
        d, _ = state
        pv, carry = _sb_step(q, load(k_ref, d), load(v_ref, d), carry_ref[...], umat, None)
        acc_ref[...] += pv
        carry = carry + penalty(d + 1)
        carry_ref[...] = carry
        return d + 1, jnp.max(carry)

    lax.while_loop(cond, body, (jnp.int32(1), jnp.max(carry)))
    a = acc_ref[...]
    out = a * lax.rsqrt(jnp.mean(a * a, axis=-1, keepdims=True) + EPS) * g_ref[...]
    o_ref[...] = out.reshape(R * KEY_BLOCK, HEAD_DIM).astype(BF16)


def _sb_prompt(q, k, v, g_sb, B, S, R):
    tq = R * KEY_BLOCK
    nq = S // tq
    return pl.pallas_call(
        functools.partial(_sb_prompt_kernel, R=R),
        grid=(B, SB_HEADS, nq),
        in_specs=[pl.BlockSpec((tq, HEAD_DIM), lambda b, h, i: (b * nq + i, h)),
                  pl.BlockSpec((S, HEAD_DIM), lambda b, h, i: (b, h)),
                  pl.BlockSpec((S, HEAD_DIM), lambda b, h, i: (b, h)),
                  pl.BlockSpec((1, HEAD_DIM), lambda b, h, i: (0, h))],
        out_specs=pl.BlockSpec((tq, HEAD_DIM), lambda b, h, i: (b * nq + i, h)),
        out_shape=jax.ShapeDtypeStruct((B * S, SB_HEADS * HEAD_DIM), BF16),
        scratch_shapes=[pltpu.VMEM((R, KEY_BLOCK, HEAD_DIM), F32),
                        pltpu.VMEM((R, KEY_BLOCK, KEY_BLOCK), F32)],
        compiler_params=_cparams("parallel", "parallel", "parallel"),
        name="sb_prompt",
    )(q, k, v, g_sb.reshape(1, -1))


def _sb_sample_kernel(q_ref, kn_ref, vn_ref, ck_ref, cv_ref, g_ref, o_ref, acc_ref, carry_ref, *, S, P):
    H = SB_HEADS
    hs = lambda h: slice(h * HEAD_DIM, (h + 1) * HEAD_DIM)
    umat = _suffix_matrix()
    q = jnp.stack([q_ref[:, hs(h)] for h in range(H)])
    pad = jnp.zeros((KEY_BLOCK - S, HEAD_DIM), BF16)

    def new_keys(ref):
        return jnp.stack([jnp.concatenate([ref[:, hs(h)], pad], axis=0) for h in range(H)])

    def past_keys(ref, j):
        start = pl.multiple_of(j * KEY_BLOCK, KEY_BLOCK)
        return jnp.stack([ref[pl.ds(start, KEY_BLOCK), hs(h)].astype(BF16) for h in range(H)])

    shape3 = (H, S, KEY_BLOCK)
    t_io = lax.broadcasted_iota(I32, shape3, 1)
    s_io = lax.broadcasted_iota(I32, shape3, 2)
    pv, carry = _sb_step(q, new_keys(kn_ref), new_keys(vn_ref), jnp.zeros(shape3, F32), umat, s_io < t_io)
    acc_ref[...] = pv
    carry_ref[...] = carry

    def cond(state):
        j, mx = state
        return (j >= 0) & (mx > EXP_ZERO_BELOW)

    def body(state):
        j, _ = state
        pv, carry = _sb_step(q, past_keys(ck_ref, j), past_keys(cv_ref, j), carry_ref[...], umat, None)
        acc_ref[...] += pv
        carry_ref[...] = carry
        return j - 1, jnp.max(carry)

    lax.while_loop(cond, body, (jnp.int32(P // KEY_BLOCK - 1), jnp.max(carry)))
    a = acc_ref[...]
    a = a * lax.rsqrt(jnp.mean(a * a, axis=-1, keepdims=True) + EPS)
    for h in range(H):
        o_ref[:, hs(h)] = (a[h] * g_ref[:, hs(h)]).astype(BF16)


def _sb_sample(q, kn, vn, cache_k, cache_v, g_sb, B, S, P):
    W = SB_HEADS * HEAD_DIM
    row = pl.BlockSpec((S, W), lambda b: (b, 0))
    past = pl.BlockSpec((P, W), lambda b: (b, 0))
    return pl.pallas_call(
        functools.partial(_sb_sample_kernel, S=S, P=P),
        grid=(B,),
        in_specs=[row, row, row, past, past, pl.BlockSpec((1, W), lambda b: (0, 0))],
        out_specs=row,
        out_shape=jax.ShapeDtypeStruct((B * S, W), BF16),
        scratch_shapes=[pltpu.VMEM((SB_HEADS, S, HEAD_DIM), F32),
                        pltpu.VMEM((SB_HEADS, S, KEY_BLOCK), F32)],
        compiler_params=_cparams("parallel"),
        name="sb_sample",
    )(q, kn, vn, cache_k.reshape(B * P, W), cache_v.reshape(B * P, W), g_sb.reshape(1, W))


def _mlstm_kernel(q_ref, k_ref, v_ref, o_ref, gt_ref, c0_ref, n0_ref, m0_ref, g_ref,
                  out_ref, c_out_ref, n_out_ref, m_out_ref, cext_ref, m_ref, *, L, Lp):
    c = pl.program_id(1)
    H = ML_HEADS
    VW = ML_V_DIM + LANES
    lane_row = lax.broadcasted_iota(I32, (1, LANES), 1)
    onehot0 = jnp.where(lane_row == 0, 1.0, 0.0)

    @pl.when(c == 0)
    def _():
        for h in range(H):
            cext_ref[h] = jnp.concatenate([c0_ref[0, h], n0_ref[0, h] * onehot0], axis=1)
            m_ref[h] = jnp.broadcast_to(m0_ref[0, h], (8, LANES))

    def pad_rows(a, fill=0.0):
        if Lp == L:
            return a
        return jnp.concatenate([a, jnp.full((Lp - L, a.shape[1]), fill, a.dtype)], axis=0)

    gt = gt_ref[...]
    lane = lax.broadcasted_iota(I32, (Lp, LANES), 1)
    if Lp != L:
        gt = jnp.concatenate([gt, jnp.broadcast_to(jnp.where(lane_row < H, NEG_BIG, 0.0), (Lp - L, LANES))], axis=0)
    lf = jnp.where((lane >= H) & (lane < 2 * H), gt, 0.0)
    ti = lax.broadcasted_iota(I32, (Lp, Lp), 0)
    si = lax.broadcasted_iota(I32, (Lp, Lp), 1)
    causal = si <= ti
    tri = jnp.where(causal, 1.0, 0.0).astype(BF16)
    bc = sum(_mm(tri, p) for p in _split3(lf))
    bc_t = bc.T
    gt_t = gt.T
    ones_blk = jnp.broadcast_to(onehot0, (Lp, LANES)).astype(BF16)

    for h in range(H):
        b_col = bc[:, H + h:H + h + 1]
        b_row = bc_t[H + h:H + h + 1, :]
        i_col = gt[:, h:h + 1]
        i_row = gt_t[h:h + 1, :]
        m_prev = m_ref[h][0:1, 0:1]
        qh = pad_rows(q_ref[:, h * HEAD_DIM:(h + 1) * HEAD_DIM])
        kh = pad_rows(k_ref[:, h * HEAD_DIM:(h + 1) * HEAD_DIM])
        vh = pad_rows(v_ref[:, h * ML_V_DIM:(h + 1) * ML_V_DIM])
        vext = jnp.concatenate([vh, ones_blk], axis=1)
        cext = cext_ref[h]

        log_d = jnp.where(causal, b_col - b_row + i_row, NEG_BIG)
        log_inter = b_col + m_prev
        m_row = jnp.maximum(log_inter, jnp.max(log_d, axis=1, keepdims=True))
        dmat = jnp.exp(log_d - m_row)
        s = lax.dot_general(qh, kh, (((1,), (1,)), ((), ())), preferred_element_type=F32) * dmat
        inter = jnp.exp(log_inter - m_row)
        num = _mm(s.astype(BF16), vext) + inter * _mm(qh, cext.astype(BF16))
        den = num[:, ML_V_DIM:ML_V_DIM + 1]
        hh = num[:L, :ML_V_DIM] / jnp.maximum(jnp.abs(den), jnp.exp(-m_row))[:L]
        hn = hh * lax.rsqrt(jnp.mean(hh * hh, axis=-1, keepdims=True) + EPS)
        cols = slice(h * ML_V_DIM, (h + 1) * ML_V_DIM)
        ogate = 1.0 / (1.0 + jnp.exp(-o_ref[:, cols]))
        out_ref[:, cols] = (ogate * (hn * g_ref[:, cols])).astype(BF16)

        b_last = b_col[Lp - 1:Lp, :]
        log_w = b_last - b_col + i_col
        m_new = jnp.maximum(b_last + m_prev, jnp.max(log_w, axis=0, keepdims=True))
        wk = jnp.exp(log_w - m_new)
        decay = jnp.exp(b_last + m_prev - m_new)
        upd = lax.dot_general(kh, (wk * vext.astype(F32)).astype(BF16), (((0,), (0,)), ((), ())),
                              preferred_element_type=F32)
        cnew = decay * cext + upd
        cext_ref[h] = cnew
        m_ref[h] = jnp.broadcast_to(m_new, (8, LANES))

    @pl.when(c == pl.num_programs(1) - 1)
    def _():
        for h in range(H):
            cf = cext_ref[h]
            c_out_ref[0, h] = cf[:, :ML_V_DIM]
            n_out_ref[0, h] = cf[:, ML_V_DIM:ML_V_DIM + 1]
            m_out_ref[0, h] = m_ref[h][0:1, 0:1]


def _mlstm(qm, km, vm, om, gt, C0, n0, m0, g_ml, B, S, L):
    nc = S // L
    Lp = max(L, LANES)
    H = ML_HEADS
    VW = ML_V_DIM + LANES
    row = lambda w: pl.BlockSpec((L, w), lambda b, c: (b * nc + c, 0))
    st = lambda *tail: pl.BlockSpec((1, H) + tail, lambda b, c: (b, 0, 0, 0))
    return pl.pallas_call(
        functools.partial(_mlstm_kernel, L=L, Lp=Lp),
        grid=(B, nc),
        in_specs=[row(H * HEAD_DIM), row(H * HEAD_DIM), row(H * ML_V_DIM), row(H * ML_V_DIM), row(LANES),
                  st(HEAD_DIM, ML_V_DIM), st(HEAD_DIM, 1), st(1, 1),
                  pl.BlockSpec((1, H * ML_V_DIM), lambda b, c: (0, 0))],
        out_specs=[row(H * ML_V_DIM), st(HEAD_DIM, ML_V_DIM), st(HEAD_DIM, 1), st(1, 1)],
        out_shape=[jax.ShapeDtypeStruct((B * S, H * ML_V_DIM), BF16),
                   jax.ShapeDtypeStruct((B, H, HEAD_DIM, ML_V_DIM), F32),
                   jax.ShapeDtypeStruct((B, H, HEAD_DIM, 1), F32),
                   jax.ShapeDtypeStruct((B, H, 1, 1), F32)],
        scratch_shapes=[pltpu.VMEM((H, HEAD_DIM, VW), F32), pltpu.VMEM((H, 8, LANES), F32)],
        compiler_params=_cparams("parallel", "arbitrary"),
        name="mlstm",
    )(qm, km, vm, om, gt, C0, n0.reshape(B, H, HEAD_DIM, 1), m0.reshape(B, H, 1, 1), g_ml.reshape(1, -1))


def _pack_halves(lo, hi):
    lo_bits = pltpu.bitcast(lo.astype(BF16).astype(F32), jnp.uint32) >> 16
    hi_bits = pltpu.bitcast(hi.astype(BF16).astype(F32), jnp.uint32) & jnp.uint32(0xFFFF0000)
    return lo_bits | hi_bits


def _unpack_halves(w):
    lo = pltpu.bitcast(w << 16, F32).astype(BF16)
    hi = pltpu.bitcast(w & jnp.uint32(0xFFFF0000), F32).astype(BF16)
    return lo, hi


def _outproj_router_kernel(x_ref, sb_ref, ml_ref, wos_ref, wom_ref, g_ref, wr_hi_ref, wr_lo_ref, br_ref, c0_ref,
                           x2_ref, xn_ref, eid_ref, gate_ref, rank_ref, cnt_ref, carry_ref):
    i = pl.program_id(0)

    @pl.when(i == 0)
    def _():
        carry_ref[...] = c0_ref[...]

    x2 = x_ref[...] + _mm(sb_ref[...], wos_ref[...]) + _mm(ml_ref[...], wom_ref[...])
    x2_ref[...] = x2
    xn = _rms(x2, g_ref[...])
    half = xn.shape[1] // 2
    xn_ref[...] = _pack_halves(xn[:, :half], xn[:, half:])
    hi, lo = _split2(xn)
    logits = _mm(hi, wr_hi_ref[...]) + _mm(lo, wr_hi_ref[...]) + _mm(hi, wr_lo_ref[...]) + br_ref[...]
    tm = logits.shape[0]
    lane = lax.broadcasted_iota(I32, (tm, LANES), 1)
    vals, ids = [], []
    cnt = jnp.zeros((tm, LANES), F32)
    for _ in range(TOP_K):
        mx = jnp.max(logits, axis=-1, keepdims=True)
        idx = jnp.min(jnp.where(logits == mx, lane, LANES), axis=-1, keepdims=True)
        sel = lane == idx
        vals.append(mx)
        ids.append(idx)
        logits = jnp.where(sel, -jnp.inf, logits)
        cnt = cnt + jnp.where(sel, 1.0, 0.0)
    es = [jnp.exp(v - vals[0]) for v in vals]
    inv = 1.0 / sum(es)
    ti = lax.broadcasted_iota(I32, (tm, tm), 0)
    si = lax.broadcasted_iota(I32, (tm, tm), 1)
    before = _mm(jnp.where(si < ti, 1.0, 0.0).astype(BF16), cnt.astype(BF16)) + carry_ref[...]
    eid_o = jnp.zeros((tm, LANES), I32)
    gate_o = jnp.zeros((tm, LANES), F32)
    rank_o = jnp.zeros((tm, LANES), I32)
    for k in range(TOP_K):
        rk = jnp.sum(jnp.where(lane == ids[k], before, 0.0), axis=-1, keepdims=True)
        eid_o = jnp.where(lane == k, ids[k], eid_o)
        gate_o = jnp.where(lane == k, es[k] * inv, gate_o)
        rank_o = jnp.where(lane == k, rk.astype(I32), rank_o)
    eid_ref[...] = eid_o
    gate_ref[...] = gate_o
    rank_ref[...] = rank_o
    carry_ref[...] += jnp.sum(cnt, axis=0, keepdims=True)
    cnt_ref[...] = carry_ref[...]


def _outproj_router(x2d, sb_o, ml_o, w_out, norm_g, w_router, b_router, counts_in, tm):
    T, D = x2d.shape
    W = sb_o.shape[1]
    assert w_out.shape[0] == 2 * W
    w_bf = w_out.astype(BF16)
    half_w = lambda blk: pl.BlockSpec((W, D), lambda i: (blk, 0), pipeline_mode=pl.Buffered(1))
    wr = jnp.zeros((D, LANES), F32).at[:, :N_EXPERTS].set(w_router)
    wr_hi = wr.astype(BF16)
    wr_lo = (wr - wr_hi.astype(F32)).astype(BF16)
    br = jnp.full((1, LANES), NEG_BIG, F32).at[0, :N_EXPERTS].set(b_router)
    row = lambda w: pl.BlockSpec((tm, w), lambda i: (i, 0))
    return pl.pallas_call(
        _outproj_router_kernel,
        grid=(T // tm,),
        in_specs=[row(D), row(W), row(W), half_w(0), half_w(1),
                  _resident((1, D)), _resident(wr_hi.shape), _resident(wr_lo.shape), _resident(br.shape),
                  _resident((1, LANES))],
        out_specs=[row(D), row(D // 2), row(LANES), row(LANES), row(LANES),
                   pl.BlockSpec((1, LANES), lambda i: (0, 0))],
        out_shape=[jax.ShapeDtypeStruct((T, D), F32), jax.ShapeDtypeStruct((T, D // 2), jnp.uint32),
                   jax.ShapeDtypeStruct((T, LANES), I32), jax.ShapeDtypeStruct((T, LANES), F32),
                   jax.ShapeDtypeStruct((T, LANES), I32), jax.ShapeDtypeStruct((1, LANES), F32)],
        scratch_shapes=[pltpu.VMEM((1, LANES), F32)],
        compiler_params=_cparams("arbitrary"),
        name="outproj_router",
    )(x2d, sb_o, ml_o, w_bf, w_bf, norm_g.reshape(1, D), wr_hi, wr_lo, br, counts_in)


ROW_BLOCK = 256
IT_TILE, IT_COL, IT_EXPERT, IT_FIRST, IT_BLOCKS, IT_NEXT_EXPERT, IT_NEXT_COL = range(7)


def _segment_weights(it_ref, w, copies, cast):
    @pl.when(it_ref[IT_FIRST, w] == 1)
    def _():
        @pl.when(w == 0)
        def _():
            for c in copies(it_ref[IT_EXPERT, w], it_ref[IT_COL, w]):
                c.start()

        for c in copies(it_ref[IT_EXPERT, w], it_ref[IT_COL, w]):
            c.wait()
        cast()

        @pl.when(it_ref[IT_NEXT_EXPERT, w] >= 0)
        def _():
            for c in copies(it_ref[IT_NEXT_EXPERT, w], it_ref[IT_NEXT_COL, w]):
                c.start()


def _for_filled_rows(blocks, tm, compute):
    for lvl in range(1, tm // ROW_BLOCK + 1):
        @pl.when(blocks == lvl)
        def _():
            compute(lvl * ROW_BLOCK)


def _gate_up_kernel(it_ref, n_ref, x_ref, w_hbm, bg_ref, bl_ref, act_ref, stage, wg_s, wl_s, x_s, sems, *, nj):
    w = pl.program_id(0)
    tm, half = x_ref.shape
    tn = act_ref.shape[1]

    def copies(e, j):
        return [pltpu.make_async_copy(w_hbm.at[e, :, pl.ds(pl.multiple_of((h * nj + j) * tn, tn), tn)],
                                      stage.at[h], sems.at[h]) for h in range(2)]

    def cast():
        wg_s[...] = stage[0].astype(BF16)
        wl_s[...] = stage[1].astype(BF16)

    @pl.when(w < n_ref[0])
    def _():
        _segment_weights(it_ref, w, copies, cast)

        def compute(rows):
            lo, hi = _unpack_halves(x_ref[:rows, :])
            x_s[:rows, :half] = lo
            x_s[:rows, half:] = hi
            x = x_s[:rows, :]
            glu = jnp.minimum(_mm(x, wg_s[...]) + bg_ref[0], SWIGLU_LIMIT)
            lin = jnp.clip(_mm(x, wl_s[...]) + bl_ref[0], -SWIGLU_LIMIT, SWIGLU_LIMIT)
            act = glu * (1.0 / (1.0 + jnp.exp(-SWIGLU_ALPHA * glu))) * (lin + 1.0)
            act_ref[:rows, :] = act.astype(BF16)
            if rows < tm:
                act_ref[rows:, :] = jnp.zeros((tm - rows, tn), BF16)

        _for_filled_rows(it_ref[IT_BLOCKS, w], tm, compute)

    @pl.when(w >= n_ref[0])
    def _():
        act_ref[...] = jnp.zeros_like(act_ref)


def _down_kernel(it_ref, n_ref, a_ref, w_hbm, b_ref, o_ref, stage, w_s, sem):
    w = pl.program_id(0)
    tm = a_ref.shape[0]
    tn = w_s.shape[1]

    def copies(e, j):
        return [pltpu.make_async_copy(w_hbm.at[e, :, pl.ds(pl.multiple_of(j * tn, tn), tn)], stage, sem)]

    def cast():
        w_s[...] = stage[...].astype(BF16)

    @pl.when(w < n_ref[0])
    def _():
        _segment_weights(it_ref, w, copies, cast)

        def compute(rows):
            out = _mm(a_ref[:rows, :], w_s[...]) + b_ref[0]
            o_ref[:rows, :] = _pack_halves(out[:, :tn // 2], out[:, tn // 2:])
            if rows < tm:
                o_ref[rows:, :] = jnp.zeros((tm - rows, tn // 2), jnp.uint32)

        _for_filled_rows(it_ref[IT_BLOCKS, w], tm, compute)

    @pl.when(w >= n_ref[0])
    def _():
        o_ref[...] = jnp.zeros_like(o_ref)


def _work_list(counts, tm, nj, n_tiles_max):
    tpe = (counts + tm - 1) // tm
    tile_start = jnp.cumsum(tpe) - tpe
    item_end = jnp.cumsum(tpe * nj)
    item_start = item_end - tpe * nj
    n_items = item_end[-1:]
    w = jnp.arange(n_tiles_max * nj, dtype=I32)
    wc = jnp.minimum(w, n_items[0] - 1)
    e = jnp.minimum(jnp.sum((wc[:, None] >= item_end[None, :]).astype(I32), axis=1), N_EXPERTS - 1)
    local = wc - item_start[e]
    t = jnp.maximum(tpe[e], 1)
    j = local // t
    il = local - j * t
    real = w < n_items[0]
    first = ((il == 0) & real).astype(I32)
    tail = jnp.maximum(w - n_items[0], 0)
    tile = jnp.where(real, tile_start[e] + il, jnp.sum(tpe) + tail // nj)
    j = jnp.where(real, j, tail % nj)
    blocks = (jnp.clip(counts[e] - il * tm, 1, tm) + ROW_BLOCK - 1) // ROW_BLOCK
    ids = jnp.where(tpe > 0, jnp.arange(N_EXPERTS, dtype=I32), N_EXPERTS)
    later = jnp.concatenate([lax.cummin(ids[::-1])[::-1][1:], jnp.full((1,), N_EXPERTS, I32)])
    next_nonempty = jnp.where(later < N_EXPERTS, later, -1)
    last_col = j + 1 >= nj
    next_e = jnp.where(last_col, next_nonempty[e], e)
    next_j = jnp.where(last_col, 0, j + 1)
    table = jnp.stack([tile, j, e, first, blocks, next_e, next_j]).astype(I32)
    return table, n_items.astype(I32)


def _expert_ffn(xs, counts, w_gu, b_gu, w_dn, b_dn, tm, tn):
    n_rows, half = xs.shape
    D = 2 * half
    F = w_dn.shape[1]
    n_tiles = n_rows // tm
    nj = F // tn
    row_tile = lambda w, it, n: (it[IT_TILE, w], 0)
    out_tile = lambda w, it, n: (it[IT_TILE, w], it[IT_COL, w])
    act = pl.pallas_call(
        functools.partial(_gate_up_kernel, nj=nj),
        grid_spec=pltpu.PrefetchScalarGridSpec(
            num_scalar_prefetch=2,
            grid=(n_tiles * nj,),
            in_specs=[pl.BlockSpec((tm, half), row_tile),
                      pl.BlockSpec(memory_space=pl.ANY),
                      pl.BlockSpec((1, 1, tn), lambda w, it, n: (it[IT_EXPERT, w], 0, it[IT_COL, w])),
                      pl.BlockSpec((1, 1, tn), lambda w, it, n: (it[IT_EXPERT, w], 0, nj + it[IT_COL, w]))],
            out_specs=pl.BlockSpec((tm, tn), out_tile),
            scratch_shapes=[pltpu.VMEM((2, D, tn), F32), pltpu.VMEM((D, tn), BF16), pltpu.VMEM((D, tn), BF16),
                            pltpu.VMEM((tm, D), BF16), pltpu.SemaphoreType.DMA((2,))]),
        out_shape=jax.ShapeDtypeStruct((n_rows, F), BF16),
        compiler_params=_cparams("arbitrary"),
        name="expert_gate_up",
    )(*_work_list(counts, tm, nj, n_tiles), xs, w_gu, b_gu, b_gu)

    njd = D // tn
    return pl.pallas_call(
        _down_kernel,
        grid_spec=pltpu.PrefetchScalarGridSpec(
            num_scalar_prefetch=2,
            grid=(n_tiles * njd,),
            in_specs=[pl.BlockSpec((tm, F), row_tile),
                      pl.BlockSpec(memory_space=pl.ANY),
                      pl.BlockSpec((1, 1, tn), lambda w, it, n: (it[IT_EXPERT, w], 0, it[IT_COL, w]))],
            out_specs=pl.BlockSpec((tm, tn // 2), out_tile),
            scratch_shapes=[pltpu.VMEM((F, tn), F32), pltpu.VMEM((F, tn), BF16), pltpu.SemaphoreType.DMA]),
        out_shape=jax.ShapeDtypeStruct((n_rows, D // 2), jnp.uint32),
        compiler_params=_cparams("arbitrary"),
        name="expert_down",
    )(*_work_list(counts, tm, njd, n_tiles), act, w_dn, b_dn)


def _scatter_kernel(zero_ref, dest_ref, xa_ref, xb_ref, xs_ref, zbuf, sem, zsem, *, tiles_a, slot_tile):
    i = pl.program_id(0)
    tm = xa_ref.shape[0]

    @pl.when(i == 0)
    def _():
        zbuf[...] = jnp.zeros_like(zbuf)

        def tile_copy(z):
            start = pl.multiple_of(zero_ref[1 + z] * slot_tile, slot_tile)
            return pltpu.make_async_copy(zbuf, xs_ref.at[pl.ds(start, slot_tile), :], zsem)

        def start(z, carry):
            tile_copy(z).start()
            return carry

        def wait(z, carry):
            tile_copy(z).wait()
            return carry

        lax.fori_loop(0, zero_ref[0], start, 0)
        lax.fori_loop(0, zero_ref[0], wait, 0)

    def scatter_from(x_ref):
        def issue(r, carry):
            for k in range(TOP_K):
                pltpu.make_async_copy(x_ref.at[pl.ds(r, 1), :], xs_ref.at[pl.ds(dest_ref[r * TOP_K + k], 1), :],
                                      sem).start()
            return carry

        lax.fori_loop(0, tm, issue, 0, unroll=8)
        for _ in range(TOP_K):
            pltpu.make_async_copy(x_ref, xs_ref.at[pl.ds(0, tm), :], sem).wait()

    @pl.when(i < tiles_a)
    def _():
        scatter_from(xa_ref)

    @pl.when(i >= tiles_a)
    def _():
        scatter_from(xb_ref)


def _scatter_rows(xa, xb, dest, zero_list, n_rows, tm, slot_tile):
    Ta, half = xa.shape
    Tb = xb.shape[0]
    tiles_a, tiles_b = Ta // tm, Tb // tm
    return pl.pallas_call(
        functools.partial(_scatter_kernel, tiles_a=tiles_a, slot_tile=slot_tile),
        grid_spec=pltpu.PrefetchScalarGridSpec(
            num_scalar_prefetch=1,
            grid=(tiles_a + tiles_b,),
            in_specs=[pl.BlockSpec((tm * TOP_K,), lambda i, z: (i,), memory_space=pltpu.SMEM),
                      pl.BlockSpec((tm, half), lambda i, z: (jnp.minimum(i, tiles_a - 1), 0)),
                      pl.BlockSpec((tm, half), lambda i, z: (jnp.maximum(i - tiles_a, 0), 0))],
            out_specs=pl.BlockSpec(memory_space=pl.ANY),
            scratch_shapes=[pltpu.VMEM((slot_tile, half), jnp.uint32), pltpu.SemaphoreType.DMA,
                            pltpu.SemaphoreType.DMA]),
        out_shape=jax.ShapeDtypeStruct((n_rows, half), jnp.uint32),
        compiler_params=_cparams("arbitrary"),
        name="scatter_rows",
    )(zero_list, dest.reshape((Ta + Tb) * TOP_K), xa, xb)


COMBINE_PARTS = 2


def _combine_kernel(dest_ref, x2_ref, gate_ref, g_ref, out_ref, y_ref, buf, sems, *, col_tile):
    tm, D = x2_ref.shape
    hw = col_tile // 2

    def unpack(w):
        lo = pltpu.bitcast(w << 16, F32)
        hi = pltpu.bitcast(w & jnp.uint32(0xFFFF0000), F32)
        parts = []
        for j in range(D // col_tile):
            parts += [lo[:, j * hw:(j + 1) * hw], hi[:, j * hw:(j + 1) * hw]]
        return jnp.concatenate(parts, axis=1)

    rows_per_part = tm // COMBINE_PARTS

    def issue(r, part):
        for k in range(TOP_K):
            pltpu.make_async_copy(out_ref.at[pl.ds(dest_ref[r * TOP_K + k], 1), :], buf.at[k, pl.ds(r, 1), :],
                                  sems.at[part]).start()
        return part

    for part in range(COMBINE_PARTS):
        lax.fori_loop(part * rows_per_part, (part + 1) * rows_per_part, issue, part, unroll=8)
    for part in range(COMBINE_PARTS):
        rows = pl.ds(part * rows_per_part, rows_per_part)
        for k in range(TOP_K):
            pltpu.make_async_copy(out_ref.at[pl.ds(0, rows_per_part), :], buf.at[k, rows, :], sems.at[part]).wait()
        gate = gate_ref[rows, :]
        acc = x2_ref[rows, :]
        for k in range(TOP_K):
            acc = acc + gate[:, k:k + 1] * unpack(buf[k, rows, :])
        y_ref[rows, :] = _rms(acc, g_ref[...])


def _combine(x2, gate, dest, out, final_g, tm, col_tile):
    T, D = x2.shape
    row = lambda w: pl.BlockSpec((tm, w), lambda i: (i, 0))
    return pl.pallas_call(
        functools.partial(_combine_kernel, col_tile=col_tile),
        grid=(T // tm,),
        in_specs=[pl.BlockSpec((tm * TOP_K,), lambda i: (i,), memory_space=pltpu.SMEM),
                  row(D), row(LANES), _resident((1, D)), pl.BlockSpec(memory_space=pl.ANY)],
        out_specs=row(D),
        out_shape=jax.ShapeDtypeStruct((T, D), F32),
        scratch_shapes=[pltpu.VMEM((TOP_K, tm, D // 2), jnp.uint32), pltpu.SemaphoreType.DMA((COMBINE_PARTS,))],
        compiler_params=_cparams("arbitrary"),
        name="combine_norm",
    )(dest.reshape(T * TOP_K), x2, gate, final_g.reshape(1, D), out)


def _tile(n, pref):
    return pref if n % pref == 0 else n


def _mixers(x, past, lw):
    (norm_mix_g, w_in, b_igate, b_fgate, g_sb_out, g_ml_out) = lw
    B, S, D = x.shape
    T = B * S
    x2d = x.reshape(T, D)
    q, kf, kb, vf, vb, qm, km, vm, om, gt = _in_projection(x2d, norm_mix_g, w_in, b_igate, b_fgate, _tile(T, 256))
    if past is None:
        sb_o = _sb_prompt(q, kb, vb, g_sb_out, B, S, R=min(8, S // KEY_BLOCK))
        C0 = jnp.zeros((B, ML_HEADS, HEAD_DIM, ML_V_DIM), F32)
        n0 = jnp.zeros((B, ML_HEADS, HEAD_DIM), F32)
        m0 = jnp.zeros((B, ML_HEADS), F32)
        L = _tile(S, 256)
    else:
        cache_k, cache_v, C0, n0, m0 = past
        sb_o = _sb_sample(q, kb, vb, cache_k, cache_v, g_sb_out, B, S, cache_k.shape[1])
        L = S
    ml_o, C, n, m = _mlstm(qm, km, vm, om, gt, C0, n0, m0, g_ml_out, B, S, L)
    state = (kf.reshape(B, S, SB_HEADS, HEAD_DIM), vf.reshape(B, S, SB_HEADS, HEAD_DIM),
             C, n.reshape(B, ML_HEADS, HEAD_DIM), m.reshape(B, ML_HEADS))
    return (x2d, sb_o, ml_o), state


MOE_ROW_TILE = 1024
MOE_COL_TILE = 1024
COMBINE_TILE = 256


def kernel(x_prompt, x_sample, cache_k, cache_v, state_C, state_n, state_m, norm_mix_g, w_in, b_igate, b_fgate,
           g_sb_out, g_ml_out, w_out, norm_ffn_g, w_router, b_router, w_gate_up, b_gate_up, w_down, b_down,
           final_norm_g):
    assert w_in.shape[0] == 1, "single-layer trunk"
    lw = (norm_mix_g[0], w_in[0], b_igate[0], b_fgate[0], g_sb_out[0], g_ml_out[0])
    E = w_gate_up.shape[1]
    D = x_prompt.shape[-1]
    groups = [_mixers(x_prompt, None, lw),
              _mixers(x_sample, (cache_k[0], cache_v[0], state_C[0], state_n[0], state_m[0]), lw)]

    counts = jnp.zeros((1, LANES), F32)
    routed = []
    for (x2d, sb_o, ml_o), _ in groups:
        x2, xn2, eid, gate, rank, counts = _outproj_router(x2d, sb_o, ml_o, w_out[0], norm_ffn_g[0], w_router[0],
                                                           b_router[0], counts, _tile(x2d.shape[0], 512))
        routed.append((x2, xn2, eid, gate, rank))
    tm = MOE_ROW_TILE
    cnt = counts[0, :N_EXPERTS].astype(I32)
    padded = (cnt + tm - 1) // tm * tm
    pstart = jnp.cumsum(padded) - padded
    n_assign = sum(r[0].shape[0] for r in routed) * TOP_K
    n_rows = (-(-n_assign // tm) + N_EXPERTS) * tm
    dests = [pstart[eid[:, :TOP_K]] + rank[:, :TOP_K] for (_, _, eid, _, rank) in routed]

    tpe = padded // tm
    n_tiles = n_rows // tm
    tile_ids = jnp.arange(n_tiles, dtype=I32)
    last_of_expert = jnp.any((tile_ids[:, None] == (jnp.cumsum(tpe) - 1)[None, :]) & (tpe[None, :] > 0), axis=1)
    needs_zero = last_of_expert | (tile_ids >= jnp.sum(tpe))
    zero_list = jnp.concatenate([jnp.sum(needs_zero.astype(I32))[None],
                                 jnp.nonzero(needs_zero, size=n_tiles, fill_value=0)[0].astype(I32)])
    xs = _scatter_rows(routed[0][1], routed[1][1], jnp.concatenate(dests, axis=0), zero_list, n_rows,
                       _tile(routed[1][1].shape[0], 512), tm)
    out = _expert_ffn(xs, cnt, w_gate_up[0], b_gate_up[0].reshape(E, 1, -1), w_down[0],
                      b_down[0].reshape(E, 1, -1), tm, MOE_COL_TILE)
    ys = [_combine(x2, gate, dest, out, final_norm_g, _tile(x2.shape[0], COMBINE_TILE), MOE_COL_TILE)
          for (x2, _, _, gate, _), dest in zip(routed, dests)]

    (kp, vp, Cp, np_, mp), (ks, vs, Cs, ns, ms) = groups[0][1], groups[1][1]
    return (ys[0].reshape(x_prompt.shape), ys[1].reshape(x_sample.shape),
            kp[None], vp[None], Cp[None], np_[None], mp[None], ks[None], vs[None], Cs[None], ns[None], ms[None])
```

```python
import functools

import jax
import jax.numpy as jnp
from jax import lax
from jax.experimental import pallas as pl
from jax.experimental.pallas import tpu as pltpu

F32 = jnp.float32
BF16 = jnp.bfloat16
I32 = jnp.int32

EPS = 1e-6
SB_HEADS = 8
HEAD_DIM = 128
ML_HEADS = 4
ML_V_DIM = 256
N_EXPERTS = 32
TOP_K = 4
SWIGLU_ALPHA = 1.702
SWIGLU_LIMIT = 7.0
LANES = 128
KEY_BLOCK = 128
NEG_BIG = -1e30
EXP_ZERO_BELOW = -105.0
VMEM_LIMIT = 56 * 1024 * 1024


def _cparams(*sem, vmem_limit=VMEM_LIMIT):
    return pltpu.CompilerParams(dimension_semantics=sem, vmem_limit_bytes=vmem_limit)


def _resident(shape):
    nd = len(shape)
    return pl.BlockSpec(shape, lambda *_: (0,) * nd, pipeline_mode=pl.Buffered(1))


def _rms(x, g):
    return x * lax.rsqrt(jnp.mean(x * x, axis=-1, keepdims=True) + EPS) * g


def _log_sigmoid(z):
    return jnp.minimum(z, 0.0) - jnp.log(1.0 + jnp.exp(-jnp.abs(z)))


def _split2(x):
    hi = x.astype(BF16)
    lo = (x - hi.astype(F32)).astype(BF16)
    return hi, lo


def _split3(x):
    h1 = x.astype(BF16)
    r = x - h1.astype(F32)
    h2 = r.astype(BF16)
    h3 = (r - h2.astype(F32)).astype(BF16)
    return h1, h2, h3


def _mm(a, b):
    return jnp.dot(a, b, preferred_element_type=F32)


def _inproj_kernel(x_ref, g_ref, wq_ref, wk_ref, wv_ref, wqm_ref, wkm_ref, wvm_ref, wom_ref,
                   wg_ref, bg_ref,
                   q_ref, kf_ref, kb_ref, vf_ref, vb_ref, qm_ref, km_ref, vm_ref, om_ref, gt_ref):
    xn = _rms(x_ref[...], g_ref[...]).astype(BF16)
    q_ref[...] = (_mm(xn, wq_ref[...]) * (HEAD_DIM ** -0.5)).astype(BF16)
    k = _mm(xn, wk_ref[...])
    kf_ref[...] = k
    kb_ref[...] = k.astype(BF16)
    v = _mm(xn, wv_ref[...])
    vf_ref[...] = v
    vb_ref[...] = v.astype(BF16)
    qm_ref[...] = _mm(xn, wqm_ref[...]).astype(BF16)
    km_ref[...] = (_mm(xn, wkm_ref[...]) * (HEAD_DIM ** -0.5)).astype(BF16)
    vm_ref[...] = _mm(xn, wvm_ref[...]).astype(BF16)
    om_ref[...] = _mm(xn, wom_ref[...])
    gpre = _mm(xn, wg_ref[...]) + bg_ref[...]
    lane = lax.broadcasted_iota(I32, gpre.shape, 1)
    is_f = (lane >= ML_HEADS) & (lane < 2 * ML_HEADS)
    gt_ref[...] = jnp.where(is_f, _log_sigmoid(gpre), gpre)


def _in_projection(x2d, norm_g, w_in, b_igate, b_fgate, tm):
    T, D = x2d.shape
    sbw = SB_HEADS * HEAD_DIM
    mqk = ML_HEADS * HEAD_DIM
    mlw = ML_HEADS * ML_V_DIM
    w_bf = w_in.astype(BF16)
    o = 0
    w_specs = []
    for width in (sbw, sbw, sbw, mqk, mqk, mlw, mlw):
        assert o % width == 0
        w_specs.append(pl.BlockSpec((D, width), functools.partial(lambda blk, i: (0, blk), o // width),
                                    pipeline_mode=pl.Buffered(1)))
        o += width
    wg = jnp.zeros((D, LANES), F32).at[:, :2 * ML_HEADS].set(w_in[:, o:o + 2 * ML_HEADS]).astype(BF16)
    bg = jnp.zeros((1, LANES), F32).at[0, :ML_HEADS].set(b_igate).at[0, ML_HEADS:2 * ML_HEADS].set(b_fgate)
    row = lambda w: pl.BlockSpec((tm, w), lambda i: (i, 0))
    out_widths = (sbw, sbw, sbw, sbw, sbw, mqk, mqk, mlw, mlw, LANES)
    out_dtypes = (BF16, F32, BF16, F32, BF16, BF16, BF16, BF16, F32, F32)
    return pl.pallas_call(
        _inproj_kernel,
        grid=(T // tm,),
        in_specs=[row(D), _resident((1, D))] + w_specs + [_resident(wg.shape), _resident(bg.shape)],
        out_specs=[row(w) for w in out_widths],
        out_shape=[jax.ShapeDtypeStruct((T, w), dt) for w, dt in zip(out_widths, out_dtypes)],
        compiler_params=_cparams("parallel"),
        name="in_projection",
    )(x2d, norm_g.reshape(1, D), *([w_bf] * len(w_specs)), wg, bg)


def _suffix_matrix():
    j = lax.broadcasted_iota(I32, (KEY_BLOCK, 2 * KEY_BLOCK), 0)
    c = lax.broadcasted_iota(I32, (KEY_BLOCK, 2 * KEY_BLOCK), 1)
    return jnp.where((c >= KEY_BLOCK) | (j > c), 1.0, 0.0).astype(BF16)


def _sb_step(q, k, v, carry, umat, mask):
    R, bq, _ = q.shape
    z = jnp.einsum("rqd,rkd->rqk", q, k, preferred_element_type=F32)
    lp = jnp.log(1.0 + jnp.exp(-jnp.abs(z)))
    log_beta = jnp.minimum(z, 0.0) - lp
    log_stay = log_beta - z
    if mask is not None:
        log_stay = jnp.where(mask, log_stay, 0.0)
    hi, lo = _split2(log_stay)
    st = _mm(hi.reshape(R * bq, KEY_BLOCK), umat) + _mm(lo.reshape(R * bq, KEY_BLOCK), umat)
    st = st.reshape(R, bq, 2 * KEY_BLOCK)
    w = jnp.exp(log_beta + st[:, :, :KEY_BLOCK] + carry)
    if mask is not None:
        w = jnp.where(mask, w, 0.0)
    pv = jnp.einsum("rqk,rkd->rqd", w.astype(BF16), v, preferred_element_type=F32)
    return pv, carry + st[:, :, KEY_BLOCK:]


def _sb_prompt_kernel(q_ref, k_ref, v_ref, g_ref, o_ref, acc_ref, carry_ref, *, R):
    qi = pl.program_id(2)
    blk0 = qi * R
    q = q_ref[...].reshape(R, KEY_BLOCK, HEAD_DIM)
    umat = _suffix_matrix()
    shape3 = (R, KEY_BLOCK, KEY_BLOCK)
    t_io = lax.broadcasted_iota(I32, shape3, 1)
    s_io = lax.broadcasted_iota(I32, shape3, 2)
    r_io = lax.broadcasted_iota(I32, shape3, 0)

    def load(ref, d):
        return jnp.stack([ref[pl.ds(pl.multiple_of(jnp.maximum(blk0 + r - d, 0) * KEY_BLOCK, KEY_BLOCK),
                                    KEY_BLOCK), :] for r in range(R)])

    def penalty(d_next):
        return jnp.where(r_io < d_next - blk0, NEG_BIG, 0.0)

    pv, carry = _sb_step(q, load(k_ref, 0), load(v_ref, 0), jnp.zeros(shape3, F32), umat, s_io < t_io)
    acc_ref[...] = pv
    carry = carry + penalty(1)
    carry_ref[...] = carry

    def cond(state):
        d, mx = state
        return (d < blk0 + R) & (mx > EXP_ZERO_BELOW)

    def body(state):
        d, _ = state
        pv, carry = _sb_step(q, load(k_ref, d), load(v_ref, d), carry_ref[...], umat, None)
        acc_ref[...] += pv
        carry = carry + penalty(d + 1)
        carry_ref[...] = carry
        return d + 1, jnp.max(carry)

    lax.while_loop(cond, body, (jnp.int32(1), jnp.max(carry)))
    a = acc_ref[...]
    out = a * lax.rsqrt(jnp.mean(a * a, axis=-1, keepdims=True) + EPS) * g_ref[...]
    o_ref[...] = out.reshape(R * KEY_BLOCK, HEAD_DIM).astype(BF16)


def _sb_prompt(q, k, v, g_sb, B, S, R):
    tq = R * KEY_BLOCK
    nq = S // tq
    return pl.pallas_call(
        functools.partial(_sb_prompt_kernel, R=R),
        grid=(B, SB_HEADS, nq),
        in_specs=[pl.BlockSpec((tq, HEAD_DIM), lambda b, h, i: (b * nq + i, h)),
                  pl.BlockSpec((S, HEAD_DIM), lambda b, h, i: (b, h)),
                  pl.BlockSpec((S, HEAD_DIM), lambda b, h, i: (b, h)),
                  pl.BlockSpec((1, HEAD_DIM), lambda b, h, i: (0, h))],
        out_specs=pl.BlockSpec((tq, HEAD_DIM), lambda b, h, i: (b * nq + i, h)),
        out_shape=jax.ShapeDtypeStruct((B * S, SB_HEADS * HEAD_DIM), BF16),
        scratch_shapes=[pltpu.VMEM((R, KEY_BLOCK, HEAD_DIM), F32),
                        pltpu.VMEM((R, KEY_BLOCK, KEY_BLOCK), F32)],
        compiler_params=_cparams("parallel", "parallel", "parallel"),
        name="sb_prompt",
    )(q, k, v, g_sb.reshape(1, -1))


def _sb_sample_kernel(q_ref, kn_ref, vn_ref, ck_ref, cv_ref, g_ref, o_ref, acc_ref, carry_ref, *, S, P):
    H = SB_HEADS
    hs = lambda h: slice(h * HEAD_DIM, (h + 1) * HEAD_DIM)
    umat = _suffix_matrix()
    q = jnp.stack([q_ref[:, hs(h)] for h in range(H)])
    pad = jnp.zeros((KEY_BLOCK - S, HEAD_DIM), BF16)

    def new_keys(ref):
        return jnp.stack([jnp.concatenate([ref[:, hs(h)], pad], axis=0) for h in range(H)])

    def past_keys(ref, j):
        start = pl.multiple_of(j * KEY_BLOCK, KEY_BLOCK)
        return jnp.stack([ref[pl.ds(start, KEY_BLOCK), hs(h)].astype(BF16) for h in range(H)])

    shape3 = (H, S, KEY_BLOCK)
    t_io = lax.broadcasted_iota(I32, shape3, 1)
    s_io = lax.broadcasted_iota(I32, shape3, 2)
    pv, carry = _sb_step(q, new_keys(kn_ref), new_keys(vn_ref), jnp.zeros(shape3, F32), umat, s_io < t_io)
    acc_ref[...] = pv
    carry_ref[...] = carry

    def cond(state):
        j, mx = state
        return (j >= 0) & (mx > EXP_ZERO_BELOW)

    def body(state):
        j, _ = state
        pv, carry = _sb_step(q, past_keys(ck_ref, j), past_keys(cv_ref, j), carry_ref[...], umat, None)
        acc_ref[...] += pv
        carry_ref[...] = carry
        return j - 1, jnp.max(carry)

    lax.while_loop(cond, body, (jnp.int32(P // KEY_BLOCK - 1), jnp.max(carry)))
    a = acc_ref[...]
    a = a * lax.rsqrt(jnp.mean(a * a, axis=-1, keepdims=True) + EPS)
    for h in range(H):
        o_ref[:, hs(h)] = (a[h] * g_ref[:, hs(h)]).astype(BF16)


def _sb_sample(q, kn, vn, cache_k, cache_v, g_sb, B, S, P):
    W = SB_HEADS * HEAD_DIM
    row = pl.BlockSpec((S, W), lambda b: (b, 0))
    past = pl.BlockSpec((P, W), lambda b: (b, 0))
    return pl.pallas_call(
        functools.partial(_sb_sample_kernel, S=S, P=P),
        grid=(B,),
        in_specs=[row, row, row, past, past, pl.BlockSpec((1, W), lambda b: (0, 0))],
        out_specs=row,
        out_shape=jax.ShapeDtypeStruct((B * S, W), BF16),
        scratch_shapes=[pltpu.VMEM((SB_HEADS, S, HEAD_DIM), F32),
                        pltpu.VMEM((SB_HEADS, S, KEY_BLOCK), F32)],
        compiler_params=_cparams("parallel"),
        name="sb_sample",
    )(q, kn, vn, cache_k.reshape(B * P, W), cache_v.reshape(B * P, W), g_sb.reshape(1, W))


def _mlstm_kernel(q_ref, k_ref, v_ref, o_ref, gt_ref, c0_ref, n0_ref, m0_ref, g_ref,
                  out_ref, c_out_ref, n_out_ref, m_out_ref, cext_ref, m_ref, *, L, Lp):
    c = pl.program_id(1)
    H = ML_HEADS
    VW = ML_V_DIM + LANES
    lane_row = lax.broadcasted_iota(I32, (1, LANES), 1)
    onehot0 = jnp.where(lane_row == 0, 1.0, 0.0)

    @pl.when(c == 0)
    def _():
        for h in range(H):
            cext_ref[h] = jnp.concatenate([c0_ref[0, h], n0_ref[0, h] * onehot0], axis=1)
            m_ref[h] = jnp.broadcast_to(m0_ref[0, h], (8, LANES))

    def pad_rows(a, fill=0.0):
        if Lp == L:
            return a
        return jnp.concatenate([a, jnp.full((Lp - L, a.shape[1]), fill, a.dtype)], axis=0)

    gt = gt_ref[...]
    lane = lax.broadcasted_iota(I32, (Lp, LANES), 1)
    if Lp != L:
        gt = jnp.concatenate([gt, jnp.broadcast_to(jnp.where(lane_row < H, NEG_BIG, 0.0), (Lp - L, LANES))], axis=0)
    lf = jnp.where((lane >= H) & (lane < 2 * H), gt, 0.0)
    ti = lax.broadcasted_iota(I32, (Lp, Lp), 0)
    si = lax.broadcasted_iota(I32, (Lp, Lp), 1)
    causal = si <= ti
    tri = jnp.where(causal, 1.0, 0.0).astype(BF16)
    bc = sum(_mm(tri, p) for p in _split3(lf))
    bc_t = bc.T
    gt_t = gt.T
    ones_blk = jnp.broadcast_to(onehot0, (Lp, LANES)).astype(BF16)

    for h in range(H):
        b_col = bc[:, H + h:H + h + 1]
        b_row = bc_t[H + h:H + h + 1, :]
        i_col = gt[:, h:h + 1]
        i_row = gt_t[h:h + 1, :]
        m_prev = m_ref[h][0:1, 0:1]
        qh = pad_rows(q_ref[:, h * HEAD_DIM:(h + 1) * HEAD_DIM])
        kh = pad_rows(k_ref[:, h * HEAD_DIM:(h + 1) * HEAD_DIM])
        vh = pad_rows(v_ref[:, h * ML_V_DIM:(h + 1) * ML_V_DIM])
        vext = jnp.concatenate([vh, ones_blk], axis=1)
        cext = cext_ref[h]

        log_d = jnp.where(causal, b_col - b_row + i_row, NEG_BIG)
        log_inter = b_col + m_prev
        m_row = jnp.maximum(log_inter, jnp.max(log_d, axis=1, keepdims=True))
        dmat = jnp.exp(log_d - m_row)
        s = lax.dot_general(qh, kh, (((1,), (1,)), ((), ())), preferred_element_type=F32) * dmat
        inter = jnp.exp(log_inter - m_row)
        num = _mm(s.astype(BF16), vext) + inter * _mm(qh, cext.astype(BF16))
        den = num[:, ML_V_DIM:ML_V_DIM + 1]
        hh = num[:L, :ML_V_DIM] / jnp.maximum(jnp.abs(den), jnp.exp(-m_row))[:L]
        hn = hh * lax.rsqrt(jnp.mean(hh * hh, axis=-1, keepdims=True) + EPS)
        cols = slice(h * ML_V_DIM, (h + 1) * ML_V_DIM)
        ogate = 1.0 / (1.0 + jnp.exp(-o_ref[:, cols]))
        out_ref[:, cols] = (ogate * (hn * g_ref[:, cols])).astype(BF16)

        b_last = b_col[Lp - 1:Lp, :]
        log_w = b_last - b_col + i_col
        m_new = jnp.maximum(b_last + m_prev, jnp.max(log_w, axis=0, keepdims=True))
        wk = jnp.exp(log_w - m_new)
        decay = jnp.exp(b_last + m_prev - m_new)
        upd = lax.dot_general(kh, (wk * vext.astype(F32)).astype(BF16), (((0,), (0,)), ((), ())),
                              preferred_element_type=F32)
        cnew = decay * cext + upd
        cext_ref[h] = cnew
        m_ref[h] = jnp.broadcast_to(m_new, (8, LANES))

    @pl.when(c == pl.num_programs(1) - 1)
    def _():
        for h in range(H):
            cf = cext_ref[h]
            c_out_ref[0, h] = cf[:, :ML_V_DIM]
            n_out_ref[0, h] = cf[:, ML_V_DIM:ML_V_DIM + 1]
            m_out_ref[0, h] = m_ref[h][0:1, 0:1]


def _mlstm(qm, km, vm, om, gt, C0, n0, m0, g_ml, B, S, L):
    nc = S // L
    Lp = max(L, LANES)
    H = ML_HEADS
    VW = ML_V_DIM + LANES
    row = lambda w: pl.BlockSpec((L, w), lambda b, c: (b * nc + c, 0))
    st = lambda *tail: pl.BlockSpec((1, H) + tail, lambda b, c: (b, 0, 0, 0))
    return pl.pallas_call(
        functools.partial(_mlstm_kernel, L=L, Lp=Lp),
        grid=(B, nc),
        in_specs=[row(H * HEAD_DIM), row(H * HEAD_DIM), row(H * ML_V_DIM), row(H * ML_V_DIM), row(LANES),
                  st(HEAD_DIM, ML_V_DIM), st(HEAD_DIM, 1), st(1, 1),
                  pl.BlockSpec((1, H * ML_V_DIM), lambda b, c: (0, 0))],
        out_specs=[row(H * ML_V_DIM), st(HEAD_DIM, ML_V_DIM), st(HEAD_DIM, 1), st(1, 1)],
        out_shape=[jax.ShapeDtypeStruct((B * S, H * ML_V_DIM), BF16),
                   jax.ShapeDtypeStruct((B, H, HEAD_DIM, ML_V_DIM), F32),
                   jax.ShapeDtypeStruct((B, H, HEAD_DIM, 1), F32),
                   jax.ShapeDtypeStruct((B, H, 1, 1), F32)],
        scratch_shapes=[pltpu.VMEM((H, HEAD_DIM, VW), F32), pltpu.VMEM((H, 8, LANES), F32)],
        compiler_params=_cparams("parallel", "arbitrary"),
        name="mlstm",
    )(qm, km, vm, om, gt, C0, n0.reshape(B, H, HEAD_DIM, 1), m0.reshape(B, H, 1, 1), g_ml.reshape(1, -1))


def _pack_halves(lo, hi):
    lo_bits = pltpu.bitcast(lo.astype(BF16).astype(F32), jnp.uint32) >> 16
    hi_bits = pltpu.bitcast(hi.astype(BF16).astype(F32), jnp.uint32) & jnp.uint32(0xFFFF0000)
    return lo_bits | hi_bits


def _unpack_halves(w):
    lo = pltpu.bitcast(w << 16, F32).astype(BF16)
    hi = pltpu.bitcast(w & jnp.uint32(0xFFFF0000), F32).astype(BF16)
    return lo, hi


def _outproj_router_kernel(x_ref, sb_ref, ml_ref, wos_ref, wom_ref, g_ref, wr_hi_ref, wr_lo_ref, br_ref, c0_ref,
                           x2_ref, xn_ref, eid_ref, gate_ref, rank_ref, cnt_ref, carry_ref):
    i = pl.program_id(0)

    @pl.when(i == 0)
    def _():
        carry_ref[...] = c0_ref[...]

    x2 = x_ref[...] + _mm(sb_ref[...], wos_ref[...]) + _mm(ml_ref[...], wom_ref[...])
    x2_ref[...] = x2
    xn = _rms(x2, g_ref[...])
    half = xn.shape[1] // 2
    xn_ref[...] = _pack_halves(xn[:, :half], xn[:, half:])
    hi, lo = _split2(xn)
    logits = _mm(hi, wr_hi_ref[...]) + _mm(lo, wr_hi_ref[...]) + _mm(hi, wr_lo_ref[...]) + br_ref[...]
    tm = logits.shape[0]
    lane = lax.broadcasted_iota(I32, (tm, LANES), 1)
    vals, ids = [], []
    cnt = jnp.zeros((tm, LANES), F32)
    for _ in range(TOP_K):
        mx = jnp.max(logits, axis=-1, keepdims=True)
        idx = jnp.min(jnp.where(logits == mx, lane, LANES), axis=-1, keepdims=True)
        sel = lane == idx
        vals.append(mx)
        ids.append(idx)
        logits = jnp.where(sel, -jnp.inf, logits)
        cnt = cnt + jnp.where(sel, 1.0, 0.0)
    es = [jnp.exp(v - vals[0]) for v in vals]
    inv = 1.0 / sum(es)
    ti = lax.broadcasted_iota(I32, (tm, tm), 0)
    si = lax.broadcasted_iota(I32, (tm, tm), 1)
    before = _mm(jnp.where(si < ti, 1.0, 0.0).astype(BF16), cnt.astype(BF16)) + carry_ref[...]
    eid_o = jnp.zeros((tm, LANES), I32)
    gate_o = jnp.zeros((tm, LANES), F32)
    rank_o = jnp.zeros((tm, LANES), I32)
    for k in range(TOP_K):
        rk = jnp.sum(jnp.where(lane == ids[k], before, 0.0), axis=-1, keepdims=True)
        eid_o = jnp.where(lane == k, ids[k], eid_o)
        gate_o = jnp.where(lane == k, es[k] * inv, gate_o)
        rank_o = jnp.where(lane == k, rk.astype(I32), rank_o)
    eid_ref[...] = eid_o
    gate_ref[...] = gate_o
    rank_ref[...] = rank_o
    carry_ref[...] += jnp.sum(cnt, axis=0, keepdims=True)
    cnt_ref[...] = carry_ref[...]


def _outproj_router(x2d, sb_o, ml_o, w_out, norm_g, w_router, b_router, counts_in, tm):
    T, D = x2d.shape
    W = sb_o.shape[1]
    assert w_out.shape[0] == 2 * W
    w_bf = w_out.astype(BF16)
    half_w = lambda blk: pl.BlockSpec((W, D), lambda i: (blk, 0), pipeline_mode=pl.Buffered(1))
    wr = jnp.zeros((D, LANES), F32).at[:, :N_EXPERTS].set(w_router)
    wr_hi = wr.astype(BF16)
    wr_lo = (wr - wr_hi.astype(F32)).astype(BF16)
    br = jnp.full((1, LANES), NEG_BIG, F32).at[0, :N_EXPERTS].set(b_router)
    row = lambda w: pl.BlockSpec((tm, w), lambda i: (i, 0))
    return pl.pallas_call(
        _outproj_router_kernel,
        grid=(T // tm,),
        in_specs=[row(D), row(W), row(W), half_w(0), half_w(1),
                  _resident((1, D)), _resident(wr_hi.shape), _resident(wr_lo.shape), _resident(br.shape),
                  _resident((1, LANES))],
        out_specs=[row(D), row(D // 2), row(LANES), row(LANES), row(LANES),
                   pl.BlockSpec((1, LANES), lambda i: (0, 0))],
        out_shape=[jax.ShapeDtypeStruct((T, D), F32), jax.ShapeDtypeStruct((T, D // 2), jnp.uint32),
                   jax.ShapeDtypeStruct((T, LANES), I32), jax.ShapeDtypeStruct((T, LANES), F32),
                   jax.ShapeDtypeStruct((T, LANES), I32), jax.ShapeDtypeStruct((1, LANES), F32)],
        scratch_shapes=[pltpu.VMEM((1, LANES), F32)],
        compiler_params=_cparams("arbitrary"),
        name="outproj_router",
    )(x2d, sb_o, ml_o, w_bf, w_bf, norm_g.reshape(1, D), wr_hi, wr_lo, br, counts_in)


ROW_BLOCK = 128
IT_TILE, IT_COL, IT_EXPERT, IT_FIRST, IT_BLOCKS, IT_NEXT_EXPERT, IT_NEXT_COL = range(7)


def _segment_weights(it_ref, w, copies, cast):
    @pl.when(it_ref[IT_FIRST, w] == 1)
    def _():
        @pl.when(w == 0)
        def _():
            for c in copies(it_ref[IT_EXPERT, w], it_ref[IT_COL, w]):
                c.start()

        for c in copies(it_ref[IT_EXPERT, w], it_ref[IT_COL, w]):
            c.wait()
        cast()

        @pl.when(it_ref[IT_NEXT_EXPERT, w] >= 0)
        def _():
            for c in copies(it_ref[IT_NEXT_EXPERT, w], it_ref[IT_NEXT_COL, w]):
                c.start()


def _for_filled_rows(blocks, tm, compute):
    for lvl in range(1, tm // ROW_BLOCK + 1):
        @pl.when(blocks == lvl)
        def _():
            compute(lvl * ROW_BLOCK)


def _gate_up_kernel(it_ref, n_ref, x_ref, w_hbm, bg_ref, bl_ref, act_ref, stage, wg_s, wl_s, x_s, sems, *, nj):
    w = pl.program_id(0)
    tm, half = x_ref.shape
    tn = act_ref.shape[1]

    def copies(e, j):
        return [pltpu.make_async_copy(w_hbm.at[e, :, pl.ds(pl.multiple_of((h * nj + j) * tn, tn), tn)],
                                      stage.at[h], sems.at[h]) for h in range(2)]

    def cast():
        wg_s[...] = stage[0].astype(BF16)
        wl_s[...] = stage[1].astype(BF16)

    @pl.when(w < n_ref[0])
    def _():
        _segment_weights(it_ref, w, copies, cast)

        def compute(rows):
            lo, hi = _unpack_halves(x_ref[:rows, :])
            x_s[:rows, :half] = lo
            x_s[:rows, half:] = hi
            x = x_s[:rows, :]
            glu = jnp.minimum(_mm(x, wg_s[...]) + bg_ref[0], SWIGLU_LIMIT)
            lin = jnp.clip(_mm(x, wl_s[...]) + bl_ref[0], -SWIGLU_LIMIT, SWIGLU_LIMIT)
            act = glu * (1.0 / (1.0 + jnp.exp(-SWIGLU_ALPHA * glu))) * (lin + 1.0)
            act_ref[:rows, :] = act.astype(BF16)
            if rows < tm:
                act_ref[rows:, :] = jnp.zeros((tm - rows, tn), BF16)

        _for_filled_rows(it_ref[IT_BLOCKS, w], tm, compute)

    @pl.when(w >= n_ref[0])
    def _():
        act_ref[...] = jnp.zeros_like(act_ref)


def _down_kernel(it_ref, n_ref, a_ref, w_hbm, b_ref, o_ref, stage, w_s, sem):
    w = pl.program_id(0)
    tm = a_ref.shape[0]
    tn = w_s.shape[1]

    def copies(e, j):
        return [pltpu.make_async_copy(w_hbm.at[e, :, pl.ds(pl.multiple_of(j * tn, tn), tn)], stage, sem)]

    def cast():
        w_s[...] = stage[...].astype(BF16)

    @pl.when(w < n_ref[0])
    def _():
        _segment_weights(it_ref, w, copies, cast)

        def compute(rows):
            out = _mm(a_ref[:rows, :], w_s[...]) + b_ref[0]
            o_ref[:rows, :] = _pack_halves(out[:, :tn // 2], out[:, tn // 2:])
            if rows < tm:
                o_ref[rows:, :] = jnp.zeros((tm - rows, tn // 2), jnp.uint32)

        _for_filled_rows(it_ref[IT_BLOCKS, w], tm, compute)

    @pl.when(w >= n_ref[0])
    def _():
        o_ref[...] = jnp.zeros_like(o_ref)


def _work_list(counts, tm, nj, n_tiles_max):
    tpe = (counts + tm - 1) // tm
    tile_start = jnp.cumsum(tpe) - tpe
    item_end = jnp.cumsum(tpe * nj)
    item_start = item_end - tpe * nj
    n_items = item_end[-1:]
    w = jnp.arange(n_tiles_max * nj, dtype=I32)
    wc = jnp.minimum(w, n_items[0] - 1)
    e = jnp.minimum(jnp.sum((wc[:, None] >= item_end[None, :]).astype(I32), axis=1), N_EXPERTS - 1)
    local = wc - item_start[e]
    t = jnp.maximum(tpe[e], 1)
    j = local // t
    il = local - j * t
    real = w < n_items[0]
    first = ((il == 0) & real).astype(I32)
    tail = jnp.maximum(w - n_items[0], 0)
    tile = jnp.where(real, tile_start[e] + il, jnp.sum(tpe) + tail // nj)
    j = jnp.where(real, j, tail % nj)
    blocks = (jnp.clip(counts[e] - il * tm, 1, tm) + ROW_BLOCK - 1) // ROW_BLOCK
    ids = jnp.where(tpe > 0, jnp.arange(N_EXPERTS, dtype=I32), N_EXPERTS)
    later = jnp.concatenate([lax.cummin(ids[::-1])[::-1][1:], jnp.full((1,), N_EXPERTS, I32)])
    next_nonempty = jnp.where(later < N_EXPERTS, later, -1)
    last_col = j + 1 >= nj
    next_e = jnp.where(last_col, next_nonempty[e], e)
    next_j = jnp.where(last_col, 0, j + 1)
    table = jnp.stack([tile, j, e, first, blocks, next_e, next_j]).astype(I32)
    return table, n_items.astype(I32)


def _expert_ffn(xs, counts, w_gu, b_gu, w_dn, b_dn, tm, tn):
    n_rows, half = xs.shape
    D = 2 * half
    F = w_dn.shape[1]
    n_tiles = n_rows // tm
    nj = F // tn
    row_tile = lambda w, it, n: (it[IT_TILE, w], 0)
    out_tile = lambda w, it, n: (it[IT_TILE, w], it[IT_COL, w])
    act = pl.pallas_call(
        functools.partial(_gate_up_kernel, nj=nj),
        grid_spec=pltpu.PrefetchScalarGridSpec(
            num_scalar_prefetch=2,
            grid=(n_tiles * nj,),
            in_specs=[pl.BlockSpec((tm, half), row_tile),
                      pl.BlockSpec(memory_space=pl.ANY),
                      pl.BlockSpec((1, 1, tn), lambda w, it, n: (it[IT_EXPERT, w], 0, it[IT_COL, w])),
                      pl.BlockSpec((1, 1, tn), lambda w, it, n: (it[IT_EXPERT, w], 0, nj + it[IT_COL, w]))],
            out_specs=pl.BlockSpec((tm, tn), out_tile),
            scratch_shapes=[pltpu.VMEM((2, D, tn), F32), pltpu.VMEM((D, tn), BF16), pltpu.VMEM((D, tn), BF16),
                            pltpu.VMEM((tm, D), BF16), pltpu.SemaphoreType.DMA((2,))]),
        out_shape=jax.ShapeDtypeStruct((n_rows, F), BF16),
        compiler_params=_cparams("arbitrary"),
        name="expert_gate_up",
    )(*_work_list(counts, tm, nj, n_tiles), xs, w_gu, b_gu, b_gu)

    njd = D // tn
    return pl.pallas_call(
        _down_kernel,
        grid_spec=pltpu.PrefetchScalarGridSpec(
            num_scalar_prefetch=2,
            grid=(n_tiles * njd,),
            in_specs=[pl.BlockSpec((tm, F), row_tile),
                      pl.BlockSpec(memory_space=pl.ANY),
                      pl.BlockSpec((1, 1, tn), lambda w, it, n: (it[IT_EXPERT, w], 0, it[IT_COL, w]))],
            out_specs=pl.BlockSpec((tm, tn // 2), out_tile),
            scratch_shapes=[pltpu.VMEM((F, tn), F32), pltpu.VMEM((F, tn), BF16), pltpu.SemaphoreType.DMA]),
        out_shape=jax.ShapeDtypeStruct((n_rows, D // 2), jnp.uint32),
        compiler_params=_cparams("arbitrary"),
        name="expert_down",
    )(*_work_list(counts, tm, njd, n_tiles), act, w_dn, b_dn)


def _scatter_kernel(zero_ref, dest_ref, xa_ref, xb_ref, xs_ref, zbuf, sem, zsem, *, tiles_a, slot_tile):
    i = pl.program_id(0)
    tm = xa_ref.shape[0]

    @pl.when(i == 0)
    def _():
        zbuf[...] = jnp.zeros_like(zbuf)

        def tile_copy(z):
            start = pl.multiple_of(zero_ref[1 + z] * slot_tile, slot_tile)
            return pltpu.make_async_copy(zbuf, xs_ref.at[pl.ds(start, slot_tile), :], zsem)

        def start(z, carry):
            tile_copy(z).start()
            return carry

        def wait(z, carry):
            tile_copy(z).wait()
            return carry

        lax.fori_loop(0, zero_ref[0], start, 0)
        lax.fori_loop(0, zero_ref[0], wait, 0)

    def scatter_from(x_ref):
        def issue(r, carry):
            for k in range(TOP_K):
                pltpu.make_async_copy(x_ref.at[pl.ds(r, 1), :], xs_ref.at[pl.ds(dest_ref[r * TOP_K + k], 1), :],
                                      sem).start()
            return carry

        lax.fori_loop(0, tm, issue, 0, unroll=8)
        for _ in range(TOP_K):
            pltpu.make_async_copy(x_ref, xs_ref.at[pl.ds(0, tm), :], sem).wait()

    @pl.when(i < tiles_a)
    def _():
        scatter_from(xa_ref)

    @pl.when(i >= tiles_a)
    def _():
        scatter_from(xb_ref)


def _scatter_rows(xa, xb, dest, zero_list, n_rows, tm, slot_tile):
    Ta, half = xa.shape
    Tb = xb.shape[0]
    tiles_a, tiles_b = Ta // tm, Tb // tm
    return pl.pallas_call(
        functools.partial(_scatter_kernel, tiles_a=tiles_a, slot_tile=slot_tile),
        grid_spec=pltpu.PrefetchScalarGridSpec(
            num_scalar_prefetch=1,
            grid=(tiles_a + tiles_b,),
            in_specs=[pl.BlockSpec((tm * TOP_K,), lambda i, z: (i,), memory_space=pltpu.SMEM),
                      pl.BlockSpec((tm, half), lambda i, z: (jnp.minimum(i, tiles_a - 1), 0)),
                      pl.BlockSpec((tm, half), lambda i, z: (jnp.maximum(i - tiles_a, 0), 0))],
            out_specs=pl.BlockSpec(memory_space=pl.ANY),
            scratch_shapes=[pltpu.VMEM((slot_tile, half), jnp.uint32), pltpu.SemaphoreType.DMA,
                            pltpu.SemaphoreType.DMA]),
        out_shape=jax.ShapeDtypeStruct((n_rows, half), jnp.uint32),
        compiler_params=_cparams("arbitrary"),
        name="scatter_rows",
    )(zero_list, dest.reshape((Ta + Tb) * TOP_K), xa, xb)


COMBINE_PARTS = 8
COMBINE_AHEAD = 2


def _combine_kernel(dest_ref, x2_ref, gate_ref, g_ref, out_ref, y_ref, buf, sems, *, col_tile):
    tm, D = x2_ref.shape
    hw = col_tile // 2

    def unpack(w):
        lo = pltpu.bitcast(w << 16, F32)
        hi = pltpu.bitcast(w & jnp.uint32(0xFFFF0000), F32)
        parts = []
        for j in range(D // col_tile):
            parts += [lo[:, j * hw:(j + 1) * hw], hi[:, j * hw:(j + 1) * hw]]
        return jnp.concatenate(parts, axis=1)

    rows_per_part = tm // COMBINE_PARTS

    def issue(part):
        for r in range(part * rows_per_part, (part + 1) * rows_per_part):
            for k in range(TOP_K):
                pltpu.make_async_copy(out_ref.at[pl.ds(dest_ref[r * TOP_K + k], 1), :], buf.at[k, pl.ds(r, 1), :],
                                      sems.at[part]).start()

    for part in range(min(COMBINE_AHEAD, COMBINE_PARTS)):
        issue(part)
    for part in range(COMBINE_PARTS):
        rows = pl.ds(part * rows_per_part, rows_per_part)
        for k in range(TOP_K):
            pltpu.make_async_copy(out_ref.at[pl.ds(0, rows_per_part), :], buf.at[k, rows, :], sems.at[part]).wait()
        if part + COMBINE_AHEAD < COMBINE_PARTS:
            issue(part + COMBINE_AHEAD)
        gate = gate_ref[rows, :]
        acc = x2_ref[rows, :]
        for k in range(TOP_K):
            acc = acc + gate[:, k:k + 1] * unpack(buf[k, rows, :])
        y_ref[rows, :] = _rms(acc, g_ref[...])


def _combine(x2, gate, dest, out, final_g, tm, col_tile):
    T, D = x2.shape
    row = lambda w: pl.BlockSpec((tm, w), lambda i: (i, 0))
    return pl.pallas_call(
        functools.partial(_combine_kernel, col_tile=col_tile),
        grid=(T // tm,),
        in_specs=[pl.BlockSpec((tm * TOP_K,), lambda i: (i,), memory_space=pltpu.SMEM),
                  row(D), row(LANES), _resident((1, D)), pl.BlockSpec(memory_space=pl.ANY)],
        out_specs=row(D),
        out_shape=jax.ShapeDtypeStruct((T, D), F32),
        scratch_shapes=[pltpu.VMEM((TOP_K, tm, D // 2), jnp.uint32), pltpu.SemaphoreType.DMA((COMBINE_PARTS,))],
        compiler_params=_cparams("arbitrary"),
        name="combine_norm",
    )(dest.reshape(T * TOP_K), x2, gate, final_g.reshape(1, D), out)


def _tile(n, pref):
    return pref if n % pref == 0 else n


def _mixers(x, past, lw):
    (norm_mix_g, w_in, b_igate, b_fgate, g_sb_out, g_ml_out) = lw
    B, S, D = x.shape
    T = B * S
    x2d = x.reshape(T, D)
    q, kf, kb, vf, vb, qm, km, vm, om, gt = _in_projection(x2d, norm_mix_g, w_in, b_igate, b_fgate, _tile(T, 256))
    if past is None:
        sb_o = _sb_prompt(q, kb, vb, g_sb_out, B, S, R=min(8, S // KEY_BLOCK))
        C0 = jnp.zeros((B, ML_HEADS, HEAD_DIM, ML_V_DIM), F32)
        n0 = jnp.zeros((B, ML_HEADS, HEAD_DIM), F32)
        m0 = jnp.zeros((B, ML_HEADS), F32)
        L = _tile(S, 256)
    else:
        cache_k, cache_v, C0, n0, m0 = past
        sb_o = _sb_sample(q, kb, vb, cache_k, cache_v, g_sb_out, B, S, cache_k.shape[1])
        L = S
    ml_o, C, n, m = _mlstm(qm, km, vm, om, gt, C0, n0, m0, g_ml_out, B, S, L)
    state = (kf.reshape(B, S, SB_HEADS, HEAD_DIM), vf.reshape(B, S, SB_HEADS, HEAD_DIM),
             C, n.reshape(B, ML_HEADS, HEAD_DIM), m.reshape(B, ML_HEADS))
    return (x2d, sb_o, ml_o), state


MOE_ROW_TILE = 512
MOE_COL_TILE = 1024
COMBINE_TILE = 256


def kernel(x_prompt, x_sample, cache_k, cache_v, state_C, state_n, state_m, norm_mix_g, w_in, b_igate, b_fgate,
           g_sb_out, g_ml_out, w_out, norm_ffn_g, w_router, b_router, w_gate_up, b_gate_up, w_down, b_down,
           final_norm_g):
    assert w_in.shape[0] == 1, "single-layer trunk"
    lw = (norm_mix_g[0], w_in[0], b_igate[0], b_fgate[0], g_sb_out[0], g_ml_out[0])
    E = w_gate_up.shape[1]
    D = x_prompt.shape[-1]
    groups = [_mixers(x_prompt, None, lw),
              _mixers(x_sample, (cache_k[0], cache_v[0], state_C[0], state_n[0], state_m[0]), lw)]

    counts = jnp.zeros((1, LANES), F32)
    routed = []
    for (x2d, sb_o, ml_o), _ in groups:
        x2, xn2, eid, gate, rank, counts = _outproj_router(x2d, sb_o, ml_o, w_out[0], norm_ffn_g[0], w_router[0],
                                                           b_router[0], counts, _tile(x2d.shape[0], 512))
        routed.append((x2, xn2, eid, gate, rank))
    tm = MOE_ROW_TILE
    cnt = counts[0, :N_EXPERTS].astype(I32)
    padded = (cnt + tm - 1) // tm * tm
    pstart = jnp.cumsum(padded) - padded
    n_assign = sum(r[0].shape[0] for r in routed) * TOP_K
    n_rows = (-(-n_assign // tm) + N_EXPERTS) * tm
    dests = [pstart[eid[:, :TOP_K]] + rank[:, :TOP_K] for (_, _, eid, _, rank) in routed]

    tpe = padded // tm
    n_tiles = n_rows // tm
    tile_ids = jnp.arange(n_tiles, dtype=I32)
    last_of_expert = jnp.any((tile_ids[:, None] == (jnp.cumsum(tpe) - 1)[None, :]) & (tpe[None, :] > 0), axis=1)
    needs_zero = last_of_expert | (tile_ids >= jnp.sum(tpe))
    zero_list = jnp.concatenate([jnp.sum(needs_zero.astype(I32))[None],
                                 jnp.nonzero(needs_zero, size=n_tiles, fill_value=0)[0].astype(I32)])
    xs = _scatter_rows(routed[0][1], routed[1][1], jnp.concatenate(dests, axis=0), zero_list, n_rows,
                       _tile(routed[1][1].shape[0], 512), tm)
    out = _expert_ffn(xs, cnt, w_gate_up[0], b_gate_up[0].reshape(E, 1, -1), w_down[0],
                      b_down[0].reshape(E, 1, -1), tm, MOE_COL_TILE)
    ys = [_combine(x2, gate, dest, out, final_norm_g, _tile(x2.shape[0], COMBINE_TILE), MOE_COL_TILE)
          for (x2, _, _, gate, _), dest in zip(routed, dests)]

    (kp, vp, Cp, np_, mp), (ks, vs, Cs, ns, ms) = groups[0][1], groups[1][1]
    return (ys[0].reshape(x_prompt.shape), ys[1].reshape(x_sample.shape),
            kp[None], vp[None], Cp[None], np_[None], mp[None], ks[None], vs[None], Cs[None], ns[None], ms[None])
```

```python
import functools

import jax
import jax.numpy as jnp
from jax import lax
from jax.experimental import pallas as pl
from jax.experimental.pallas import tpu as pltpu

F32 = jnp.float32
BF16 = jnp.bfloat16
I32 = jnp.int32

EPS = 1e-6
SB_HEADS = 8
HEAD_DIM = 128
ML_HEADS = 4
ML_V_DIM = 256
N_EXPERTS = 32
TOP_K = 4
SWIGLU_ALPHA = 1.702
SWIGLU_LIMIT = 7.0
LANES = 128
KEY_BLOCK = 128
NEG_BIG = -1e30
EXP_ZERO_BELOW = -105.0
VMEM_LIMIT = 56 * 1024 * 1024


def _cparams(*sem, vmem_limit=VMEM_LIMIT):
    return pltpu.CompilerParams(dimension_semantics=sem, vmem_limit_bytes=vmem_limit)


def _resident(shape):
    nd = len(shape)
    return pl.BlockSpec(shape, lambda *_: (0,) * nd, pipeline_mode=pl.Buffered(1))


def _rms(x, g):
    return x * lax.rsqrt(jnp.mean(x * x, axis=-1, keepdims=True) + EPS) * g


def _log_sigmoid(z):
    return jnp.minimum(z, 0.0) - jnp.log(1.0 + jnp.exp(-jnp.abs(z)))


def _split2(x):
    hi = x.astype(BF16)
    lo = (x - hi.astype(F32)).astype(BF16)
    return hi, lo


def _split3(x):
    h1 = x.astype(BF16)
    r = x - h1.astype(F32)
    h2 = r.astype(BF16)
    h3 = (r - h2.astype(F32)).astype(BF16)
    return h1, h2, h3


def _mm(a, b):
    return jnp.dot(a, b, preferred_element_type=F32)


def _inproj_kernel(x_ref, g_ref, wq_ref, wk_ref, wv_ref, wqm_ref, wkm_ref, wvm_ref, wom_ref,
                   wg_ref, bg_ref,
                   q_ref, kf_ref, kb_ref, vf_ref, vb_ref, qm_ref, km_ref, vm_ref, om_ref, gt_ref):
    xn = _rms(x_ref[...], g_ref[...]).astype(BF16)
    q_ref[...] = (_mm(xn, wq_ref[...]) * (HEAD_DIM ** -0.5)).astype(BF16)
    k = _mm(xn, wk_ref[...])
    kf_ref[...] = k
    kb_ref[...] = k.astype(BF16)
    v = _mm(xn, wv_ref[...])
    vf_ref[...] = v
    vb_ref[...] = v.astype(BF16)
    qm_ref[...] = _mm(xn, wqm_ref[...]).astype(BF16)
    km_ref[...] = (_mm(xn, wkm_ref[...]) * (HEAD_DIM ** -0.5)).astype(BF16)
    vm_ref[...] = _mm(xn, wvm_ref[...]).astype(BF16)
    om_ref[...] = _mm(xn, wom_ref[...])
    gpre = _mm(xn, wg_ref[...]) + bg_ref[...]
    lane = lax.broadcasted_iota(I32, gpre.shape, 1)
    is_f = (lane >= ML_HEADS) & (lane < 2 * ML_HEADS)
    gt_ref[...] = jnp.where(is_f, _log_sigmoid(gpre), gpre)


def _in_projection(x2d, norm_g, w_in, b_igate, b_fgate, tm):
    T, D = x2d.shape
    sbw = SB_HEADS * HEAD_DIM
    mqk = ML_HEADS * HEAD_DIM
    mlw = ML_HEADS * ML_V_DIM
    w_bf = w_in.astype(BF16)
    o = 0
    w_specs = []
    for width in (sbw, sbw, sbw, mqk, mqk, mlw, mlw):
        assert o % width == 0
        w_specs.append(pl.BlockSpec((D, width), functools.partial(lambda blk, i: (0, blk), o // width),
                                    pipeline_mode=pl.Buffered(1)))
        o += width
    wg = jnp.zeros((D, LANES), F32).at[:, :2 * ML_HEADS].set(w_in[:, o:o + 2 * ML_HEADS]).astype(BF16)
    bg = jnp.zeros((1, LANES), F32).at[0, :ML_HEADS].set(b_igate).at[0, ML_HEADS:2 * ML_HEADS].set(b_fgate)
    row = lambda w: pl.BlockSpec((tm, w), lambda i: (i, 0))
    out_widths = (sbw, sbw, sbw, sbw, sbw, mqk, mqk, mlw, mlw, LANES)
    out_dtypes = (BF16, F32, BF16, F32, BF16, BF16, BF16, BF16, F32, F32)
    return pl.pallas_call(
        _inproj_kernel,
        grid=(T // tm,),
        in_specs=[row(D), _resident((1, D))] + w_specs + [_resident(wg.shape), _resident(bg.shape)],
        out_specs=[row(w) for w in out_widths],
        out_shape=[jax.ShapeDtypeStruct((T, w), dt) for w, dt in zip(out_widths, out_dtypes)],
        compiler_params=_cparams("parallel"),
        name="in_projection",
    )(x2d, norm_g.reshape(1, D), *([w_bf] * len(w_specs)), wg, bg)


def _suffix_matrix():
    j = lax.broadcasted_iota(I32, (KEY_BLOCK, 2 * KEY_BLOCK), 0)
    c = lax.broadcasted_iota(I32, (KEY_BLOCK, 2 * KEY_BLOCK), 1)
    return jnp.where((c >= KEY_BLOCK) | (j > c), 1.0, 0.0).astype(BF16)


def _sb_step(q, k, v, carry, umat, mask):
    R, bq, _ = q.shape
    z = jnp.einsum("rqd,rkd->rqk", q, k, preferred_element_type=F32)
    lp = jnp.log(1.0 + jnp.exp(-jnp.abs(z)))
    log_beta = jnp.minimum(z, 0.0) - lp
    log_stay = log_beta - z
    if mask is not None:
        log_stay = jnp.where(mask, log_stay, 0.0)
    hi, lo = _split2(log_stay)
    st = _mm(hi.reshape(R * bq, KEY_BLOCK), umat) + _mm(lo.reshape(R * bq, KEY_BLOCK), umat)
    st = st.reshape(R, bq, 2 * KEY_BLOCK)
    w = jnp.exp(log_beta + st[:, :, :KEY_BLOCK] + carry)
    if mask is not None:
        w = jnp.where(mask, w, 0.0)
    pv = jnp.einsum("rqk,rkd->rqd", w.astype(BF16), v, preferred_element_type=F32)
    return pv, carry + st[:, :, KEY_BLOCK:]


def _sb_prompt_kernel(q_ref, k_ref, v_ref, g_ref, o_ref, acc_ref, carry_ref, *, R):
    qi = pl.program_id(2)
    blk0 = qi * R
    q = q_ref[...].reshape(R, KEY_BLOCK, HEAD_DIM)
    umat = _suffix_matrix()
    shape3 = (R, KEY_BLOCK, KEY_BLOCK)
    t_io = lax.broadcasted_iota(I32, shape3, 1)
    s_io = lax.broadcasted_iota(I32, shape3, 2)
    r_io = lax.broadcasted_iota(I32, shape3, 0)

    def load(ref, d):
        return jnp.stack([ref[pl.ds(pl.multiple_of(jnp.maximum(blk0 + r - d, 0) * KEY_BLOCK, KEY_BLOCK),
                                    KEY_BLOCK), :] for r in range(R)])

    def penalty(d_next):
        return jnp.where(r_io < d_next - blk0, NEG_BIG, 0.0)

    pv, carry = _sb_step(q, load(k_ref, 0), load(v_ref, 0), jnp.zeros(shape3, F32), umat, s_io < t_io)
    acc_ref[...] = pv
    carry = carry + penalty(1)
    carry_ref[...] = carry

    def cond(state):
        d, mx = state
        return (d < blk0 + R) & (mx > EXP_ZERO_BELOW)

    def body(state):
        d, _ = state
        pv, carry = _sb_step(q, load(k_ref, d), load(v_ref, d), carry_ref[...], umat, None)
        acc_ref[...] += pv
        carry = carry + penalty(d + 1)
        carry_ref[...] = carry
        return d + 1, jnp.max(carry)

    lax.while_loop(cond, body, (jnp.int32(1), jnp.max(carry)))
    a = acc_ref[...]
    out = a * lax.rsqrt(jnp.mean(a * a, axis=-1, keepdims=True) + EPS) * g_ref[...]
    o_ref[...] = out.reshape(R * KEY_BLOCK, HEAD_DIM).astype(BF16)


def _sb_prompt(q, k, v, g_sb, B, S, R):
    tq = R * KEY_BLOCK
    nq = S // tq
    return pl.pallas_call(
        functools.partial(_sb_prompt_kernel, R=R),
        grid=(B, SB_HEADS, nq),
        in_specs=[pl.BlockSpec((tq, HEAD_DIM), lambda b, h, i: (b * nq + i, h)),
                  pl.BlockSpec((S, HEAD_DIM), lambda b, h, i: (b, h)),
                  pl.BlockSpec((S, HEAD_DIM), lambda b, h, i: (b, h)),
                  pl.BlockSpec((1, HEAD_DIM), lambda b, h, i: (0, h))],
        out_specs=pl.BlockSpec((tq, HEAD_DIM), lambda b, h, i: (b * nq + i, h)),
        out_shape=jax.ShapeDtypeStruct((B * S, SB_HEADS * HEAD_DIM), BF16),
        scratch_shapes=[pltpu.VMEM((R, KEY_BLOCK, HEAD_DIM), F32),
                        pltpu.VMEM((R, KEY_BLOCK, KEY_BLOCK), F32)],
        compiler_params=_cparams("parallel", "parallel", "parallel"),
        name="sb_prompt",
    )(q, k, v, g_sb.reshape(1, -1))


def _sb_sample_kernel(q_ref, kn_ref, vn_ref, ck_ref, cv_ref, g_ref, o_ref, acc_ref, carry_ref, *, S, P):
    H = SB_HEADS
    hs = lambda h: slice(h * HEAD_DIM, (h + 1) * HEAD_DIM)
    umat = _suffix_matrix()
    q = jnp.stack([q_ref[:, hs(h)] for h in range(H)])
    pad = jnp.zeros((KEY_BLOCK - S, HEAD_DIM), BF16)

    def new_keys(ref):
        return jnp.stack([jnp.concatenate([ref[:, hs(h)], pad], axis=0) for h in range(H)])

    def past_keys(ref, j):
        start = pl.multiple_of(j * KEY_BLOCK, KEY_BLOCK)
        return jnp.stack([ref[pl.ds(start, KEY_BLOCK), hs(h)].astype(BF16) for h in range(H)])

    shape3 = (H, S, KEY_BLOCK)
    t_io = lax.broadcasted_iota(I32, shape3, 1)
    s_io = lax.broadcasted_iota(I32, shape3, 2)
    pv, carry = _sb_step(q, new_keys(kn_ref), new_keys(vn_ref), jnp.zeros(shape3, F32), umat, s_io < t_io)
    acc_ref[...] = pv
    carry_ref[...] = carry

    def cond(state):
        j, mx = state
        return (j >= 0) & (mx > EXP_ZERO_BELOW)

    def body(state):
        j, _ = state
        pv, carry = _sb_step(q, past_keys(ck_ref, j), past_keys(cv_ref, j), carry_ref[...], umat, None)
        acc_ref[...] += pv
        carry_ref[...] = carry
        return j - 1, jnp.max(carry)

    lax.while_loop(cond, body, (jnp.int32(P // KEY_BLOCK - 1), jnp.max(carry)))
    a = acc_ref[...]
    a = a * lax.rsqrt(jnp.mean(a * a, axis=-1, keepdims=True) + EPS)
    for h in range(H):
        o_ref[:, hs(h)] = (a[h] * g_ref[:, hs(h)]).astype(BF16)


def _sb_sample(q, kn, vn, cache_k, cache_v, g_sb, B, S, P):
    W = SB_HEADS * HEAD_DIM
    row = pl.BlockSpec((S, W), lambda b: (b, 0))
    past = pl.BlockSpec((P, W), lambda b: (b, 0))
    return pl.pallas_call(
        functools.partial(_sb_sample_kernel, S=S, P=P),
        grid=(B,),
        in_specs=[row, row, row, past, past, pl.BlockSpec((1, W), lambda b: (0, 0))],
        out_specs=row,
        out_shape=jax.ShapeDtypeStruct((B * S, W), BF16),
        scratch_shapes=[pltpu.VMEM((SB_HEADS, S, HEAD_DIM), F32),
                        pltpu.VMEM((SB_HEADS, S, KEY_BLOCK), F32)],
        compiler_params=_cparams("parallel"),
        name="sb_sample",
    )(q, kn, vn, cache_k.reshape(B * P, W), cache_v.reshape(B * P, W), g_sb.reshape(1, W))


def _mlstm_kernel(q_ref, k_ref, v_ref, o_ref, gt_ref, c0_ref, n0_ref, m0_ref, g_ref,
                  out_ref, c_out_ref, n_out_ref, m_out_ref, cext_ref, m_ref, *, L, Lp):
    c = pl.program_id(1)
    H = ML_HEADS
    VW = ML_V_DIM + LANES
    lane_row = lax.broadcasted_iota(I32, (1, LANES), 1)
    onehot0 = jnp.where(lane_row == 0, 1.0, 0.0)

    @pl.when(c == 0)
    def _():
        for h in range(H):
            cext_ref[h] = jnp.concatenate([c0_ref[0, h], n0_ref[0, h] * onehot0], axis=1)
            m_ref[h] = jnp.broadcast_to(m0_ref[0, h], (8, LANES))

    def pad_rows(a, fill=0.0):
        if Lp == L:
            return a
        return jnp.concatenate([a, jnp.full((Lp - L, a.shape[1]), fill, a.dtype)], axis=0)

    gt = gt_ref[...]
    lane = lax.broadcasted_iota(I32, (Lp, LANES), 1)
    if Lp != L:
        gt = jnp.concatenate([gt, jnp.broadcast_to(jnp.where(lane_row < H, NEG_BIG, 0.0), (Lp - L, LANES))], axis=0)
    lf = jnp.where((lane >= H) & (lane < 2 * H), gt, 0.0)
    ti = lax.broadcasted_iota(I32, (Lp, Lp), 0)
    si = lax.broadcasted_iota(I32, (Lp, Lp), 1)
    causal = si <= ti
    tri = jnp.where(causal, 1.0, 0.0).astype(BF16)
    bc = sum(_mm(tri, p) for p in _split3(lf))
    bc_t = bc.T
    gt_t = gt.T
    ones_blk = jnp.broadcast_to(onehot0, (Lp, LANES)).astype(BF16)

    for h in range(H):
        b_col = bc[:, H + h:H + h + 1]
        b_row = bc_t[H + h:H + h + 1, :]
        i_col = gt[:, h:h + 1]
        i_row = gt_t[h:h + 1, :]
        m_prev = m_ref[h][0:1, 0:1]
        qh = pad_rows(q_ref[:, h * HEAD_DIM:(h + 1) * HEAD_DIM])
        kh = pad_rows(k_ref[:, h * HEAD_DIM:(h + 1) * HEAD_DIM])
        vh = pad_rows(v_ref[:, h * ML_V_DIM:(h + 1) * ML_V_DIM])
        vext = jnp.concatenate([vh, ones_blk], axis=1)
        cext = cext_ref[h]

        log_d = jnp.where(causal, b_col - b_row + i_row, NEG_BIG)
        log_inter = b_col + m_prev
        m_row = jnp.maximum(log_inter, jnp.max(log_d, axis=1, keepdims=True))
        dmat = jnp.exp(log_d - m_row)
        s = lax.dot_general(qh, kh, (((1,), (1,)), ((), ())), preferred_element_type=F32) * dmat
        inter = jnp.exp(log_inter - m_row)
        num = _mm(s.astype(BF16), vext) + inter * _mm(qh, cext.astype(BF16))
        den = num[:, ML_V_DIM:ML_V_DIM + 1]
        hh = num[:L, :ML_V_DIM] / jnp.maximum(jnp.abs(den), jnp.exp(-m_row))[:L]
        hn = hh * lax.rsqrt(jnp.mean(hh * hh, axis=-1, keepdims=True) + EPS)
        cols = slice(h * ML_V_DIM, (h + 1) * ML_V_DIM)
        ogate = 1.0 / (1.0 + jnp.exp(-o_ref[:, cols]))
        out_ref[:, cols] = (ogate * (hn * g_ref[:, cols])).astype(BF16)

        b_last = b_col[Lp - 1:Lp, :]
        log_w = b_last - b_col + i_col
        m_new = jnp.maximum(b_last + m_prev, jnp.max(log_w, axis=0, keepdims=True))
        wk = jnp.exp(log_w - m_new)
        decay = jnp.exp(b_last + m_prev - m_new)
        upd = lax.dot_general(kh, (wk * vext.astype(F32)).astype(BF16), (((0,), (0,)), ((), ())),
                              preferred_element_type=F32)
        cnew = decay * cext + upd
        cext_ref[h] = cnew
        m_ref[h] = jnp.broadcast_to(m_new, (8, LANES))

    @pl.when(c == pl.num_programs(1) - 1)
    def _():
        for h in range(H):
            cf = cext_ref[h]
            c_out_ref[0, h] = cf[:, :ML_V_DIM]
            n_out_ref[0, h] = cf[:, ML_V_DIM:ML_V_DIM + 1]
            m_out_ref[0, h] = m_ref[h][0:1, 0:1]


def _mlstm(qm, km, vm, om, gt, C0, n0, m0, g_ml, B, S, L):
    nc = S // L
    Lp = max(L, LANES)
    H = ML_HEADS
    VW = ML_V_DIM + LANES
    row = lambda w: pl.BlockSpec((L, w), lambda b, c: (b * nc + c, 0))
    st = lambda *tail: pl.BlockSpec((1, H) + tail, lambda b, c: (b, 0, 0, 0))
    return pl.pallas_call(
        functools.partial(_mlstm_kernel, L=L, Lp=Lp),
        grid=(B, nc),
        in_specs=[row(H * HEAD_DIM), row(H * HEAD_DIM), row(H * ML_V_DIM), row(H * ML_V_DIM), row(LANES),
                  st(HEAD_DIM, ML_V_DIM), st(HEAD_DIM, 1), st(1, 1),
                  pl.BlockSpec((1, H * ML_V_DIM), lambda b, c: (0, 0))],
        out_specs=[row(H * ML_V_DIM), st(HEAD_DIM, ML_V_DIM), st(HEAD_DIM, 1), st(1, 1)],
        out_shape=[jax.ShapeDtypeStruct((B * S, H * ML_V_DIM), BF16),
                   jax.ShapeDtypeStruct((B, H, HEAD_DIM, ML_V_DIM), F32),
                   jax.ShapeDtypeStruct((B, H, HEAD_DIM, 1), F32),
                   jax.ShapeDtypeStruct((B, H, 1, 1), F32)],
        scratch_shapes=[pltpu.VMEM((H, HEAD_DIM, VW), F32), pltpu.VMEM((H, 8, LANES), F32)],
        compiler_params=_cparams("parallel", "arbitrary"),
        name="mlstm",
    )(qm, km, vm, om, gt, C0, n0.reshape(B, H, HEAD_DIM, 1), m0.reshape(B, H, 1, 1), g_ml.reshape(1, -1))


def _pack_halves(lo, hi):
    lo_bits = pltpu.bitcast(lo.astype(BF16).astype(F32), jnp.uint32) >> 16
    hi_bits = pltpu.bitcast(hi.astype(BF16).astype(F32), jnp.uint32) & jnp.uint32(0xFFFF0000)
    return lo_bits | hi_bits


def _unpack_halves(w):
    lo = pltpu.bitcast(w << 16, F32).astype(BF16)
    hi = pltpu.bitcast(w & jnp.uint32(0xFFFF0000), F32).astype(BF16)
    return lo, hi


def _outproj_router_kernel(x_ref, sb_ref, ml_ref, wos_ref, wom_ref, g_ref, wr_hi_ref, wr_lo_ref, br_ref, c0_ref,
                           x2_ref, xn_ref, eid_ref, gate_ref, rank_ref, cnt_ref, carry_ref):
    i = pl.program_id(0)

    @pl.when(i == 0)
    def _():
        carry_ref[...] = c0_ref[...]

    x2 = x_ref[...] + _mm(sb_ref[...], wos_ref[...]) + _mm(ml_ref[...], wom_ref[...])
    x2_ref[...] = x2
    xn = _rms(x2, g_ref[...])
    half = xn.shape[1] // 2
    xn_ref[...] = _pack_halves(xn[:, :half], xn[:, half:])
    hi, lo = _split2(xn)
    logits = _mm(hi, wr_hi_ref[...]) + _mm(lo, wr_hi_ref[...]) + _mm(hi, wr_lo_ref[...]) + br_ref[...]
    tm = logits.shape[0]
    lane = lax.broadcasted_iota(I32, (tm, LANES), 1)
    vals, ids = [], []
    cnt = jnp.zeros((tm, LANES), F32)
    for _ in range(TOP_K):
        mx = jnp.max(logits, axis=-1, keepdims=True)
        idx = jnp.min(jnp.where(logits == mx, lane, LANES), axis=-1, keepdims=True)
        sel = lane == idx
        vals.append(mx)
        ids.append(idx)
        logits = jnp.where(sel, -jnp.inf, logits)
        cnt = cnt + jnp.where(sel, 1.0, 0.0)
    es = [jnp.exp(v - vals[0]) for v in vals]
    inv = 1.0 / sum(es)
    ti = lax.broadcasted_iota(I32, (tm, tm), 0)
    si = lax.broadcasted_iota(I32, (tm, tm), 1)
    before = _mm(jnp.where(si < ti, 1.0, 0.0).astype(BF16), cnt.astype(BF16)) + carry_ref[...]
    eid_o = jnp.zeros((tm, LANES), I32)
    gate_o = jnp.zeros((tm, LANES), F32)
    rank_o = jnp.zeros((tm, LANES), I32)
    for k in range(TOP_K):
        rk = jnp.sum(jnp.where(lane == ids[k], before, 0.0), axis=-1, keepdims=True)
        eid_o = jnp.where(lane == k, ids[k], eid_o)
        gate_o = jnp.where(lane == k, es[k] * inv, gate_o)
        rank_o = jnp.where(lane == k, rk.astype(I32), rank_o)
    eid_ref[...] = eid_o
    gate_ref[...] = gate_o
    rank_ref[...] = rank_o
    carry_ref[...] += jnp.sum(cnt, axis=0, keepdims=True)
    cnt_ref[...] = carry_ref[...]


def _outproj_router(x2d, sb_o, ml_o, w_out, norm_g, w_router, b_router, counts_in, tm):
    T, D = x2d.shape
    W = sb_o.shape[1]
    assert w_out.shape[0] == 2 * W
    w_bf = w_out.astype(BF16)
    half_w = lambda blk: pl.BlockSpec((W, D), lambda i: (blk, 0), pipeline_mode=pl.Buffered(1))
    wr = jnp.zeros((D, LANES), F32).at[:, :N_EXPERTS].set(w_router)
    wr_hi = wr.astype(BF16)
    wr_lo = (wr - wr_hi.astype(F32)).astype(BF16)
    br = jnp.full((1, LANES), NEG_BIG, F32).at[0, :N_EXPERTS].set(b_router)
    row = lambda w: pl.BlockSpec((tm, w), lambda i: (i, 0))
    return pl.pallas_call(
        _outproj_router_kernel,
        grid=(T // tm,),
        in_specs=[row(D), row(W), row(W), half_w(0), half_w(1),
                  _resident((1, D)), _resident(wr_hi.shape), _resident(wr_lo.shape), _resident(br.shape),
                  _resident((1, LANES))],
        out_specs=[row(D), row(D // 2), row(LANES), row(LANES), row(LANES),
                   pl.BlockSpec((1, LANES), lambda i: (0, 0))],
        out_shape=[jax.ShapeDtypeStruct((T, D), F32), jax.ShapeDtypeStruct((T, D // 2), jnp.uint32),
                   jax.ShapeDtypeStruct((T, LANES), I32), jax.ShapeDtypeStruct((T, LANES), F32),
                   jax.ShapeDtypeStruct((T, LANES), I32), jax.ShapeDtypeStruct((1, LANES), F32)],
        scratch_shapes=[pltpu.VMEM((1, LANES), F32)],
        compiler_params=_cparams("arbitrary"),
        name="outproj_router",
    )(x2d, sb_o, ml_o, w_bf, w_bf, norm_g.reshape(1, D), wr_hi, wr_lo, br, counts_in)


ROW_BLOCK = 128
IT_TILE, IT_COL, IT_EXPERT, IT_FIRST, IT_BLOCKS, IT_NEXT_EXPERT, IT_NEXT_COL = range(7)


def _segment_weights(it_ref, w, copies, cast):
    @pl.when(it_ref[IT_FIRST, w] == 1)
    def _():
        @pl.when(w == 0)
        def _():
            for c in copies(it_ref[IT_EXPERT, w], it_ref[IT_COL, w]):
                c.start()

        for c in copies(it_ref[IT_EXPERT, w], it_ref[IT_COL, w]):
            c.wait()
        cast()

        @pl.when(it_ref[IT_NEXT_EXPERT, w] >= 0)
        def _():
            for c in copies(it_ref[IT_NEXT_EXPERT, w], it_ref[IT_NEXT_COL, w]):
                c.start()


def _for_filled_rows(blocks, tm, compute):
    for lvl in range(1, tm // ROW_BLOCK + 1):
        @pl.when(blocks == lvl)
        def _():
            compute(lvl * ROW_BLOCK)


def _gate_up_kernel(it_ref, n_ref, x_ref, w_hbm, bg_ref, bl_ref, act_ref, stage, wg_s, wl_s, x_s, sems, *, nj):
    w = pl.program_id(0)
    tm, half = x_ref.shape
    tn = act_ref.shape[1]

    def copies(e, j):
        return [pltpu.make_async_copy(w_hbm.at[e, :, pl.ds(pl.multiple_of((h * nj + j) * tn, tn), tn)],
                                      stage.at[h], sems.at[h]) for h in range(2)]

    def cast():
        wg_s[...] = stage[0].astype(BF16)
        wl_s[...] = stage[1].astype(BF16)

    @pl.when(w < n_ref[0])
    def _():
        _segment_weights(it_ref, w, copies, cast)

        def compute(rows):
            lo, hi = _unpack_halves(x_ref[:rows, :])
            x_s[:rows, :half] = lo
            x_s[:rows, half:] = hi
            x = x_s[:rows, :]
            glu = jnp.minimum(_mm(x, wg_s[...]) + bg_ref[0], SWIGLU_LIMIT)
            lin = jnp.clip(_mm(x, wl_s[...]) + bl_ref[0], -SWIGLU_LIMIT, SWIGLU_LIMIT)
            act = glu * (1.0 / (1.0 + jnp.exp(-SWIGLU_ALPHA * glu))) * (lin + 1.0)
            act_ref[:rows, :] = act.astype(BF16)
            if rows < tm:
                act_ref[rows:, :] = jnp.zeros((tm - rows, tn), BF16)

        _for_filled_rows(it_ref[IT_BLOCKS, w], tm, compute)

    @pl.when(w >= n_ref[0])
    def _():
        act_ref[...] = jnp.zeros_like(act_ref)


def _down_kernel(it_ref, n_ref, a_ref, w_hbm, b_ref, o_ref, stage, w_s, sem):
    w = pl.program_id(0)
    tm = a_ref.shape[0]
    tn = w_s.shape[1]

    def copies(e, j):
        return [pltpu.make_async_copy(w_hbm.at[e, :, pl.ds(pl.multiple_of(j * tn, tn), tn)], stage, sem)]

    def cast():
        w_s[...] = stage[...].astype(BF16)

    @pl.when(w < n_ref[0])
    def _():
        _segment_weights(it_ref, w, copies, cast)

        def compute(rows):
            out = _mm(a_ref[:rows, :], w_s[...]) + b_ref[0]
            o_ref[:rows, :] = _pack_halves(out[:, :tn // 2], out[:, tn // 2:])
            if rows < tm:
                o_ref[rows:, :] = jnp.zeros((tm - rows, tn // 2), jnp.uint32)

        _for_filled_rows(it_ref[IT_BLOCKS, w], tm, compute)

    @pl.when(w >= n_ref[0])
    def _():
        o_ref[...] = jnp.zeros_like(o_ref)


def _work_list(counts, tm, nj, n_tiles_max):
    tpe = (counts + tm - 1) // tm
    tile_start = jnp.cumsum(tpe) - tpe
    item_end = jnp.cumsum(tpe * nj)
    item_start = item_end - tpe * nj
    n_items = item_end[-1:]
    w = jnp.arange(n_tiles_max * nj, dtype=I32)
    wc = jnp.minimum(w, n_items[0] - 1)
    e = jnp.minimum(jnp.sum((wc[:, None] >= item_end[None, :]).astype(I32), axis=1), N_EXPERTS - 1)
    local = wc - item_start[e]
    t = jnp.maximum(tpe[e], 1)
    j = local // t
    il = local - j * t
    real = w < n_items[0]
    first = ((il == 0) & real).astype(I32)
    tail = jnp.maximum(w - n_items[0], 0)
    tile = jnp.where(real, tile_start[e] + il, jnp.sum(tpe) + tail // nj)
    j = jnp.where(real, j, tail % nj)
    blocks = (jnp.clip(counts[e] - il * tm, 1, tm) + ROW_BLOCK - 1) // ROW_BLOCK
    ids = jnp.where(tpe > 0, jnp.arange(N_EXPERTS, dtype=I32), N_EXPERTS)
    later = jnp.concatenate([lax.cummin(ids[::-1])[::-1][1:], jnp.full((1,), N_EXPERTS, I32)])
    next_nonempty = jnp.where(later < N_EXPERTS, later, -1)
    last_col = j + 1 >= nj
    next_e = jnp.where(last_col, next_nonempty[e], e)
    next_j = jnp.where(last_col, 0, j + 1)
    table = jnp.stack([tile, j, e, first, blocks, next_e, next_j]).astype(I32)
    return table, n_items.astype(I32)


def _expert_ffn(xs, counts, w_gu, b_gu, w_dn, b_dn, tm, tn):
    n_rows, half = xs.shape
    D = 2 * half
    F = w_dn.shape[1]
    n_tiles = n_rows // tm
    nj = F // tn
    row_tile = lambda w, it, n: (it[IT_TILE, w], 0)
    out_tile = lambda w, it, n: (it[IT_TILE, w], it[IT_COL, w])
    act = pl.pallas_call(
        functools.partial(_gate_up_kernel, nj=nj),
        grid_spec=pltpu.PrefetchScalarGridSpec(
            num_scalar_prefetch=2,
            grid=(n_tiles * nj,),
            in_specs=[pl.BlockSpec((tm, half), row_tile),
                      pl.BlockSpec(memory_space=pl.ANY),
                      pl.BlockSpec((1, 1, tn), lambda w, it, n: (it[IT_EXPERT, w], 0, it[IT_COL, w])),
                      pl.BlockSpec((1, 1, tn), lambda w, it, n: (it[IT_EXPERT, w], 0, nj + it[IT_COL, w]))],
            out_specs=pl.BlockSpec((tm, tn), out_tile),
            scratch_shapes=[pltpu.VMEM((2, D, tn), F32), pltpu.VMEM((D, tn), BF16), pltpu.VMEM((D, tn), BF16),
                            pltpu.VMEM((tm, D), BF16), pltpu.SemaphoreType.DMA((2,))]),
        out_shape=jax.ShapeDtypeStruct((n_rows, F), BF16),
        compiler_params=_cparams("arbitrary"),
        name="expert_gate_up",
    )(*_work_list(counts, tm, nj, n_tiles), xs, w_gu, b_gu, b_gu)

    njd = D // tn
    return pl.pallas_call(
        _down_kernel,
        grid_spec=pltpu.PrefetchScalarGridSpec(
            num_scalar_prefetch=2,
            grid=(n_tiles * njd,),
            in_specs=[pl.BlockSpec((tm, F), row_tile),
                      pl.BlockSpec(memory_space=pl.ANY),
                      pl.BlockSpec((1, 1, tn), lambda w, it, n: (it[IT_EXPERT, w], 0, it[IT_COL, w]))],
            out_specs=pl.BlockSpec((tm, tn // 2), out_tile),
            scratch_shapes=[pltpu.VMEM((F, tn), F32), pltpu.VMEM((F, tn), BF16), pltpu.SemaphoreType.DMA]),
        out_shape=jax.ShapeDtypeStruct((n_rows, D // 2), jnp.uint32),
        compiler_params=_cparams("arbitrary"),
        name="expert_down",
    )(*_work_list(counts, tm, njd, n_tiles), act, w_dn, b_dn)


def _scatter_kernel(zero_ref, dest_ref, xa_ref, xb_ref, xs_ref, zbuf, sem, zsem, *, tiles_a, slot_tile):
    i = pl.program_id(0)
    tm = xa_ref.shape[0]

    @pl.when(i == 0)
    def _():
        zbuf[...] = jnp.zeros_like(zbuf)

        def tile_copy(z):
            start = pl.multiple_of(zero_ref[1 + z] * slot_tile, slot_tile)
            return pltpu.make_async_copy(zbuf, xs_ref.at[pl.ds(start, slot_tile), :], zsem)

        def start(z, carry):
            tile_copy(z).start()
            return carry

        def wait(z, carry):
            tile_copy(z).wait()
            return carry

        lax.fori_loop(0, zero_ref[0], start, 0)
        lax.fori_loop(0, zero_ref[0], wait, 0)

    def scatter_from(x_ref):
        def issue(r, carry):
            for k in range(TOP_K):
                pltpu.make_async_copy(x_ref.at[pl.ds(r, 1), :], xs_ref.at[pl.ds(dest_ref[r * TOP_K + k], 1), :],
                                      sem).start()
            return carry

        lax.fori_loop(0, tm, issue, 0, unroll=8)
        for _ in range(TOP_K):
            pltpu.make_async_copy(x_ref, xs_ref.at[pl.ds(0, tm), :], sem).wait()

    @pl.when(i < tiles_a)
    def _():
        scatter_from(xa_ref)

    @pl.when(i >= tiles_a)
    def _():
        scatter_from(xb_ref)


def _scatter_rows(xa, xb, dest, zero_list, n_rows, tm, slot_tile):
    Ta, half = xa.shape
    Tb = xb.shape[0]
    tiles_a, tiles_b = Ta // tm, Tb // tm
    return pl.pallas_call(
        functools.partial(_scatter_kernel, tiles_a=tiles_a, slot_tile=slot_tile),
        grid_spec=pltpu.PrefetchScalarGridSpec(
            num_scalar_prefetch=1,
            grid=(tiles_a + tiles_b,),
            in_specs=[pl.BlockSpec((tm * TOP_K,), lambda i, z: (i,), memory_space=pltpu.SMEM),
                      pl.BlockSpec((tm, half), lambda i, z: (jnp.minimum(i, tiles_a - 1), 0)),
                      pl.BlockSpec((tm, half), lambda i, z: (jnp.maximum(i - tiles_a, 0), 0))],
            out_specs=pl.BlockSpec(memory_space=pl.ANY),
            scratch_shapes=[pltpu.VMEM((slot_tile, half), jnp.uint32), pltpu.SemaphoreType.DMA,
                            pltpu.SemaphoreType.DMA]),
        out_shape=jax.ShapeDtypeStruct((n_rows, half), jnp.uint32),
        compiler_params=_cparams("arbitrary"),
        name="scatter_rows",
    )(zero_list, dest.reshape((Ta + Tb) * TOP_K), xa, xb)


COMBINE_PARTS = 8
COMBINE_AHEAD = 4


def _combine_kernel(dest_ref, x2_ref, gate_ref, g_ref, out_ref, y_ref, buf, sems, *, col_tile):
    tm, D = x2_ref.shape
    hw = col_tile // 2

    def unpack(w):
        lo = pltpu.bitcast(w << 16, F32)
        hi = pltpu.bitcast(w & jnp.uint32(0xFFFF0000), F32)
        parts = []
        for j in range(D // col_tile):
            parts += [lo[:, j * hw:(j + 1) * hw], hi[:, j * hw:(j + 1) * hw]]
        return jnp.concatenate(parts, axis=1)

    rows_per_part = tm // COMBINE_PARTS

    def issue(part):
        for r in range(part * rows_per_part, (part + 1) * rows_per_part):
            for k in range(TOP_K):
                pltpu.make_async_copy(out_ref.at[pl.ds(dest_ref[r * TOP_K + k], 1), :], buf.at[k, pl.ds(r, 1), :],
                                      sems.at[part]).start()

    for part in range(min(COMBINE_AHEAD, COMBINE_PARTS)):
        issue(part)
    for part in range(COMBINE_PARTS):
        rows = pl.ds(part * rows_per_part, rows_per_part)
        for k in range(TOP_K):
            pltpu.make_async_copy(out_ref.at[pl.ds(0, rows_per_part), :], buf.at[k, rows, :], sems.at[part]).wait()
        if part + COMBINE_AHEAD < COMBINE_PARTS:
            issue(part + COMBINE_AHEAD)
        gate = gate_ref[rows, :]
        acc = x2_ref[rows, :]
        for k in range(TOP_K):
            acc = acc + gate[:, k:k + 1] * unpack(buf[k, rows, :])
        y_ref[rows, :] = _rms(acc, g_ref[...])


def _combine(x2, gate, dest, out, final_g, tm, col_tile):
    T, D = x2.shape
    row = lambda w: pl.BlockSpec((tm, w), lambda i: (i, 0))
    return pl.pallas_call(
        functools.partial(_combine_kernel, col_tile=col_tile),
        grid=(T // tm,),
        in_specs=[pl.BlockSpec((tm * TOP_K,), lambda i: (i,), memory_space=pltpu.SMEM),
                  row(D), row(LANES), _resident((1, D)), pl.BlockSpec(memory_space=pl.ANY)],
        out_specs=row(D),
        out_shape=jax.ShapeDtypeStruct((T, D), F32),
        scratch_shapes=[pltpu.VMEM((TOP_K, tm, D // 2), jnp.uint32), pltpu.SemaphoreType.DMA((COMBINE_PARTS,))],
        compiler_params=_cparams("arbitrary"),
        name="combine_norm",
    )(dest.reshape(T * TOP_K), x2, gate, final_g.reshape(1, D), out)


def _tile(n, pref):
    return pref if n % pref == 0 else n


def _mixers(x, past, lw):
    (norm_mix_g, w_in, b_igate, b_fgate, g_sb_out, g_ml_out) = lw
    B, S, D = x.shape
    T = B * S
    x2d = x.reshape(T, D)
    q, kf, kb, vf, vb, qm, km, vm, om, gt = _in_projection(x2d, norm_mix_g, w_in, b_igate, b_fgate, _tile(T, 256))
    if past is None:
        sb_o = _sb_prompt(q, kb, vb, g_sb_out, B, S, R=min(8, S // KEY_BLOCK))
        C0 = jnp.zeros((B, ML_HEADS, HEAD_DIM, ML_V_DIM), F32)
        n0 = jnp.zeros((B, ML_HEADS, HEAD_DIM), F32)
        m0 = jnp.zeros((B, ML_HEADS), F32)
        L = _tile(S, 256)
    else:
        cache_k, cache_v, C0, n0, m0 = past
        sb_o = _sb_sample(q, kb, vb, cache_k, cache_v, g_sb_out, B, S, cache_k.shape[1])
        L = S
    ml_o, C, n, m = _mlstm(qm, km, vm, om, gt, C0, n0, m0, g_ml_out, B, S, L)
    state = (kf.reshape(B, S, SB_HEADS, HEAD_DIM), vf.reshape(B, S, SB_HEADS, HEAD_DIM),
             C, n.reshape(B, ML_HEADS, HEAD_DIM), m.reshape(B, ML_HEADS))
    return (x2d, sb_o, ml_o), state


MOE_ROW_TILE = 512
MOE_COL_TILE = 1024
COMBINE_TILE = 256


def kernel(x_prompt, x_sample, cache_k, cache_v, state_C, state_n, state_m, norm_mix_g, w_in, b_igate, b_fgate,
           g_sb_out, g_ml_out, w_out, norm_ffn_g, w_router, b_router, w_gate_up, b_gate_up, w_down, b_down,
           final_norm_g):
    assert w_in.shape[0] == 1, "single-layer trunk"
    lw = (norm_mix_g[0], w_in[0], b_igate[0], b_fgate[0], g_sb_out[0], g_ml_out[0])
    E = w_gate_up.shape[1]
    D = x_prompt.shape[-1]
    groups = [_mixers(x_prompt, None, lw),
              _mixers(x_sample, (cache_k[0], cache_v[0], state_C[0], state_n[0], state_m[0]), lw)]

    counts = jnp.zeros((1, LANES), F32)
    routed = []
    for (x2d, sb_o, ml_o), _ in groups:
        x2, xn2, eid, gate, rank, counts = _outproj_router(x2d, sb_o, ml_o, w_out[0], norm_ffn_g[0], w_router[0],
                                                           b_router[0], counts, _tile(x2d.shape[0], 512))
        routed.append((x2, xn2, eid, gate, rank))
    tm = MOE_ROW_TILE
    cnt = counts[0, :N_EXPERTS].astype(I32)
    padded = (cnt + tm - 1) // tm * tm
    pstart = jnp.cumsum(padded) - padded
    n_assign = sum(r[0].shape[0] for r in routed) * TOP_K
    n_rows = (-(-n_assign // tm) + N_EXPERTS) * tm
    dests = [pstart[eid[:, :TOP_K]] + rank[:, :TOP_K] for (_, _, eid, _, rank) in routed]

    tpe = padded // tm
    n_tiles = n_rows // tm
    tile_ids = jnp.arange(n_tiles, dtype=I32)
    last_of_expert = jnp.any((tile_ids[:, None] == (jnp.cumsum(tpe) - 1)[None, :]) & (tpe[None, :] > 0), axis=1)
    needs_zero = last_of_expert | (tile_ids >= jnp.sum(tpe))
    zero_list = jnp.concatenate([jnp.sum(needs_zero.astype(I32))[None],
                                 jnp.nonzero(needs_zero, size=n_tiles, fill_value=0)[0].astype(I32)])
    xs = _scatter_rows(routed[0][1], routed[1][1], jnp.concatenate(dests, axis=0), zero_list, n_rows,
                       _tile(routed[1][1].shape[0], 512), tm)
    out = _expert_ffn(xs, cnt, w_gate_up[0], b_gate_up[0].reshape(E, 1, -1), w_down[0],
                      b_down[0].reshape(E, 1, -1), tm, MOE_COL_TILE)
    ys = [_combine(x2, gate, dest, out, final_norm_g, _tile(x2.shape[0], COMBINE_TILE), MOE_COL_TILE)
          for (x2, _, _, gate, _), dest in zip(routed, dests)]

    (kp, vp, Cp, np_, mp), (ks, vs, Cs, ns, ms) = groups[0][1], groups[1][1]
    return (ys[0].reshape(x_prompt.shape), ys[1].reshape(x_sample.shape),
            kp[None], vp[None], Cp[None], np_[None], mp[None], ks[None], vs[None], Cs[None], ns[None], ms[None])
```

```python
import functools

import jax
import jax.numpy as jnp
from jax import lax
from jax.experimental import pallas as pl
from jax.experimental.pallas import tpu as pltpu

F32 = jnp.float32
BF16 = jnp.bfloat16
I32 = jnp.int32

EPS = 1e-6
SB_HEADS = 8
HEAD_DIM = 128
ML_HEADS = 4
ML_V_DIM = 256
N_EXPERTS = 32
TOP_K = 4
SWIGLU_ALPHA = 1.702
SWIGLU_LIMIT = 7.0
LANES = 128
KEY_BLOCK = 128
NEG_BIG = -1e30
EXP_ZERO_BELOW = -105.0
VMEM_LIMIT = 56 * 1024 * 1024


def _cparams(*sem, vmem_limit=VMEM_LIMIT):
    return pltpu.CompilerParams(dimension_semantics=sem, vmem_limit_bytes=vmem_limit)


def _resident(shape):
    nd = len(shape)
    return pl.BlockSpec(shape, lambda *_: (0,) * nd, pipeline_mode=pl.Buffered(1))


def _rms(x, g):
    return x * lax.rsqrt(jnp.mean(x * x, axis=-1, keepdims=True) + EPS) * g


def _log_sigmoid(z):
    return jnp.minimum(z, 0.0) - jnp.log(1.0 + jnp.exp(-jnp.abs(z)))


def _split2(x):
    hi = x.astype(BF16)
    lo = (x - hi.astype(F32)).astype(BF16)
    return hi, lo


def _split3(x):
    h1 = x.astype(BF16)
    r = x - h1.astype(F32)
    h2 = r.astype(BF16)
    h3 = (r - h2.astype(F32)).astype(BF16)
    return h1, h2, h3


def _mm(a, b):
    return jnp.dot(a, b, preferred_element_type=F32)


def _inproj_kernel(x_ref, g_ref, wq_ref, wk_ref, wv_ref, wqm_ref, wkm_ref, wvm_ref, wom_ref,
                   wg_ref, bg_ref,
                   q_ref, kf_ref, kb_ref, vf_ref, vb_ref, qm_ref, km_ref, vm_ref, om_ref, gt_ref):
    xn = _rms(x_ref[...], g_ref[...]).astype(BF16)
    q_ref[...] = (_mm(xn, wq_ref[...]) * (HEAD_DIM ** -0.5)).astype(BF16)
    k = _mm(xn, wk_ref[...])
    kf_ref[...] = k
    kb_ref[...] = k.astype(BF16)
    v = _mm(xn, wv_ref[...])
    vf_ref[...] = v
    vb_ref[...] = v.astype(BF16)
    qm_ref[...] = _mm(xn, wqm_ref[...]).astype(BF16)
    km_ref[...] = (_mm(xn, wkm_ref[...]) * (HEAD_DIM ** -0.5)).astype(BF16)
    vm_ref[...] = _mm(xn, wvm_ref[...]).astype(BF16)
    om_ref[...] = _mm(xn, wom_ref[...])
    gpre = _mm(xn, wg_ref[...]) + bg_ref[...]
    lane = lax.broadcasted_iota(I32, gpre.shape, 1)
    is_f = (lane >= ML_HEADS) & (lane < 2 * ML_HEADS)
    gt_ref[...] = jnp.where(is_f, _log_sigmoid(gpre), gpre)


def _in_projection(x2d, norm_g, w_in, b_igate, b_fgate, tm):
    T, D = x2d.shape
    sbw = SB_HEADS * HEAD_DIM
    mqk = ML_HEADS * HEAD_DIM
    mlw = ML_HEADS * ML_V_DIM
    w_bf = w_in.astype(BF16)
    o = 0
    w_specs = []
    for width in (sbw, sbw, sbw, mqk, mqk, mlw, mlw):
        assert o % width == 0
        w_specs.append(pl.BlockSpec((D, width), functools.partial(lambda blk, i: (0, blk), o // width),
                                    pipeline_mode=pl.Buffered(1)))
        o += width
    wg = jnp.zeros((D, LANES), F32).at[:, :2 * ML_HEADS].set(w_in[:, o:o + 2 * ML_HEADS]).astype(BF16)
    bg = jnp.zeros((1, LANES), F32).at[0, :ML_HEADS].set(b_igate).at[0, ML_HEADS:2 * ML_HEADS].set(b_fgate)
    row = lambda w: pl.BlockSpec((tm, w), lambda i: (i, 0))
    out_widths = (sbw, sbw, sbw, sbw, sbw, mqk, mqk, mlw, mlw, LANES)
    out_dtypes = (BF16, F32, BF16, F32, BF16, BF16, BF16, BF16, F32, F32)
    return pl.pallas_call(
        _inproj_kernel,
        grid=(T // tm,),
        in_specs=[row(D), _resident((1, D))] + w_specs + [_resident(wg.shape), _resident(bg.shape)],
        out_specs=[row(w) for w in out_widths],
        out_shape=[jax.ShapeDtypeStruct((T, w), dt) for w, dt in zip(out_widths, out_dtypes)],
        compiler_params=_cparams("parallel"),
        name="in_projection",
    )(x2d, norm_g.reshape(1, D), *([w_bf] * len(w_specs)), wg, bg)


def _suffix_matrix():
    j = lax.broadcasted_iota(I32, (KEY_BLOCK, 2 * KEY_BLOCK), 0)
    c = lax.broadcasted_iota(I32, (KEY_BLOCK, 2 * KEY_BLOCK), 1)
    return jnp.where((c >= KEY_BLOCK) | (j > c), 1.0, 0.0).astype(BF16)


def _sb_step(q, k, v, carry, umat, mask):
    R, bq, _ = q.shape
    z = jnp.einsum("rqd,rkd->rqk", q, k, preferred_element_type=F32)
    lp = jnp.log(1.0 + jnp.exp(-jnp.abs(z)))
    log_beta = jnp.minimum(z, 0.0) - lp
    log_stay = log_beta - z
    if mask is not None:
        log_stay = jnp.where(mask, log_stay, 0.0)
    hi, lo = _split2(log_stay)
    st = _mm(hi.reshape(R * bq, KEY_BLOCK), umat) + _mm(lo.reshape(R * bq, KEY_BLOCK), umat)
    st = st.reshape(R, bq, 2 * KEY_BLOCK)
    w = jnp.exp(log_beta + st[:, :, :KEY_BLOCK] + carry)
    if mask is not None:
        w = jnp.where(mask, w, 0.0)
    pv = jnp.einsum("rqk,rkd->rqd", w.astype(BF16), v, preferred_element_type=F32)
    return pv, carry + st[:, :, KEY_BLOCK:]


def _sb_prompt_kernel(q_ref, k_ref, v_ref, g_ref, o_ref, acc_ref, carry_ref, *, R):
    qi = pl.program_id(2)
    blk0 = qi * R
    q = q_ref[...].reshape(R, KEY_BLOCK, HEAD_DIM)
    umat = _suffix_matrix()
    shape3 = (R, KEY_BLOCK, KEY_BLOCK)
    t_io = lax.broadcasted_iota(I32, shape3, 1)
    s_io = lax.broadcasted_iota(I32, shape3, 2)
    r_io = lax.broadcasted_iota(I32, shape3, 0)

    def load(ref, d):
        return jnp.stack([ref[pl.ds(pl.multiple_of(jnp.maximum(blk0 + r - d, 0) * KEY_BLOCK, KEY_BLOCK),
                                    KEY_BLOCK), :] for r in range(R)])

    def penalty(d_next):
        return jnp.where(r_io < d_next - blk0, NEG_BIG, 0.0)

    pv, carry = _sb_step(q, load(k_ref, 0), load(v_ref, 0), jnp.zeros(shape3, F32), umat, s_io < t_io)
    acc_ref[...] = pv
    carry = carry + penalty(1)
    carry_ref[...] = carry

    def cond(state):
        d, mx = state
        return (d < blk0 + R) & (mx > EXP_ZERO_BELOW)

    def body(state):
        d, _ = state
        pv, carry = _sb_step(q, load(k_ref, d), load(v_ref, d), carry_ref[...], umat, None)
        acc_ref[...] += pv
        carry = carry + penalty(d + 1)
        carry_ref[...] = carry
        return d + 1, jnp.max(carry)

    lax.while_loop(cond, body, (jnp.int32(1), jnp.max(carry)))
    a = acc_ref[...]
    out = a * lax.rsqrt(jnp.mean(a * a, axis=-1, keepdims=True) + EPS) * g_ref[...]
    o_ref[...] = out.reshape(R * KEY_BLOCK, HEAD_DIM).astype(BF16)


def _sb_prompt(q, k, v, g_sb, B, S, R):
    tq = R * KEY_BLOCK
    nq = S // tq
    return pl.pallas_call(
        functools.partial(_sb_prompt_kernel, R=R),
        grid=(B, SB_HEADS, nq),
        in_specs=[pl.BlockSpec((tq, HEAD_DIM), lambda b, h, i: (b * nq + i, h)),
                  pl.BlockSpec((S, HEAD_DIM), lambda b, h, i: (b, h)),
                  pl.BlockSpec((S, HEAD_DIM), lambda b, h, i: (b, h)),
                  pl.BlockSpec((1, HEAD_DIM), lambda b, h, i: (0, h))],
        out_specs=pl.BlockSpec((tq, HEAD_DIM), lambda b, h, i: (b * nq + i, h)),
        out_shape=jax.ShapeDtypeStruct((B * S, SB_HEADS * HEAD_DIM), BF16),
        scratch_shapes=[pltpu.VMEM((R, KEY_BLOCK, HEAD_DIM), F32),
                        pltpu.VMEM((R, KEY_BLOCK, KEY_BLOCK), F32)],
        compiler_params=_cparams("parallel", "parallel", "parallel"),
        name="sb_prompt",
    )(q, k, v, g_sb.reshape(1, -1))


def _sb_sample_kernel(q_ref, kn_ref, vn_ref, ck_ref, cv_ref, g_ref, o_ref, acc_ref, carry_ref, *, S, P):
    H = SB_HEADS
    hs = lambda h: slice(h * HEAD_DIM, (h + 1) * HEAD_DIM)
    umat = _suffix_matrix()
    q = jnp.stack([q_ref[:, hs(h)] for h in range(H)])
    pad = jnp.zeros((KEY_BLOCK - S, HEAD_DIM), BF16)

    def new_keys(ref):
        return jnp.stack([jnp.concatenate([ref[:, hs(h)], pad], axis=0) for h in range(H)])

    def past_keys(ref, j):
        start = pl.multiple_of(j * KEY_BLOCK, KEY_BLOCK)
        return jnp.stack([ref[pl.ds(start, KEY_BLOCK), hs(h)].astype(BF16) for h in range(H)])

    shape3 = (H, S, KEY_BLOCK)
    t_io = lax.broadcasted_iota(I32, shape3, 1)
    s_io = lax.broadcasted_iota(I32, shape3, 2)
    pv, carry = _sb_step(q, new_keys(kn_ref), new_keys(vn_ref), jnp.zeros(shape3, F32), umat, s_io < t_io)
    acc_ref[...] = pv
    carry_ref[...] = carry

    def cond(state):
        j, mx = state
        return (j >= 0) & (mx > EXP_ZERO_BELOW)

    def body(state):
        j, _ = state
        pv, carry = _sb_step(q, past_keys(ck_ref, j), past_keys(cv_ref, j), carry_ref[...], umat, None)
        acc_ref[...] += pv
        carry_ref[...] = carry
        return j - 1, jnp.max(carry)

    lax.while_loop(cond, body, (jnp.int32(P // KEY_BLOCK - 1), jnp.max(carry)))
    a = acc_ref[...]
    a = a * lax.rsqrt(jnp.mean(a * a, axis=-1, keepdims=True) + EPS)
    for h in range(H):
        o_ref[:, hs(h)] = (a[h] * g_ref[:, hs(h)]).astype(BF16)


def _sb_sample(q, kn, vn, cache_k, cache_v, g_sb, B, S, P):
    W = SB_HEADS * HEAD_DIM
    row = pl.BlockSpec((S, W), lambda b: (b, 0))
    past = pl.BlockSpec((P, W), lambda b: (b, 0))
    return pl.pallas_call(
        functools.partial(_sb_sample_kernel, S=S, P=P),
        grid=(B,),
        in_specs=[row, row, row, past, past, pl.BlockSpec((1, W), lambda b: (0, 0))],
        out_specs=row,
        out_shape=jax.ShapeDtypeStruct((B * S, W), BF16),
        scratch_shapes=[pltpu.VMEM((SB_HEADS, S, HEAD_DIM), F32),
                        pltpu.VMEM((SB_HEADS, S, KEY_BLOCK), F32)],
        compiler_params=_cparams("parallel"),
        name="sb_sample",
    )(q, kn, vn, cache_k.reshape(B * P, W), cache_v.reshape(B * P, W), g_sb.reshape(1, W))


def _mlstm_kernel(q_ref, k_ref, v_ref, o_ref, gt_ref, c0_ref, n0_ref, m0_ref, g_ref,
                  out_ref, c_out_ref, n_out_ref, m_out_ref, cext_ref, m_ref, *, L, Lp):
    c = pl.program_id(1)
    H = ML_HEADS
    VW = ML_V_DIM + LANES
    lane_row = lax.broadcasted_iota(I32, (1, LANES), 1)
    onehot0 = jnp.where(lane_row == 0, 1.0, 0.0)

    @pl.when(c == 0)
    def _():
        for h in range(H):
            cext_ref[h] = jnp.concatenate([c0_ref[0, h], n0_ref[0, h] * onehot0], axis=1)
            m_ref[h] = jnp.broadcast_to(m0_ref[0, h], (8, LANES))

    def pad_rows(a, fill=0.0):
        if Lp == L:
            return a
        return jnp.concatenate([a, jnp.full((Lp - L, a.shape[1]), fill, a.dtype)], axis=0)

    gt = gt_ref[...]
    lane = lax.broadcasted_iota(I32, (Lp, LANES), 1)
    if Lp != L:
        gt = jnp.concatenate([gt, jnp.broadcast_to(jnp.where(lane_row < H, NEG_BIG, 0.0), (Lp - L, LANES))], axis=0)
    lf = jnp.where((lane >= H) & (lane < 2 * H), gt, 0.0)
    ti = lax.broadcasted_iota(I32, (Lp, Lp), 0)
    si = lax.broadcasted_iota(I32, (Lp, Lp), 1)
    causal = si <= ti
    tri = jnp.where(causal, 1.0, 0.0).astype(BF16)
    bc = sum(_mm(tri, p) for p in _split3(lf))
    bc_t = bc.T
    gt_t = gt.T
    ones_blk = jnp.broadcast_to(onehot0, (Lp, LANES)).astype(BF16)

    for h in range(H):
        b_col = bc[:, H + h:H + h + 1]
        b_row = bc_t[H + h:H + h + 1, :]
        i_col = gt[:, h:h + 1]
        i_row = gt_t[h:h + 1, :]
        m_prev = m_ref[h][0:1, 0:1]
        qh = pad_rows(q_ref[:, h * HEAD_DIM:(h + 1) * HEAD_DIM])
        kh = pad_rows(k_ref[:, h * HEAD_DIM:(h + 1) * HEAD_DIM])
        vh = pad_rows(v_ref[:, h * ML_V_DIM:(h + 1) * ML_V_DIM])
        vext = jnp.concatenate([vh, ones_blk], axis=1)
        cext = cext_ref[h]

        log_d = jnp.where(causal, b_col - b_row + i_row, NEG_BIG)
        log_inter = b_col + m_prev
        m_row = jnp.maximum(log_inter, jnp.max(log_d, axis=1, keepdims=True))
        dmat = jnp.exp(log_d - m_row)
        s = lax.dot_general(qh, kh, (((1,), (1,)), ((), ())), preferred_element_type=F32) * dmat
        inter = jnp.exp(log_inter - m_row)
        num = _mm(s.astype(BF16), vext) + inter * _mm(qh, cext.astype(BF16))
        den = num[:, ML_V_DIM:ML_V_DIM + 1]
        hh = num[:L, :ML_V_DIM] / jnp.maximum(jnp.abs(den), jnp.exp(-m_row))[:L]
        hn = hh * lax.rsqrt(jnp.mean(hh * hh, axis=-1, keepdims=True) + EPS)
        cols = slice(h * ML_V_DIM, (h + 1) * ML_V_DIM)
        ogate = 1.0 / (1.0 + jnp.exp(-o_ref[:, cols]))
        out_ref[:, cols] = (ogate * (hn * g_ref[:, cols])).astype(BF16)

        b_last = b_col[Lp - 1:Lp, :]
        log_w = b_last - b_col + i_col
        m_new = jnp.maximum(b_last + m_prev, jnp.max(log_w, axis=0, keepdims=True))
        wk = jnp.exp(log_w - m_new)
        decay = jnp.exp(b_last + m_prev - m_new)
        upd = lax.dot_general(kh, (wk * vext.astype(F32)).astype(BF16), (((0,), (0,)), ((), ())),
                              preferred_element_type=F32)
        cnew = decay * cext + upd
        cext_ref[h] = cnew
        m_ref[h] = jnp.broadcast_to(m_new, (8, LANES))

    @pl.when(c == pl.num_programs(1) - 1)
    def _():
        for h in range(H):
            cf = cext_ref[h]
            c_out_ref[0, h] = cf[:, :ML_V_DIM]
            n_out_ref[0, h] = cf[:, ML_V_DIM:ML_V_DIM + 1]
            m_out_ref[0, h] = m_ref[h][0:1, 0:1]


def _mlstm(qm, km, vm, om, gt, C0, n0, m0, g_ml, B, S, L):
    nc = S // L
    Lp = max(L, LANES)
    H = ML_HEADS
    VW = ML_V_DIM + LANES
    row = lambda w: pl.BlockSpec((L, w), lambda b, c: (b * nc + c, 0))
    st = lambda *tail: pl.BlockSpec((1, H) + tail, lambda b, c: (b, 0, 0, 0))
    return pl.pallas_call(
        functools.partial(_mlstm_kernel, L=L, Lp=Lp),
        grid=(B, nc),
        in_specs=[row(H * HEAD_DIM), row(H * HEAD_DIM), row(H * ML_V_DIM), row(H * ML_V_DIM), row(LANES),
                  st(HEAD_DIM, ML_V_DIM), st(HEAD_DIM, 1), st(1, 1),
                  pl.BlockSpec((1, H * ML_V_DIM), lambda b, c: (0, 0))],
        out_specs=[row(H * ML_V_DIM), st(HEAD_DIM, ML_V_DIM), st(HEAD_DIM, 1), st(1, 1)],
        out_shape=[jax.ShapeDtypeStruct((B * S, H * ML_V_DIM), BF16),
                   jax.ShapeDtypeStruct((B, H, HEAD_DIM, ML_V_DIM), F32),
                   jax.ShapeDtypeStruct((B, H, HEAD_DIM, 1), F32),
                   jax.ShapeDtypeStruct((B, H, 1, 1), F32)],
        scratch_shapes=[pltpu.VMEM((H, HEAD_DIM, VW), F32), pltpu.VMEM((H, 8, LANES), F32)],
        compiler_params=_cparams("parallel", "arbitrary"),
        name="mlstm",
    )(qm, km, vm, om, gt, C0, n0.reshape(B, H, HEAD_DIM, 1), m0.reshape(B, H, 1, 1), g_ml.reshape(1, -1))


def _pack_halves(lo, hi):
    lo_bits = pltpu.bitcast(lo.astype(BF16).astype(F32), jnp.uint32) >> 16
    hi_bits = pltpu.bitcast(hi.astype(BF16).astype(F32), jnp.uint32) & jnp.uint32(0xFFFF0000)
    return lo_bits | hi_bits


def _unpack_halves(w):
    lo = pltpu.bitcast(w << 16, F32).astype(BF16)
    hi = pltpu.bitcast(w & jnp.uint32(0xFFFF0000), F32).astype(BF16)
    return lo, hi


def _outproj_router_kernel(x_ref, sb_ref, ml_ref, wos_ref, wom_ref, g_ref, wr_hi_ref, wr_lo_ref, br_ref, c0_ref,
                           x2_ref, xn_ref, eid_ref, gate_ref, rank_ref, cnt_ref, carry_ref):
    i = pl.program_id(0)

    @pl.when(i == 0)
    def _():
        carry_ref[...] = c0_ref[...]

    x2 = x_ref[...] + _mm(sb_ref[...], wos_ref[...]) + _mm(ml_ref[...], wom_ref[...])
    x2_ref[...] = x2
    xn = _rms(x2, g_ref[...])
    half = xn.shape[1] // 2
    xn_ref[...] = _pack_halves(xn[:, :half], xn[:, half:])
    hi, lo = _split2(xn)
    logits = _mm(hi, wr_hi_ref[...]) + _mm(lo, wr_hi_ref[...]) + _mm(hi, wr_lo_ref[...]) + br_ref[...]
    tm = logits.shape[0]
    lane = lax.broadcasted_iota(I32, (tm, LANES), 1)
    vals, ids = [], []
    cnt = jnp.zeros((tm, LANES), F32)
    for _ in range(TOP_K):
        mx = jnp.max(logits, axis=-1, keepdims=True)
        idx = jnp.min(jnp.where(logits == mx, lane, LANES), axis=-1, keepdims=True)
        sel = lane == idx
        vals.append(mx)
        ids.append(idx)
        logits = jnp.where(sel, -jnp.inf, logits)
        cnt = cnt + jnp.where(sel, 1.0, 0.0)
    es = [jnp.exp(v - vals[0]) for v in vals]
    inv = 1.0 / sum(es)
    ti = lax.broadcasted_iota(I32, (tm, tm), 0)
    si = lax.broadcasted_iota(I32, (tm, tm), 1)
    before = _mm(jnp.where(si < ti, 1.0, 0.0).astype(BF16), cnt.astype(BF16)) + carry_ref[...]
    eid_o = jnp.zeros((tm, LANES), I32)
    gate_o = jnp.zeros((tm, LANES), F32)
    rank_o = jnp.zeros((tm, LANES), I32)
    for k in range(TOP_K):
        rk = jnp.sum(jnp.where(lane == ids[k], before, 0.0), axis=-1, keepdims=True)
        eid_o = jnp.where(lane == k, ids[k], eid_o)
        gate_o = jnp.where(lane == k, es[k] * inv, gate_o)
        rank_o = jnp.where(lane == k, rk.astype(I32), rank_o)
    eid_ref[...] = eid_o
    gate_ref[...] = gate_o
    rank_ref[...] = rank_o
    carry_ref[...] += jnp.sum(cnt, axis=0, keepdims=True)
    cnt_ref[...] = carry_ref[...]


def _outproj_router(x2d, sb_o, ml_o, w_out, norm_g, w_router, b_router, counts_in, tm):
    T, D = x2d.shape
    W = sb_o.shape[1]
    assert w_out.shape[0] == 2 * W
    w_bf = w_out.astype(BF16)
    half_w = lambda blk: pl.BlockSpec((W, D), lambda i: (blk, 0), pipeline_mode=pl.Buffered(1))
    wr = jnp.zeros((D, LANES), F32).at[:, :N_EXPERTS].set(w_router)
    wr_hi = wr.astype(BF16)
    wr_lo = (wr - wr_hi.astype(F32)).astype(BF16)
    br = jnp.full((1, LANES), NEG_BIG, F32).at[0, :N_EXPERTS].set(b_router)
    row = lambda w: pl.BlockSpec((tm, w), lambda i: (i, 0))
    return pl.pallas_call(
        _outproj_router_kernel,
        grid=(T // tm,),
        in_specs=[row(D), row(W), row(W), half_w(0), half_w(1),
                  _resident((1, D)), _resident(wr_hi.shape), _resident(wr_lo.shape), _resident(br.shape),
                  _resident((1, LANES))],
        out_specs=[row(D), row(D // 2), row(LANES), row(LANES), row(LANES),
                   pl.BlockSpec((1, LANES), lambda i: (0, 0))],
        out_shape=[jax.ShapeDtypeStruct((T, D), F32), jax.ShapeDtypeStruct((T, D // 2), jnp.uint32),
                   jax.ShapeDtypeStruct((T, LANES), I32), jax.ShapeDtypeStruct((T, LANES), F32),
                   jax.ShapeDtypeStruct((T, LANES), I32), jax.ShapeDtypeStruct((1, LANES), F32)],
        scratch_shapes=[pltpu.VMEM((1, LANES), F32)],
        compiler_params=_cparams("arbitrary"),
        name="outproj_router",
    )(x2d, sb_o, ml_o, w_bf, w_bf, norm_g.reshape(1, D), wr_hi, wr_lo, br, counts_in)


ROW_BLOCK = 128
IT_TILE, IT_COL, IT_EXPERT, IT_FIRST, IT_BLOCKS, IT_NEXT_EXPERT, IT_NEXT_COL = range(7)


def _segment_weights(it_ref, w, copies, cast):
    @pl.when(it_ref[IT_FIRST, w] == 1)
    def _():
        @pl.when(w == 0)
        def _():
            for c in copies(it_ref[IT_EXPERT, w], it_ref[IT_COL, w]):
                c.start()

        for c in copies(it_ref[IT_EXPERT, w], it_ref[IT_COL, w]):
            c.wait()
        cast()

        @pl.when(it_ref[IT_NEXT_EXPERT, w] >= 0)
        def _():
            for c in copies(it_ref[IT_NEXT_EXPERT, w], it_ref[IT_NEXT_COL, w]):
                c.start()


def _for_filled_rows(blocks, tm, compute):
    for lvl in range(1, tm // ROW_BLOCK + 1):
        @pl.when(blocks == lvl)
        def _():
            compute(lvl * ROW_BLOCK)


def _gate_up_kernel(it_ref, n_ref, x_ref, w_hbm, bg_ref, bl_ref, act_ref, stage, wg_s, wl_s, x_s, sems, *, nj):
    w = pl.program_id(0)
    tm, half = x_ref.shape
    tn = act_ref.shape[1]

    def copies(e, j):
        return [pltpu.make_async_copy(w_hbm.at[e, :, pl.ds(pl.multiple_of((h * nj + j) * tn, tn), tn)],
                                      stage.at[h], sems.at[h]) for h in range(2)]

    def cast():
        wg_s[...] = stage[0].astype(BF16)
        wl_s[...] = stage[1].astype(BF16)

    @pl.when(w < n_ref[0])
    def _():
        _segment_weights(it_ref, w, copies, cast)

        def compute(rows):
            lo, hi = _unpack_halves(x_ref[:rows, :])
            x_s[:rows, :half] = lo
            x_s[:rows, half:] = hi
            x = x_s[:rows, :]
            glu = jnp.minimum(_mm(x, wg_s[...]) + bg_ref[0], SWIGLU_LIMIT)
            lin = jnp.clip(_mm(x, wl_s[...]) + bl_ref[0], -SWIGLU_LIMIT, SWIGLU_LIMIT)
            act = glu * (1.0 / (1.0 + jnp.exp(-SWIGLU_ALPHA * glu))) * (lin + 1.0)
            act_ref[:rows, :] = act.astype(BF16)
            if rows < tm:
                act_ref[rows:, :] = jnp.zeros((tm - rows, tn), BF16)

        _for_filled_rows(it_ref[IT_BLOCKS, w], tm, compute)

    @pl.when(w >= n_ref[0])
    def _():
        act_ref[...] = jnp.zeros_like(act_ref)


def _down_kernel(it_ref, n_ref, a_ref, w_hbm, b_ref, o_ref, stage, w_s, sem):
    w = pl.program_id(0)
    tm = a_ref.shape[0]
    tn = w_s.shape[1]

    def copies(e, j):
        return [pltpu.make_async_copy(w_hbm.at[e, :, pl.ds(pl.multiple_of(j * tn, tn), tn)], stage, sem)]

    def cast():
        w_s[...] = stage[...].astype(BF16)

    @pl.when(w < n_ref[0])
    def _():
        _segment_weights(it_ref, w, copies, cast)

        def compute(rows):
            out = _mm(a_ref[:rows, :], w_s[...]) + b_ref[0]
            o_ref[:rows, :] = _pack_halves(out[:, :tn // 2], out[:, tn // 2:])
            if rows < tm:
                o_ref[rows:, :] = jnp.zeros((tm - rows, tn // 2), jnp.uint32)

        _for_filled_rows(it_ref[IT_BLOCKS, w], tm, compute)

    @pl.when(w >= n_ref[0])
    def _():
        o_ref[...] = jnp.zeros_like(o_ref)


def _work_list(counts, tm, nj, n_tiles_max):
    tpe = (counts + tm - 1) // tm
    tile_start = jnp.cumsum(tpe) - tpe
    item_end = jnp.cumsum(tpe * nj)
    item_start = item_end - tpe * nj
    n_items = item_end[-1:]
    w = jnp.arange(n_tiles_max * nj, dtype=I32)
    wc = jnp.minimum(w, n_items[0] - 1)
    e = jnp.minimum(jnp.sum((wc[:, None] >= item_end[None, :]).astype(I32), axis=1), N_EXPERTS - 1)
    local = wc - item_start[e]
    t = jnp.maximum(tpe[e], 1)
    j = local // t
    il = local - j * t
    real = w < n_items[0]
    first = ((il == 0) & real).astype(I32)
    tail = jnp.maximum(w - n_items[0], 0)
    tile = jnp.where(real, tile_start[e] + il, jnp.sum(tpe) + tail // nj)
    j = jnp.where(real, j, tail % nj)
    blocks = (jnp.clip(counts[e] - il * tm, 1, tm) + ROW_BLOCK - 1) // ROW_BLOCK
    ids = jnp.where(tpe > 0, jnp.arange(N_EXPERTS, dtype=I32), N_EXPERTS)
    later = jnp.concatenate([lax.cummin(ids[::-1])[::-1][1:], jnp.full((1,), N_EXPERTS, I32)])
    next_nonempty = jnp.where(later < N_EXPERTS, later, -1)
    last_col = j + 1 >= nj
    next_e = jnp.where(last_col, next_nonempty[e], e)
    next_j = jnp.where(last_col, 0, j + 1)
    table = jnp.stack([tile, j, e, first, blocks, next_e, next_j]).astype(I32)
    return table, n_items.astype(I32)


def _expert_ffn(xs, counts, w_gu, b_gu, w_dn, b_dn, tm, tn, tn_down):
    n_rows, half = xs.shape
    D = 2 * half
    F = w_dn.shape[1]
    n_tiles = n_rows // tm
    nj = F // tn
    row_tile = lambda w, it, n: (it[IT_TILE, w], 0)
    out_tile = lambda w, it, n: (it[IT_TILE, w], it[IT_COL, w])
    act = pl.pallas_call(
        functools.partial(_gate_up_kernel, nj=nj),
        grid_spec=pltpu.PrefetchScalarGridSpec(
            num_scalar_prefetch=2,
            grid=(n_tiles * nj,),
            in_specs=[pl.BlockSpec((tm, half), row_tile),
                      pl.BlockSpec(memory_space=pl.ANY),
                      pl.BlockSpec((1, 1, tn), lambda w, it, n: (it[IT_EXPERT, w], 0, it[IT_COL, w])),
                      pl.BlockSpec((1, 1, tn), lambda w, it, n: (it[IT_EXPERT, w], 0, nj + it[IT_COL, w]))],
            out_specs=pl.BlockSpec((tm, tn), out_tile),
            scratch_shapes=[pltpu.VMEM((2, D, tn), F32), pltpu.VMEM((D, tn), BF16), pltpu.VMEM((D, tn), BF16),
                            pltpu.VMEM((tm, D), BF16), pltpu.SemaphoreType.DMA((2,))]),
        out_shape=jax.ShapeDtypeStruct((n_rows, F), BF16),
        compiler_params=_cparams("arbitrary"),
        name="expert_gate_up",
    )(*_work_list(counts, tm, nj, n_tiles), xs, w_gu, b_gu, b_gu)

    tn = tn_down
    njd = D // tn
    return pl.pallas_call(
        _down_kernel,
        grid_spec=pltpu.PrefetchScalarGridSpec(
            num_scalar_prefetch=2,
            grid=(n_tiles * njd,),
            in_specs=[pl.BlockSpec((tm, F), row_tile),
                      pl.BlockSpec(memory_space=pl.ANY),
                      pl.BlockSpec((1, 1, tn), lambda w, it, n: (it[IT_EXPERT, w], 0, it[IT_COL, w]))],
            out_specs=pl.BlockSpec((tm, tn // 2), out_tile),
            scratch_shapes=[pltpu.VMEM((F, tn), F32), pltpu.VMEM((F, tn), BF16), pltpu.SemaphoreType.DMA]),
        out_shape=jax.ShapeDtypeStruct((n_rows, D // 2), jnp.uint32),
        compiler_params=_cparams("arbitrary"),
        name="expert_down",
    )(*_work_list(counts, tm, njd, n_tiles), act, w_dn, b_dn)


def _scatter_kernel(zero_ref, dest_ref, xa_ref, xb_ref, xs_ref, zbuf, sem, zsem, *, tiles_a, slot_tile):
    i = pl.program_id(0)
    tm = xa_ref.shape[0]

    @pl.when(i == 0)
    def _():
        zbuf[...] = jnp.zeros_like(zbuf)

        def tile_copy(z):
            start = pl.multiple_of(zero_ref[1 + z] * slot_tile, slot_tile)
            return pltpu.make_async_copy(zbuf, xs_ref.at[pl.ds(start, slot_tile), :], zsem)

        def start(z, carry):
            tile_copy(z).start()
            return carry

        def wait(z, carry):
            tile_copy(z).wait()
            return carry

        lax.fori_loop(0, zero_ref[0], start, 0)
        lax.fori_loop(0, zero_ref[0], wait, 0)

    def scatter_from(x_ref):
        def issue(r, carry):
            for k in range(TOP_K):
                pltpu.make_async_copy(x_ref.at[pl.ds(r, 1), :], xs_ref.at[pl.ds(dest_ref[r * TOP_K + k], 1), :],
                                      sem).start()
            return carry

        lax.fori_loop(0, tm, issue, 0, unroll=8)
        for _ in range(TOP_K):
            pltpu.make_async_copy(x_ref, xs_ref.at[pl.ds(0, tm), :], sem).wait()

    @pl.when(i < tiles_a)
    def _():
        scatter_from(xa_ref)

    @pl.when(i >= tiles_a)
    def _():
        scatter_from(xb_ref)


def _scatter_rows(xa, xb, dest, zero_list, n_rows, tm, slot_tile):
    Ta, half = xa.shape
    Tb = xb.shape[0]
    tiles_a, tiles_b = Ta // tm, Tb // tm
    return pl.pallas_call(
        functools.partial(_scatter_kernel, tiles_a=tiles_a, slot_tile=slot_tile),
        grid_spec=pltpu.PrefetchScalarGridSpec(
            num_scalar_prefetch=1,
            grid=(tiles_a + tiles_b,),
            in_specs=[pl.BlockSpec((tm * TOP_K,), lambda i, z: (i,), memory_space=pltpu.SMEM),
                      pl.BlockSpec((tm, half), lambda i, z: (jnp.minimum(i, tiles_a - 1), 0)),
                      pl.BlockSpec((tm, half), lambda i, z: (jnp.maximum(i - tiles_a, 0), 0))],
            out_specs=pl.BlockSpec(memory_space=pl.ANY),
            scratch_shapes=[pltpu.VMEM((slot_tile, half), jnp.uint32), pltpu.SemaphoreType.DMA,
                            pltpu.SemaphoreType.DMA]),
        out_shape=jax.ShapeDtypeStruct((n_rows, half), jnp.uint32),
        compiler_params=_cparams("arbitrary"),
        name="scatter_rows",
    )(zero_list, dest.reshape((Ta + Tb) * TOP_K), xa, xb)


COMBINE_PARTS = 2


def _combine_kernel(dest_ref, x2_ref, gate_ref, g_ref, out_ref, y_ref, buf, sems, *, col_tile):
    tm, D = x2_ref.shape
    hw = col_tile // 2

    def unpack(w):
        lo = pltpu.bitcast(w << 16, F32)
        hi = pltpu.bitcast(w & jnp.uint32(0xFFFF0000), F32)
        parts = []
        for j in range(D // col_tile):
            parts += [lo[:, j * hw:(j + 1) * hw], hi[:, j * hw:(j + 1) * hw]]
        return jnp.concatenate(parts, axis=1)

    rows_per_part = tm // COMBINE_PARTS

    def issue(r, part):
        for k in range(TOP_K):
            pltpu.make_async_copy(out_ref.at[pl.ds(dest_ref[r * TOP_K + k], 1), :], buf.at[k, pl.ds(r, 1), :],
                                  sems.at[part]).start()
        return part

    for part in range(COMBINE_PARTS):
        lax.fori_loop(part * rows_per_part, (part + 1) * rows_per_part, issue, part, unroll=8)
    for part in range(COMBINE_PARTS):
        rows = pl.ds(part * rows_per_part, rows_per_part)
        for k in range(TOP_K):
            pltpu.make_async_copy(out_ref.at[pl.ds(0, rows_per_part), :], buf.at[k, rows, :], sems.at[part]).wait()
        gate = gate_ref[rows, :]
        acc = x2_ref[rows, :]
        for k in range(TOP_K):
            acc = acc + gate[:, k:k + 1] * unpack(buf[k, rows, :])
        y_ref[rows, :] = _rms(acc, g_ref[...])


def _combine(x2, gate, dest, out, final_g, tm, col_tile):
    T, D = x2.shape
    row = lambda w: pl.BlockSpec((tm, w), lambda i: (i, 0))
    return pl.pallas_call(
        functools.partial(_combine_kernel, col_tile=col_tile),
        grid=(T // tm,),
        in_specs=[pl.BlockSpec((tm * TOP_K,), lambda i: (i,), memory_space=pltpu.SMEM),
                  row(D), row(LANES), _resident((1, D)), pl.BlockSpec(memory_space=pl.ANY)],
        out_specs=row(D),
        out_shape=jax.ShapeDtypeStruct((T, D), F32),
        scratch_shapes=[pltpu.VMEM((TOP_K, tm, D // 2), jnp.uint32), pltpu.SemaphoreType.DMA((COMBINE_PARTS,))],
        compiler_params=_cparams("arbitrary"),
        name="combine_norm",
    )(dest.reshape(T * TOP_K), x2, gate, final_g.reshape(1, D), out)


def _tile(n, pref):
    return pref if n % pref == 0 else n


def _mixers(x, past, lw):
    (norm_mix_g, w_in, b_igate, b_fgate, g_sb_out, g_ml_out) = lw
    B, S, D = x.shape
    T = B * S
    x2d = x.reshape(T, D)
    q, kf, kb, vf, vb, qm, km, vm, om, gt = _in_projection(x2d, norm_mix_g, w_in, b_igate, b_fgate, _tile(T, 256))
    if past is None:
        sb_o = _sb_prompt(q, kb, vb, g_sb_out, B, S, R=min(8, S // KEY_BLOCK))
        C0 = jnp.zeros((B, ML_HEADS, HEAD_DIM, ML_V_DIM), F32)
        n0 = jnp.zeros((B, ML_HEADS, HEAD_DIM), F32)
        m0 = jnp.zeros((B, ML_HEADS), F32)
        L = _tile(S, 256)
    else:
        cache_k, cache_v, C0, n0, m0 = past
        sb_o = _sb_sample(q, kb, vb, cache_k, cache_v, g_sb_out, B, S, cache_k.shape[1])
        L = S
    ml_o, C, n, m = _mlstm(qm, km, vm, om, gt, C0, n0, m0, g_ml_out, B, S, L)
    state = (kf.reshape(B, S, SB_HEADS, HEAD_DIM), vf.reshape(B, S, SB_HEADS, HEAD_DIM),
             C, n.reshape(B, ML_HEADS, HEAD_DIM), m.reshape(B, ML_HEADS))
    return (x2d, sb_o, ml_o), state


MOE_ROW_TILE = 512
MOE_COL_TILE = 1024
MOE_DOWN_COL_TILE = 2048
COMBINE_TILE = 256


def kernel(x_prompt, x_sample, cache_k, cache_v, state_C, state_n, state_m, norm_mix_g, w_in, b_igate, b_fgate,
           g_sb_out, g_ml_out, w_out, norm_ffn_g, w_router, b_router, w_gate_up, b_gate_up, w_down, b_down,
           final_norm_g):
    assert w_in.shape[0] == 1, "single-layer trunk"
    lw = (norm_mix_g[0], w_in[0], b_igate[0], b_fgate[0], g_sb_out[0], g_ml_out[0])
    E = w_gate_up.shape[1]
    D = x_prompt.shape[-1]
    groups = [_mixers(x_prompt, None, lw),
              _mixers(x_sample, (cache_k[0], cache_v[0], state_C[0], state_n[0], state_m[0]), lw)]

    counts = jnp.zeros((1, LANES), F32)
    routed = []
    for (x2d, sb_o, ml_o), _ in groups:
        x2, xn2, eid, gate, rank, counts = _outproj_router(x2d, sb_o, ml_o, w_out[0], norm_ffn_g[0], w_router[0],
                                                           b_router[0], counts, _tile(x2d.shape[0], 512))
        routed.append((x2, xn2, eid, gate, rank))
    tm = MOE_ROW_TILE
    cnt = counts[0, :N_EXPERTS].astype(I32)
    padded = (cnt + tm - 1) // tm * tm
    pstart = jnp.cumsum(padded) - padded
    n_assign = sum(r[0].shape[0] for r in routed) * TOP_K
    n_rows = (-(-n_assign // tm) + N_EXPERTS) * tm
    dests = [pstart[eid[:, :TOP_K]] + rank[:, :TOP_K] for (_, _, eid, _, rank) in routed]

    tpe = padded // tm
    n_tiles = n_rows // tm
    tile_ids = jnp.arange(n_tiles, dtype=I32)
    last_of_expert = jnp.any((tile_ids[:, None] == (jnp.cumsum(tpe) - 1)[None, :]) & (tpe[None, :] > 0), axis=1)
    needs_zero = last_of_expert | (tile_ids >= jnp.sum(tpe))
    zero_list = jnp.concatenate([jnp.sum(needs_zero.astype(I32))[None],
                                 jnp.nonzero(needs_zero, size=n_tiles, fill_value=0)[0].astype(I32)])
    xs = _scatter_rows(routed[0][1], routed[1][1], jnp.concatenate(dests, axis=0), zero_list, n_rows,
                       _tile(routed[1][1].shape[0], 512), tm)
    out = _expert_ffn(xs, cnt, w_gate_up[0], b_gate_up[0].reshape(E, 1, -1), w_down[0],
                      b_down[0].reshape(E, 1, -1), tm, MOE_COL_TILE, MOE_DOWN_COL_TILE)
    ys = [_combine(x2, gate, dest, out, final_norm_g, _tile(x2.shape[0], COMBINE_TILE), MOE_DOWN_COL_TILE)
          for (x2, _, _, gate, _), dest in zip(routed, dests)]

    (kp, vp, Cp, np_, mp), (ks, vs, Cs, ns, ms) = groups[0][1], groups[1][1]
    return (ys[0].reshape(x_prompt.shape), ys[1].reshape(x_sample.shape),
            kp[None], vp[None], Cp[None], np_[None], mp[None], ks[None], vs[None], Cs[None], ns[None], ms[None])
```

```python
import functools

import jax
import jax.numpy as jnp
from jax import lax
from jax.experimental import pallas as pl
from jax.experimental.pallas import tpu as pltpu

F32 = jnp.float32
BF16 = jnp.bfloat16
I32 = jnp.int32

EPS = 1e-6
SB_HEADS = 8
HEAD_DIM = 128
ML_HEADS = 4
ML_V_DIM = 256
N_EXPERTS = 32
TOP_K = 4
SWIGLU_ALPHA = 1.702
SWIGLU_LIMIT = 7.0
LANES = 128
KEY_BLOCK = 128
NEG_BIG = -1e30
EXP_ZERO_BELOW = -105.0
VMEM_LIMIT = 56 * 1024 * 1024


def _cparams(*sem, vmem_limit=VMEM_LIMIT):
    return pltpu.CompilerParams(dimension_semantics=sem, vmem_limit_bytes=vmem_limit)


def _resident(shape):
    nd = len(shape)
    return pl.BlockSpec(shape, lambda *_: (0,) * nd, pipeline_mode=pl.Buffered(1))


def _rms(x, g):
    return x * lax.rsqrt(jnp.mean(x * x, axis=-1, keepdims=True) + EPS) * g


def _log_sigmoid(z):
    return jnp.minimum(z, 0.0) - jnp.log(1.0 + jnp.exp(-jnp.abs(z)))


def _split2(x):
    hi = x.astype(BF16)
    lo = (x - hi.astype(F32)).astype(BF16)
    return hi, lo


def _split3(x):
    h1 = x.astype(BF16)
    r = x - h1.astype(F32)
    h2 = r.astype(BF16)
    h3 = (r - h2.astype(F32)).astype(BF16)
    return h1, h2, h3


def _mm(a, b):
    return jnp.dot(a, b, preferred_element_type=F32)


def _inproj_kernel(x_ref, g_ref, wq_ref, wk_ref, wv_ref, wqm_ref, wkm_ref, wvm_ref, wom_ref,
                   wg_ref, bg_ref,
                   q_ref, kf_ref, kb_ref, vf_ref, vb_ref, qm_ref, km_ref, vm_ref, om_ref, gt_ref):
    xn = _rms(x_ref[...], g_ref[...]).astype(BF16)
    q_ref[...] = (_mm(xn, wq_ref[...]) * (HEAD_DIM ** -0.5)).astype(BF16)
    k = _mm(xn, wk_ref[...])
    kf_ref[...] = k
    kb_ref[...] = k.astype(BF16)
    v = _mm(xn, wv_ref[...])
    vf_ref[...] = v
    vb_ref[...] = v.astype(BF16)
    qm_ref[...] = _mm(xn, wqm_ref[...]).astype(BF16)
    km_ref[...] = (_mm(xn, wkm_ref[...]) * (HEAD_DIM ** -0.5)).astype(BF16)
    vm_ref[...] = _mm(xn, wvm_ref[...]).astype(BF16)
    om_ref[...] = _mm(xn, wom_ref[...])
    gpre = _mm(xn, wg_ref[...]) + bg_ref[...]
    lane = lax.broadcasted_iota(I32, gpre.shape, 1)
    is_f = (lane >= ML_HEADS) & (lane < 2 * ML_HEADS)
    gt_ref[...] = jnp.where(is_f, _log_sigmoid(gpre), gpre)


def _in_projection(x2d, norm_g, w_in, b_igate, b_fgate, tm):
    T, D = x2d.shape
    sbw = SB_HEADS * HEAD_DIM
    mqk = ML_HEADS * HEAD_DIM
    mlw = ML_HEADS * ML_V_DIM
    w_bf = w_in.astype(BF16)
    o = 0
    w_specs = []
    for width in (sbw, sbw, sbw, mqk, mqk, mlw, mlw):
        assert o % width == 0
        w_specs.append(pl.BlockSpec((D, width), functools.partial(lambda blk, i: (0, blk), o // width),
                                    pipeline_mode=pl.Buffered(1)))
        o += width
    wg = jnp.zeros((D, LANES), F32).at[:, :2 * ML_HEADS].set(w_in[:, o:o + 2 * ML_HEADS]).astype(BF16)
    bg = jnp.zeros((1, LANES), F32).at[0, :ML_HEADS].set(b_igate).at[0, ML_HEADS:2 * ML_HEADS].set(b_fgate)
    row = lambda w: pl.BlockSpec((tm, w), lambda i: (i, 0))
    out_widths = (sbw, sbw, sbw, sbw, sbw, mqk, mqk, mlw, mlw, LANES)
    out_dtypes = (BF16, F32, BF16, F32, BF16, BF16, BF16, BF16, F32, F32)
    return pl.pallas_call(
        _inproj_kernel,
        grid=(T // tm,),
        in_specs=[row(D), _resident((1, D))] + w_specs + [_resident(wg.shape), _resident(bg.shape)],
        out_specs=[row(w) for w in out_widths],
        out_shape=[jax.ShapeDtypeStruct((T, w), dt) for w, dt in zip(out_widths, out_dtypes)],
        compiler_params=_cparams("parallel"),
        name="in_projection",
    )(x2d, norm_g.reshape(1, D), *([w_bf] * len(w_specs)), wg, bg)


def _suffix_matrix():
    j = lax.broadcasted_iota(I32, (KEY_BLOCK, 2 * KEY_BLOCK), 0)
    c = lax.broadcasted_iota(I32, (KEY_BLOCK, 2 * KEY_BLOCK), 1)
    return jnp.where((c >= KEY_BLOCK) | (j > c), 1.0, 0.0).astype(BF16)


def _sb_step(q, k, v, carry, umat, mask):
    R, bq, _ = q.shape
    z = jnp.einsum("rqd,rkd->rqk", q, k, preferred_element_type=F32)
    lp = jnp.log(1.0 + jnp.exp(-jnp.abs(z)))
    log_beta = jnp.minimum(z, 0.0) - lp
    log_stay = log_beta - z
    if mask is not None:
        log_stay = jnp.where(mask, log_stay, 0.0)
    hi, lo = _split2(log_stay)
    st = _mm(hi.reshape(R * bq, KEY_BLOCK), umat) + _mm(lo.reshape(R * bq, KEY_BLOCK), umat)
    st = st.reshape(R, bq, 2 * KEY_BLOCK)
    w = jnp.exp(log_beta + st[:, :, :KEY_BLOCK] + carry)
    if mask is not None:
        w = jnp.where(mask, w, 0.0)
    pv = jnp.einsum("rqk,rkd->rqd", w.astype(BF16), v, preferred_element_type=F32)
    return pv, carry + st[:, :, KEY_BLOCK:]


def _sb_prompt_kernel(q_ref, k_ref, v_ref, g_ref, o_ref, acc_ref, carry_ref, *, R):
    qi = pl.program_id(2)
    blk0 = qi * R
    q = q_ref[...].reshape(R, KEY_BLOCK, HEAD_DIM)
    umat = _suffix_matrix()
    shape3 = (R, KEY_BLOCK, KEY_BLOCK)
    t_io = lax.broadcasted_iota(I32, shape3, 1)
    s_io = lax.broadcasted_iota(I32, shape3, 2)
    r_io = lax.broadcasted_iota(I32, shape3, 0)

    def load(ref, d):
        return jnp.stack([ref[pl.ds(pl.multiple_of(jnp.maximum(blk0 + r - d, 0) * KEY_BLOCK, KEY_BLOCK),
                                    KEY_BLOCK), :] for r in range(R)])

    def penalty(d_next):
        return jnp.where(r_io < d_next - blk0, NEG_BIG, 0.0)

    pv, carry = _sb_step(q, load(k_ref, 0), load(v_ref, 0), jnp.zeros(shape3, F32), umat, s_io < t_io)
    acc_ref[...] = pv
    carry = carry + penalty(1)
    carry_ref[...] = carry

    def cond(state):
        d, mx = state
        return (d < blk0 + R) & (mx > EXP_ZERO_BELOW)

    def body(state):
        d, _ = state
        pv, carry = _sb_step(q, load(k_ref, d), load(v_ref, d), carry_ref[...], umat, None)
        acc_ref[...] += pv
        carry = carry + penalty(d + 1)
        carry_ref[...] = carry
        return d + 1, jnp.max(carry)

    lax.while_loop(cond, body, (jnp.int32(1), jnp.max(carry)))
    a = acc_ref[...]
    out = a * lax.rsqrt(jnp.mean(a * a, axis=-1, keepdims=True) + EPS) * g_ref[...]
    o_ref[...] = out.reshape(R * KEY_BLOCK, HEAD_DIM).astype(BF16)


def _sb_prompt(q, k, v, g_sb, B, S, R):
    tq = R * KEY_BLOCK
    nq = S // tq
    return pl.pallas_call(
        functools.partial(_sb_prompt_kernel, R=R),
        grid=(B, SB_HEADS, nq),
        in_specs=[pl.BlockSpec((tq, HEAD_DIM), lambda b, h, i: (b * nq + i, h)),
                  pl.BlockSpec((S, HEAD_DIM), lambda b, h, i: (b, h)),
                  pl.BlockSpec((S, HEAD_DIM), lambda b, h, i: (b, h)),
                  pl.BlockSpec((1, HEAD_DIM), lambda b, h, i: (0, h))],
        out_specs=pl.BlockSpec((tq, HEAD_DIM), lambda b, h, i: (b * nq + i, h)),
        out_shape=jax.ShapeDtypeStruct((B * S, SB_HEADS * HEAD_DIM), BF16),
        scratch_shapes=[pltpu.VMEM((R, KEY_BLOCK, HEAD_DIM), F32),
                        pltpu.VMEM((R, KEY_BLOCK, KEY_BLOCK), F32)],
        compiler_params=_cparams("parallel", "parallel", "parallel"),
        name="sb_prompt",
    )(q, k, v, g_sb.reshape(1, -1))


def _sb_sample_kernel(q_ref, kn_ref, vn_ref, ck_ref, cv_ref, g_ref, o_ref, acc_ref, carry_ref, *, S, P):
    H = SB_HEADS
    hs = lambda h: slice(h * HEAD_DIM, (h + 1) * HEAD_DIM)
    umat = _suffix_matrix()
    q = jnp.stack([q_ref[:, hs(h)] for h in range(H)])
    pad = jnp.zeros((KEY_BLOCK - S, HEAD_DIM), BF16)

    def new_keys(ref):
        return jnp.stack([jnp.concatenate([ref[:, hs(h)], pad], axis=0) for h in range(H)])

    def past_keys(ref, j):
        start = pl.multiple_of(j * KEY_BLOCK, KEY_BLOCK)
        return jnp.stack([ref[pl.ds(start, KEY_BLOCK), hs(h)].astype(BF16) for h in range(H)])

    shape3 = (H, S, KEY_BLOCK)
    t_io = lax.broadcasted_iota(I32, shape3, 1)
    s_io = lax.broadcasted_iota(I32, shape3, 2)
    pv, carry = _sb_step(q, new_keys(kn_ref), new_keys(vn_ref), jnp.zeros(shape3, F32), umat, s_io < t_io)
    acc_ref[...] = pv
    carry_ref[...] = carry

    def cond(state):
        j, mx = state
        return (j >= 0) & (mx > EXP_ZERO_BELOW)

    def body(state):
        j, _ = state
        pv, carry = _sb_step(q, past_keys(ck_ref, j), past_keys(cv_ref, j), carry_ref[...], umat, None)
        acc_ref[...] += pv
        carry_ref[...] = carry
        return j - 1, jnp.max(carry)

    lax.while_loop(cond, body, (jnp.int32(P // KEY_BLOCK - 1), jnp.max(carry)))
    a = acc_ref[...]
    a = a * lax.rsqrt(jnp.mean(a * a, axis=-1, keepdims=True) + EPS)
    for h in range(H):
        o_ref[:, hs(h)] = (a[h] * g_ref[:, hs(h)]).astype(BF16)


def _sb_sample(q, kn, vn, cache_k, cache_v, g_sb, B, S, P):
    W = SB_HEADS * HEAD_DIM
    row = pl.BlockSpec((S, W), lambda b: (b, 0))
    past = pl.BlockSpec((P, W), lambda b: (b, 0))
    return pl.pallas_call(
        functools.partial(_sb_sample_kernel, S=S, P=P),
        grid=(B,),
        in_specs=[row, row, row, past, past, pl.BlockSpec((1, W), lambda b: (0, 0))],
        out_specs=row,
        out_shape=jax.ShapeDtypeStruct((B * S, W), BF16),
        scratch_shapes=[pltpu.VMEM((SB_HEADS, S, HEAD_DIM), F32),
                        pltpu.VMEM((SB_HEADS, S, KEY_BLOCK), F32)],
        compiler_params=_cparams("parallel"),
        name="sb_sample",
    )(q, kn, vn, cache_k.reshape(B * P, W), cache_v.reshape(B * P, W), g_sb.reshape(1, W))


def _mlstm_kernel(q_ref, k_ref, v_ref, o_ref, gt_ref, c0_ref, n0_ref, m0_ref, g_ref,
                  out_ref, c_out_ref, n_out_ref, m_out_ref, cext_ref, m_ref, *, L, Lp):
    c = pl.program_id(1)
    H = ML_HEADS
    VW = ML_V_DIM + LANES
    lane_row = lax.broadcasted_iota(I32, (1, LANES), 1)
    onehot0 = jnp.where(lane_row == 0, 1.0, 0.0)

    @pl.when(c == 0)
    def _():
        for h in range(H):
            cext_ref[h] = jnp.concatenate([c0_ref[0, h], n0_ref[0, h] * onehot0], axis=1)
            m_ref[h] = jnp.broadcast_to(m0_ref[0, h], (8, LANES))

    def pad_rows(a, fill=0.0):
        if Lp == L:
            return a
        return jnp.concatenate([a, jnp.full((Lp - L, a.shape[1]), fill, a.dtype)], axis=0)

    gt = gt_ref[...]
    lane = lax.broadcasted_iota(I32, (Lp, LANES), 1)
    if Lp != L:
        gt = jnp.concatenate([gt, jnp.broadcast_to(jnp.where(lane_row < H, NEG_BIG, 0.0), (Lp - L, LANES))], axis=0)
    lf = jnp.where((lane >= H) & (lane < 2 * H), gt, 0.0)
    ti = lax.broadcasted_iota(I32, (Lp, Lp), 0)
    si = lax.broadcasted_iota(I32, (Lp, Lp), 1)
    causal = si <= ti
    tri = jnp.where(causal, 1.0, 0.0).astype(BF16)
    bc = sum(_mm(tri, p) for p in _split3(lf))
    bc_t = bc.T
    gt_t = gt.T
    ones_blk = jnp.broadcast_to(onehot0, (Lp, LANES)).astype(BF16)

    for h in range(H):
        b_col = bc[:, H + h:H + h + 1]
        b_row = bc_t[H + h:H + h + 1, :]
        i_col = gt[:, h:h + 1]
        i_row = gt_t[h:h + 1, :]
        m_prev = m_ref[h][0:1, 0:1]
        qh = pad_rows(q_ref[:, h * HEAD_DIM:(h + 1) * HEAD_DIM])
        kh = pad_rows(k_ref[:, h * HEAD_DIM:(h + 1) * HEAD_DIM])
        vh = pad_rows(v_ref[:, h * ML_V_DIM:(h + 1) * ML_V_DIM])
        vext = jnp.concatenate([vh, ones_blk], axis=1)
        cext = cext_ref[h]

        log_d = jnp.where(causal, b_col - b_row + i_row, NEG_BIG)
        log_inter = b_col + m_prev
        m_row = jnp.maximum(log_inter, jnp.max(log_d, axis=1, keepdims=True))
        dmat = jnp.exp(log_d - m_row)
        s = lax.dot_general(qh, kh, (((1,), (1,)), ((), ())), preferred_element_type=F32) * dmat
        inter = jnp.exp(log_inter - m_row)
        num = _mm(s.astype(BF16), vext) + inter * _mm(qh, cext.astype(BF16))
        den = num[:, ML_V_DIM:ML_V_DIM + 1]
        hh = num[:L, :ML_V_DIM] / jnp.maximum(jnp.abs(den), jnp.exp(-m_row))[:L]
        hn = hh * lax.rsqrt(jnp.mean(hh * hh, axis=-1, keepdims=True) + EPS)
        cols = slice(h * ML_V_DIM, (h + 1) * ML_V_DIM)
        ogate = 1.0 / (1.0 + jnp.exp(-o_ref[:, cols]))
        out_ref[:, cols] = (ogate * (hn * g_ref[:, cols])).astype(BF16)

        b_last = b_col[Lp - 1:Lp, :]
        log_w = b_last - b_col + i_col
        m_new = jnp.maximum(b_last + m_prev, jnp.max(log_w, axis=0, keepdims=True))
        wk = jnp.exp(log_w - m_new)
        decay = jnp.exp(b_last + m_prev - m_new)
        upd = lax.dot_general(kh, (wk * vext.astype(F32)).astype(BF16), (((0,), (0,)), ((), ())),
                              preferred_element_type=F32)
        cnew = decay * cext + upd
        cext_ref[h] = cnew
        m_ref[h] = jnp.broadcast_to(m_new, (8, LANES))

    @pl.when(c == pl.num_programs(1) - 1)
    def _():
        for h in range(H):
            cf = cext_ref[h]
            c_out_ref[0, h] = cf[:, :ML_V_DIM]
            n_out_ref[0, h] = cf[:, ML_V_DIM:ML_V_DIM + 1]
            m_out_ref[0, h] = m_ref[h][0:1, 0:1]


def _mlstm(qm, km, vm, om, gt, C0, n0, m0, g_ml, B, S, L):
    nc = S // L
    Lp = max(L, LANES)
    H = ML_HEADS
    VW = ML_V_DIM + LANES
    row = lambda w: pl.BlockSpec((L, w), lambda b, c: (b * nc + c, 0))
    st = lambda *tail: pl.BlockSpec((1, H) + tail, lambda b, c: (b, 0, 0, 0))
    return pl.pallas_call(
        functools.partial(_mlstm_kernel, L=L, Lp=Lp),
        grid=(B, nc),
        in_specs=[row(H * HEAD_DIM), row(H * HEAD_DIM), row(H * ML_V_DIM), row(H * ML_V_DIM), row(LANES),
                  st(HEAD_DIM, ML_V_DIM), st(HEAD_DIM, 1), st(1, 1),
                  pl.BlockSpec((1, H * ML_V_DIM), lambda b, c: (0, 0))],
        out_specs=[row(H * ML_V_DIM), st(HEAD_DIM, ML_V_DIM), st(HEAD_DIM, 1), st(1, 1)],
        out_shape=[jax.ShapeDtypeStruct((B * S, H * ML_V_DIM), BF16),
                   jax.ShapeDtypeStruct((B, H, HEAD_DIM, ML_V_DIM), F32),
                   jax.ShapeDtypeStruct((B, H, HEAD_DIM, 1), F32),
                   jax.ShapeDtypeStruct((B, H, 1, 1), F32)],
        scratch_shapes=[pltpu.VMEM((H, HEAD_DIM, VW), F32), pltpu.VMEM((H, 8, LANES), F32)],
        compiler_params=_cparams("parallel", "arbitrary"),
        name="mlstm",
    )(qm, km, vm, om, gt, C0, n0.reshape(B, H, HEAD_DIM, 1), m0.reshape(B, H, 1, 1), g_ml.reshape(1, -1))


def _pack_halves(lo, hi):
    lo_bits = pltpu.bitcast(lo.astype(BF16).astype(F32), jnp.uint32) >> 16
    hi_bits = pltpu.bitcast(hi.astype(BF16).astype(F32), jnp.uint32) & jnp.uint32(0xFFFF0000)
    return lo_bits | hi_bits


def _unpack_halves(w):
    lo = pltpu.bitcast(w << 16, F32).astype(BF16)
    hi = pltpu.bitcast(w & jnp.uint32(0xFFFF0000), F32).astype(BF16)
    return lo, hi


def _outproj_router_kernel(x_ref, sb_ref, ml_ref, wos_ref, wom_ref, g_ref, wr_hi_ref, wr_lo_ref, br_ref, c0_ref,
                           x2_ref, xn_ref, eid_ref, gate_ref, rank_ref, cnt_ref, carry_ref):
    i = pl.program_id(0)

    @pl.when(i == 0)
    def _():
        carry_ref[...] = c0_ref[...]

    x2 = x_ref[...] + _mm(sb_ref[...], wos_ref[...]) + _mm(ml_ref[...], wom_ref[...])
    x2_ref[...] = x2
    xn = _rms(x2, g_ref[...])
    half = xn.shape[1] // 2
    xn_ref[...] = _pack_halves(xn[:, :half], xn[:, half:])
    hi, lo = _split2(xn)
    logits = _mm(hi, wr_hi_ref[...]) + _mm(lo, wr_hi_ref[...]) + _mm(hi, wr_lo_ref[...]) + br_ref[...]
    tm = logits.shape[0]
    lane = lax.broadcasted_iota(I32, (tm, LANES), 1)
    vals, ids = [], []
    cnt = jnp.zeros((tm, LANES), F32)
    for _ in range(TOP_K):
        mx = jnp.max(logits, axis=-1, keepdims=True)
        idx = jnp.min(jnp.where(logits == mx, lane, LANES), axis=-1, keepdims=True)
        sel = lane == idx
        vals.append(mx)
        ids.append(idx)
        logits = jnp.where(sel, -jnp.inf, logits)
        cnt = cnt + jnp.where(sel, 1.0, 0.0)
    es = [jnp.exp(v - vals[0]) for v in vals]
    inv = 1.0 / sum(es)
    ti = lax.broadcasted_iota(I32, (tm, tm), 0)
    si = lax.broadcasted_iota(I32, (tm, tm), 1)
    before = _mm(jnp.where(si < ti, 1.0, 0.0).astype(BF16), cnt.astype(BF16)) + carry_ref[...]
    eid_o = jnp.zeros((tm, LANES), I32)
    gate_o = jnp.zeros((tm, LANES), F32)
    rank_o = jnp.zeros((tm, LANES), I32)
    for k in range(TOP_K):
        rk = jnp.sum(jnp.where(lane == ids[k], before, 0.0), axis=-1, keepdims=True)
        eid_o = jnp.where(lane == k, ids[k], eid_o)
        gate_o = jnp.where(lane == k, es[k] * inv, gate_o)
        rank_o = jnp.where(lane == k, rk.astype(I32), rank_o)
    eid_ref[...] = eid_o
    gate_ref[...] = gate_o
    rank_ref[...] = rank_o
    carry_ref[...] += jnp.sum(cnt, axis=0, keepdims=True)
    cnt_ref[...] = carry_ref[...]


def _outproj_router(x2d, sb_o, ml_o, w_out, norm_g, w_router, b_router, counts_in, tm):
    T, D = x2d.shape
    W = sb_o.shape[1]
    assert w_out.shape[0] == 2 * W
    w_bf = w_out.astype(BF16)
    half_w = lambda blk: pl.BlockSpec((W, D), lambda i: (blk, 0), pipeline_mode=pl.Buffered(1))
    wr = jnp.zeros((D, LANES), F32).at[:, :N_EXPERTS].set(w_router)
    wr_hi = wr.astype(BF16)
    wr_lo = (wr - wr_hi.astype(F32)).astype(BF16)
    br = jnp.full((1, LANES), NEG_BIG, F32).at[0, :N_EXPERTS].set(b_router)
    row = lambda w: pl.BlockSpec((tm, w), lambda i: (i, 0))
    return pl.pallas_call(
        _outproj_router_kernel,
        grid=(T // tm,),
        in_specs=[row(D), row(W), row(W), half_w(0), half_w(1),
                  _resident((1, D)), _resident(wr_hi.shape), _resident(wr_lo.shape), _resident(br.shape),
                  _resident((1, LANES))],
        out_specs=[row(D), row(D // 2), row(LANES), row(LANES), row(LANES),
                   pl.BlockSpec((1, LANES), lambda i: (0, 0))],
        out_shape=[jax.ShapeDtypeStruct((T, D), F32), jax.ShapeDtypeStruct((T, D // 2), jnp.uint32),
                   jax.ShapeDtypeStruct((T, LANES), I32), jax.ShapeDtypeStruct((T, LANES), F32),
                   jax.ShapeDtypeStruct((T, LANES), I32), jax.ShapeDtypeStruct((1, LANES), F32)],
        scratch_shapes=[pltpu.VMEM((1, LANES), F32)],
        compiler_params=_cparams("arbitrary"),
        name="outproj_router",
    )(x2d, sb_o, ml_o, w_bf, w_bf, norm_g.reshape(1, D), wr_hi, wr_lo, br, counts_in)


ROW_BLOCK = 128
IT_TILE, IT_COL, IT_EXPERT, IT_FIRST, IT_BLOCKS, IT_NEXT_EXPERT, IT_NEXT_COL = range(7)


def _segment_weights(it_ref, w, copies, cast):
    @pl.when(it_ref[IT_FIRST, w] == 1)
    def _():
        @pl.when(w == 0)
        def _():
            for c in copies(it_ref[IT_EXPERT, w], it_ref[IT_COL, w]):
                c.start()

        for c in copies(it_ref[IT_EXPERT, w], it_ref[IT_COL, w]):
            c.wait()
        cast()

        @pl.when(it_ref[IT_NEXT_EXPERT, w] >= 0)
        def _():
            for c in copies(it_ref[IT_NEXT_EXPERT, w], it_ref[IT_NEXT_COL, w]):
                c.start()


def _for_filled_rows(blocks, tm, compute):
    for lvl in range(1, tm // ROW_BLOCK + 1):
        @pl.when(blocks == lvl)
        def _():
            compute(lvl * ROW_BLOCK)


def _gate_up_kernel(it_ref, n_ref, x_ref, w_hbm, bg_ref, bl_ref, act_ref, stage, wg_s, wl_s, x_s, sems, *, nj):
    w = pl.program_id(0)
    tm, half = x_ref.shape
    tn = act_ref.shape[1]

    def copies(e, j):
        return [pltpu.make_async_copy(w_hbm.at[e, :, pl.ds(pl.multiple_of((h * nj + j) * tn, tn), tn)],
                                      stage.at[h], sems.at[h]) for h in range(2)]

    def cast():
        wg_s[...] = stage[0].astype(BF16)
        wl_s[...] = stage[1].astype(BF16)

    @pl.when(w < n_ref[0])
    def _():
        _segment_weights(it_ref, w, copies, cast)

        def compute(rows):
            lo, hi = _unpack_halves(x_ref[:rows, :])
            x_s[:rows, :half] = lo
            x_s[:rows, half:] = hi
            x = x_s[:rows, :]
            glu = jnp.minimum(_mm(x, wg_s[...]) + bg_ref[0], SWIGLU_LIMIT)
            lin = jnp.clip(_mm(x, wl_s[...]) + bl_ref[0], -SWIGLU_LIMIT, SWIGLU_LIMIT)
            act = glu * (1.0 / (1.0 + jnp.exp(-SWIGLU_ALPHA * glu))) * (lin + 1.0)
            act_ref[:rows, :] = act.astype(BF16)
            if rows < tm:
                act_ref[rows:, :] = jnp.zeros((tm - rows, tn), BF16)

        _for_filled_rows(it_ref[IT_BLOCKS, w], tm, compute)

    @pl.when(w >= n_ref[0])
    def _():
        act_ref[...] = jnp.zeros_like(act_ref)


def _down_kernel(it_ref, n_ref, a_ref, w_hbm, b_ref, o_ref, stage, w_s, sem):
    w = pl.program_id(0)
    tm = a_ref.shape[0]
    tn = w_s.shape[1]

    def copies(e, j):
        return [pltpu.make_async_copy(w_hbm.at[e, :, pl.ds(pl.multiple_of(j * tn, tn), tn)], stage, sem)]

    def cast():
        w_s[...] = stage[...].astype(BF16)

    @pl.when(w < n_ref[0])
    def _():
        _segment_weights(it_ref, w, copies, cast)

        def compute(rows):
            out = _mm(a_ref[:rows, :], w_s[...]) + b_ref[0]
            o_ref[:rows, :] = _pack_halves(out[:, :tn // 2], out[:, tn // 2:])
            if rows < tm:
                o_ref[rows:, :] = jnp.zeros((tm - rows, tn // 2), jnp.uint32)

        _for_filled_rows(it_ref[IT_BLOCKS, w], tm, compute)

    @pl.when(w >= n_ref[0])
    def _():
        o_ref[...] = jnp.zeros_like(o_ref)


def _work_list(counts, tm, nj, n_tiles_max):
    tpe = (counts + tm - 1) // tm
    tile_start = jnp.cumsum(tpe) - tpe
    item_end = jnp.cumsum(tpe * nj)
    item_start = item_end - tpe * nj
    n_items = item_end[-1:]
    w = jnp.arange(n_tiles_max * nj, dtype=I32)
    wc = jnp.minimum(w, n_items[0] - 1)
    e = jnp.minimum(jnp.sum((wc[:, None] >= item_end[None, :]).astype(I32), axis=1), N_EXPERTS - 1)
    ids = jnp.where(tpe > 0, jnp.arange(N_EXPERTS, dtype=I32), N_EXPERTS)
    later = jnp.concatenate([lax.cummin(ids[::-1])[::-1][1:], jnp.full((1,), N_EXPERTS, I32)])
    next_nonempty = jnp.where(later < N_EXPERTS, later, -1)
    per_expert = jnp.stack([item_start, tpe, tile_start, counts, next_nonempty], axis=1).astype(F32)
    onehot = (e[:, None] == jnp.arange(N_EXPERTS, dtype=I32)[None, :]).astype(F32)
    looked = jnp.dot(onehot, per_expert, precision=lax.Precision.HIGHEST).astype(I32)
    item_start_e, tpe_e, tile_start_e, counts_e, next_nonempty_e = (looked[:, c] for c in range(5))
    local = wc - item_start_e
    t = jnp.maximum(tpe_e, 1)
    j = local // t
    il = local - j * t
    real = w < n_items[0]
    first = ((il == 0) & real).astype(I32)
    tail = jnp.maximum(w - n_items[0], 0)
    tile = jnp.where(real, tile_start_e + il, jnp.sum(tpe) + tail // nj)
    j = jnp.where(real, j, tail % nj)
    blocks = (jnp.clip(counts_e - il * tm, 1, tm) + ROW_BLOCK - 1) // ROW_BLOCK
    last_col = j + 1 >= nj
    next_e = jnp.where(last_col, next_nonempty_e, e)
    next_j = jnp.where(last_col, 0, j + 1)
    table = jnp.stack([tile, j, e, first, blocks, next_e, next_j]).astype(I32)
    return table, n_items.astype(I32)


def _expert_ffn(xs, counts, w_gu, b_gu, w_dn, b_dn, tm, tn, tn_down):
    n_rows, half = xs.shape
    D = 2 * half
    F = w_dn.shape[1]
    n_tiles = n_rows // tm
    nj = F // tn
    row_tile = lambda w, it, n: (it[IT_TILE, w], 0)
    out_tile = lambda w, it, n: (it[IT_TILE, w], it[IT_COL, w])
    act = pl.pallas_call(
        functools.partial(_gate_up_kernel, nj=nj),
        grid_spec=pltpu.PrefetchScalarGridSpec(
            num_scalar_prefetch=2,
            grid=(n_tiles * nj,),
            in_specs=[pl.BlockSpec((tm, half), row_tile),
                      pl.BlockSpec(memory_space=pl.ANY),
                      pl.BlockSpec((1, 1, tn), lambda w, it, n: (it[IT_EXPERT, w], 0, it[IT_COL, w])),
                      pl.BlockSpec((1, 1, tn), lambda w, it, n: (it[IT_EXPERT, w], 0, nj + it[IT_COL, w]))],
            out_specs=pl.BlockSpec((tm, tn), out_tile),
            scratch_shapes=[pltpu.VMEM((2, D, tn), F32), pltpu.VMEM((D, tn), BF16), pltpu.VMEM((D, tn), BF16),
                            pltpu.VMEM((tm, D), BF16), pltpu.SemaphoreType.DMA((2,))]),
        out_shape=jax.ShapeDtypeStruct((n_rows, F), BF16),
        compiler_params=_cparams("arbitrary"),
        name="expert_gate_up",
    )(*_work_list(counts, tm, nj, n_tiles), xs, w_gu, b_gu, b_gu)

    tn = tn_down
    njd = D // tn
    return pl.pallas_call(
        _down_kernel,
        grid_spec=pltpu.PrefetchScalarGridSpec(
            num_scalar_prefetch=2,
            grid=(n_tiles * njd,),
            in_specs=[pl.BlockSpec((tm, F), row_tile),
                      pl.BlockSpec(memory_space=pl.ANY),
                      pl.BlockSpec((1, 1, tn), lambda w, it, n: (it[IT_EXPERT, w], 0, it[IT_COL, w]))],
            out_specs=pl.BlockSpec((tm, tn // 2), out_tile),
            scratch_shapes=[pltpu.VMEM((F, tn), F32), pltpu.VMEM((F, tn), BF16), pltpu.SemaphoreType.DMA]),
        out_shape=jax.ShapeDtypeStruct((n_rows, D // 2), jnp.uint32),
        compiler_params=_cparams("arbitrary"),
        name="expert_down",
    )(*_work_list(counts, tm, njd, n_tiles), act, w_dn, b_dn)


def _scatter_kernel(zero_ref, dest_ref, xa_ref, xb_ref, xs_ref, zbuf, sem, zsem, *, tiles_a, slot_tile):
    i = pl.program_id(0)
    tm = xa_ref.shape[0]

    @pl.when(i == 0)
    def _():
        zbuf[...] = jnp.zeros_like(zbuf)

        def tile_copy(z):
            start = pl.multiple_of(zero_ref[1 + z] * slot_tile, slot_tile)
            return pltpu.make_async_copy(zbuf, xs_ref.at[pl.ds(start, slot_tile), :], zsem)

        def start(z, carry):
            tile_copy(z).start()
            return carry

        def wait(z, carry):
            tile_copy(z).wait()
            return carry

        lax.fori_loop(0, zero_ref[0], start, 0)
        lax.fori_loop(0, zero_ref[0], wait, 0)

    def scatter_from(x_ref):
        def issue(r, carry):
            for k in range(TOP_K):
                pltpu.make_async_copy(x_ref.at[pl.ds(r, 1), :], xs_ref.at[pl.ds(dest_ref[r * TOP_K + k], 1), :],
                                      sem).start()
            return carry

        lax.fori_loop(0, tm, issue, 0, unroll=8)
        for _ in range(TOP_K):
            pltpu.make_async_copy(x_ref, xs_ref.at[pl.ds(0, tm), :], sem).wait()

    @pl.when(i < tiles_a)
    def _():
        scatter_from(xa_ref)

    @pl.when(i >= tiles_a)
    def _():
        scatter_from(xb_ref)


def _scatter_rows(xa, xb, dest, zero_list, n_rows, tm, slot_tile):
    Ta, half = xa.shape
    Tb = xb.shape[0]
    tiles_a, tiles_b = Ta // tm, Tb // tm
    return pl.pallas_call(
        functools.partial(_scatter_kernel, tiles_a=tiles_a, slot_tile=slot_tile),
        grid_spec=pltpu.PrefetchScalarGridSpec(
            num_scalar_prefetch=1,
            grid=(tiles_a + tiles_b,),
            in_specs=[pl.BlockSpec((tm * TOP_K,), lambda i, z: (i,), memory_space=pltpu.SMEM),
                      pl.BlockSpec((tm, half), lambda i, z: (jnp.minimum(i, tiles_a - 1), 0)),
                      pl.BlockSpec((tm, half), lambda i, z: (jnp.maximum(i - tiles_a, 0), 0))],
            out_specs=pl.BlockSpec(memory_space=pl.ANY),
            scratch_shapes=[pltpu.VMEM((slot_tile, half), jnp.uint32), pltpu.SemaphoreType.DMA,
                            pltpu.SemaphoreType.DMA]),
        out_shape=jax.ShapeDtypeStruct((n_rows, half), jnp.uint32),
        compiler_params=_cparams("arbitrary"),
        name="scatter_rows",
    )(zero_list, dest.reshape((Ta + Tb) * TOP_K), xa, xb)


COMBINE_PARTS = 2


def _combine_kernel(dest_ref, x2_ref, gate_ref, g_ref, out_ref, y_ref, buf, sems, *, col_tile):
    tm, D = x2_ref.shape
    hw = col_tile // 2

    def unpack(w):
        lo = pltpu.bitcast(w << 16, F32)
        hi = pltpu.bitcast(w & jnp.uint32(0xFFFF0000), F32)
        parts = []
        for j in range(D // col_tile):
            parts += [lo[:, j * hw:(j + 1) * hw], hi[:, j * hw:(j + 1) * hw]]
        return jnp.concatenate(parts, axis=1)

    rows_per_part = tm // COMBINE_PARTS

    def issue(r, part):
        for k in range(TOP_K):
            pltpu.make_async_copy(out_ref.at[pl.ds(dest_ref[r * TOP_K + k], 1), :], buf.at[k, pl.ds(r, 1), :],
                                  sems.at[part]).start()
        return part

    for part in range(COMBINE_PARTS):
        lax.fori_loop(part * rows_per_part, (part + 1) * rows_per_part, issue, part, unroll=8)
    for part in range(COMBINE_PARTS):
        rows = pl.ds(part * rows_per_part, rows_per_part)
        for k in range(TOP_K):
            pltpu.make_async_copy(out_ref.at[pl.ds(0, rows_per_part), :], buf.at[k, rows, :], sems.at[part]).wait()
        gate = gate_ref[rows, :]
        acc = x2_ref[rows, :]
        for k in range(TOP_K):
            acc = acc + gate[:, k:k + 1] * unpack(buf[k, rows, :])
        y_ref[rows, :] = _rms(acc, g_ref[...])


def _combine(x2, gate, dest, out, final_g, tm, col_tile):
    T, D = x2.shape
    row = lambda w: pl.BlockSpec((tm, w), lambda i: (i, 0))
    return pl.pallas_call(
        functools.partial(_combine_kernel, col_tile=col_tile),
        grid=(T // tm,),
        in_specs=[pl.BlockSpec((tm * TOP_K,), lambda i: (i,), memory_space=pltpu.SMEM),
                  row(D), row(LANES), _resident((1, D)), pl.BlockSpec(memory_space=pl.ANY)],
        out_specs=row(D),
        out_shape=jax.ShapeDtypeStruct((T, D), F32),
        scratch_shapes=[pltpu.VMEM((TOP_K, tm, D // 2), jnp.uint32), pltpu.SemaphoreType.DMA((COMBINE_PARTS,))],
        compiler_params=_cparams("arbitrary"),
        name="combine_norm",
    )(dest.reshape(T * TOP_K), x2, gate, final_g.reshape(1, D), out)


def _tile(n, pref):
    return pref if n % pref == 0 else n


def _mixers(x, past, lw):
    (norm_mix_g, w_in, b_igate, b_fgate, g_sb_out, g_ml_out) = lw
    B, S, D = x.shape
    T = B * S
    x2d = x.reshape(T, D)
    q, kf, kb, vf, vb, qm, km, vm, om, gt = _in_projection(x2d, norm_mix_g, w_in, b_igate, b_fgate, _tile(T, 256))
    if past is None:
        sb_o = _sb_prompt(q, kb, vb, g_sb_out, B, S, R=min(8, S // KEY_BLOCK))
        C0 = jnp.zeros((B, ML_HEADS, HEAD_DIM, ML_V_DIM), F32)
        n0 = jnp.zeros((B, ML_HEADS, HEAD_DIM), F32)
        m0 = jnp.zeros((B, ML_HEADS), F32)
        L = _tile(S, 256)
    else:
        cache_k, cache_v, C0, n0, m0 = past
        sb_o = _sb_sample(q, kb, vb, cache_k, cache_v, g_sb_out, B, S, cache_k.shape[1])
        L = S
    ml_o, C, n, m = _mlstm(qm, km, vm, om, gt, C0, n0, m0, g_ml_out, B, S, L)
    state = (kf.reshape(B, S, SB_HEADS, HEAD_DIM), vf.reshape(B, S, SB_HEADS, HEAD_DIM),
             C, n.reshape(B, ML_HEADS, HEAD_DIM), m.reshape(B, ML_HEADS))
    return (x2d, sb_o, ml_o), state


MOE_ROW_TILE = 512
MOE_COL_TILE = 1024
MOE_DOWN_COL_TILE = 2048
COMBINE_TILE = 256


def kernel(x_prompt, x_sample, cache_k, cache_v, state_C, state_n, state_m, norm_mix_g, w_in, b_igate, b_fgate,
           g_sb_out, g_ml_out, w_out, norm_ffn_g, w_router, b_router, w_gate_up, b_gate_up, w_down, b_down,
           final_norm_g):
    assert w_in.shape[0] == 1, "single-layer trunk"
    lw = (norm_mix_g[0], w_in[0], b_igate[0], b_fgate[0], g_sb_out[0], g_ml_out[0])
    E = w_gate_up.shape[1]
    D = x_prompt.shape[-1]
    groups = [_mixers(x_prompt, None, lw),
              _mixers(x_sample, (cache_k[0], cache_v[0], state_C[0], state_n[0], state_m[0]), lw)]

    counts = jnp.zeros((1, LANES), F32)
    routed = []
    for (x2d, sb_o, ml_o), _ in groups:
        x2, xn2, eid, gate, rank, counts = _outproj_router(x2d, sb_o, ml_o, w_out[0], norm_ffn_g[0], w_router[0],
                                                           b_router[0], counts, _tile(x2d.shape[0], 512))
        routed.append((x2, xn2, eid, gate, rank))
    tm = MOE_ROW_TILE
    cnt = counts[0, :N_EXPERTS].astype(I32)
    padded = (cnt + tm - 1) // tm * tm
    pstart = jnp.cumsum(padded) - padded
    n_assign = sum(r[0].shape[0] for r in routed) * TOP_K
    n_rows = (-(-n_assign // tm) + N_EXPERTS) * tm
    dests = [pstart[eid[:, :TOP_K]] + rank[:, :TOP_K] for (_, _, eid, _, rank) in routed]

    tpe = padded // tm
    n_tiles = n_rows // tm
    tile_ids = jnp.arange(n_tiles, dtype=I32)
    last_of_expert = jnp.any((tile_ids[:, None] == (jnp.cumsum(tpe) - 1)[None, :]) & (tpe[None, :] > 0), axis=1)
    needs_zero = last_of_expert | (tile_ids >= jnp.sum(tpe))
    zero_list = jnp.concatenate([jnp.sum(needs_zero.astype(I32))[None],
                                 jnp.nonzero(needs_zero, size=n_tiles, fill_value=0)[0].astype(I32)])
    xs = _scatter_rows(routed[0][1], routed[1][1], jnp.concatenate(dests, axis=0), zero_list, n_rows,
                       _tile(routed[1][1].shape[0], 512), tm)
    out = _expert_ffn(xs, cnt, w_gate_up[0], b_gate_up[0].reshape(E, 1, -1), w_down[0],
                      b_down[0].reshape(E, 1, -1), tm, MOE_COL_TILE, MOE_DOWN_COL_TILE)
    ys = [_combine(x2, gate, dest, out, final_norm_g, _tile(x2.shape[0], COMBINE_TILE), MOE_DOWN_COL_TILE)
          for (x2, _, _, gate, _), dest in zip(routed, dests)]

    (kp, vp, Cp, np_, mp), (ks, vs, Cs, ns, ms) = groups[0][1], groups[1][1]
    return (ys[0].reshape(x_prompt.shape), ys[1].reshape(x_sample.shape),
            kp[None], vp[None], Cp[None], np_[None], mp[None], ks[None], vs[None], Cs[None], ns[None], ms[None])
```

```python
import functools

import jax
import jax.numpy as jnp
from jax import lax
from jax.experimental import pallas as pl
from jax.experimental.pallas import tpu as pltpu

F32 = jnp.float32
BF16 = jnp.bfloat16
I32 = jnp.int32

EPS = 1e-6
SB_HEADS = 8
HEAD_DIM = 128
ML_HEADS = 4
ML_V_DIM = 256
N_EXPERTS = 32
TOP_K = 4
SWIGLU_ALPHA = 1.702
SWIGLU_LIMIT = 7.0
LANES = 128
KEY_BLOCK = 128
NEG_BIG = -1e30
EXP_ZERO_BELOW = -105.0
VMEM_LIMIT = 56 * 1024 * 1024


def _cparams(*sem, vmem_limit=VMEM_LIMIT):
    return pltpu.CompilerParams(dimension_semantics=sem, vmem_limit_bytes=vmem_limit)


def _resident(shape):
    nd = len(shape)
    return pl.BlockSpec(shape, lambda *_: (0,) * nd, pipeline_mode=pl.Buffered(1))


def _rms(x, g):
    return x * lax.rsqrt(jnp.mean(x * x, axis=-1, keepdims=True) + EPS) * g


def _log_sigmoid(z):
    return jnp.minimum(z, 0.0) - jnp.log(1.0 + jnp.exp(-jnp.abs(z)))


def _split2(x):
    hi = x.astype(BF16)
    lo = (x - hi.astype(F32)).astype(BF16)
    return hi, lo


def _split3(x):
    h1 = x.astype(BF16)
    r = x - h1.astype(F32)
    h2 = r.astype(BF16)
    h3 = (r - h2.astype(F32)).astype(BF16)
    return h1, h2, h3


def _mm(a, b):
    return jnp.dot(a, b, preferred_element_type=F32)


def _inproj_kernel(x_ref, g_ref, wq_ref, wk_ref, wv_ref, wqm_ref, wkm_ref, wvm_ref, wom_ref,
                   wg_ref, bg_ref,
                   q_ref, kf_ref, kb_ref, vf_ref, vb_ref, qm_ref, km_ref, vm_ref, om_ref, gt_ref):
    xn = _rms(x_ref[...], g_ref[...]).astype(BF16)
    q_ref[...] = (_mm(xn, wq_ref[...]) * (HEAD_DIM ** -0.5)).astype(BF16)
    k = _mm(xn, wk_ref[...])
    kf_ref[...] = k
    kb_ref[...] = k.astype(BF16)
    v = _mm(xn, wv_ref[...])
    vf_ref[...] = v
    vb_ref[...] = v.astype(BF16)
    qm_ref[...] = _mm(xn, wqm_ref[...]).astype(BF16)
    km_ref[...] = (_mm(xn, wkm_ref[...]) * (HEAD_DIM ** -0.5)).astype(BF16)
    vm_ref[...] = _mm(xn, wvm_ref[...]).astype(BF16)
    om_ref[...] = _mm(xn, wom_ref[...])
    gpre = _mm(xn, wg_ref[...]) + bg_ref[...]
    lane = lax.broadcasted_iota(I32, gpre.shape, 1)
    is_f = (lane >= ML_HEADS) & (lane < 2 * ML_HEADS)
    gt_ref[...] = jnp.where(is_f, _log_sigmoid(gpre), gpre)


def _in_projection(x2d, norm_g, w_in, b_igate, b_fgate, tm):
    T, D = x2d.shape
    sbw = SB_HEADS * HEAD_DIM
    mqk = ML_HEADS * HEAD_DIM
    mlw = ML_HEADS * ML_V_DIM
    w_bf = w_in.astype(BF16)
    o = 0
    w_specs = []
    for width in (sbw, sbw, sbw, mqk, mqk, mlw, mlw):
        assert o % width == 0
        w_specs.append(pl.BlockSpec((D, width), functools.partial(lambda blk, i: (0, blk), o // width),
                                    pipeline_mode=pl.Buffered(1)))
        o += width
    wg = jnp.zeros((D, LANES), F32).at[:, :2 * ML_HEADS].set(w_in[:, o:o + 2 * ML_HEADS]).astype(BF16)
    bg = jnp.zeros((1, LANES), F32).at[0, :ML_HEADS].set(b_igate).at[0, ML_HEADS:2 * ML_HEADS].set(b_fgate)
    row = lambda w: pl.BlockSpec((tm, w), lambda i: (i, 0))
    out_widths = (sbw, sbw, sbw, sbw, sbw, mqk, mqk, mlw, mlw, LANES)
    out_dtypes = (BF16, F32, BF16, F32, BF16, BF16, BF16, BF16, F32, F32)
    return pl.pallas_call(
        _inproj_kernel,
        grid=(T // tm,),
        in_specs=[row(D), _resident((1, D))] + w_specs + [_resident(wg.shape), _resident(bg.shape)],
        out_specs=[row(w) for w in out_widths],
        out_shape=[jax.ShapeDtypeStruct((T, w), dt) for w, dt in zip(out_widths, out_dtypes)],
        compiler_params=_cparams("parallel"),
        name="in_projection",
    )(x2d, norm_g.reshape(1, D), *([w_bf] * len(w_specs)), wg, bg)


def _suffix_matrix():
    j = lax.broadcasted_iota(I32, (KEY_BLOCK, 2 * KEY_BLOCK), 0)
    c = lax.broadcasted_iota(I32, (KEY_BLOCK, 2 * KEY_BLOCK), 1)
    return jnp.where((c >= KEY_BLOCK) | (j > c), 1.0, 0.0).astype(BF16)


def _sb_step(q, k, v, carry, umat, mask):
    R, bq, _ = q.shape
    z = jnp.einsum("rqd,rkd->rqk", q, k, preferred_element_type=F32)
    lp = jnp.log(1.0 + jnp.exp(-jnp.abs(z)))
    log_beta = jnp.minimum(z, 0.0) - lp
    log_stay = log_beta - z
    if mask is not None:
        log_stay = jnp.where(mask, log_stay, 0.0)
    hi, lo = _split2(log_stay)
    st = _mm(hi.reshape(R * bq, KEY_BLOCK), umat) + _mm(lo.reshape(R * bq, KEY_BLOCK), umat)
    st = st.reshape(R, bq, 2 * KEY_BLOCK)
    w = jnp.exp(log_beta + st[:, :, :KEY_BLOCK] + carry)
    if mask is not None:
        w = jnp.where(mask, w, 0.0)
    pv = jnp.einsum("rqk,rkd->rqd", w.astype(BF16), v, preferred_element_type=F32)
    return pv, carry + st[:, :, KEY_BLOCK:]


def _sb_prompt_kernel(q_ref, k_ref, v_ref, g_ref, o_ref, acc_ref, carry_ref, *, R):
    qi = pl.program_id(2)
    blk0 = qi * R
    q = q_ref[...].reshape(R, KEY_BLOCK, HEAD_DIM)
    umat = _suffix_matrix()
    shape3 = (R, KEY_BLOCK, KEY_BLOCK)
    t_io = lax.broadcasted_iota(I32, shape3, 1)
    s_io = lax.broadcasted_iota(I32, shape3, 2)
    r_io = lax.broadcasted_iota(I32, shape3, 0)

    def load(ref, d):
        return jnp.stack([ref[pl.ds(pl.multiple_of(jnp.maximum(blk0 + r - d, 0) * KEY_BLOCK, KEY_BLOCK),
                                    KEY_BLOCK), :] for r in range(R)])

    def penalty(d_next):
        return jnp.where(r_io < d_next - blk0, NEG_BIG, 0.0)

    pv, carry = _sb_step(q, load(k_ref, 0), load(v_ref, 0), jnp.zeros(shape3, F32), umat, s_io < t_io)
    acc_ref[...] = pv
    carry = carry + penalty(1)
    carry_ref[...] = carry

    def cond(state):
        d, mx = state
        return (d < blk0 + R) & (mx > EXP_ZERO_BELOW)

    def body(state):
        d, _ = state
        pv, carry = _sb_step(q, load(k_ref, d), load(v_ref, d), carry_ref[...], umat, None)
        acc_ref[...] += pv
        carry = carry + penalty(d + 1)
        carry_ref[...] = carry
        return d + 1, jnp.max(carry)

    lax.while_loop(cond, body, (jnp.int32(1), jnp.max(carry)))
    a = acc_ref[...]
    out = a * lax.rsqrt(jnp.mean(a * a, axis=-1, keepdims=True) + EPS) * g_ref[...]
    o_ref[...] = out.reshape(R * KEY_BLOCK, HEAD_DIM).astype(BF16)


def _sb_prompt(q, k, v, g_sb, B, S, R):
    tq = R * KEY_BLOCK
    nq = S // tq
    return pl.pallas_call(
        functools.partial(_sb_prompt_kernel, R=R),
        grid=(B, SB_HEADS, nq),
        in_specs=[pl.BlockSpec((tq, HEAD_DIM), lambda b, h, i: (b * nq + i, h)),
                  pl.BlockSpec((S, HEAD_DIM), lambda b, h, i: (b, h)),
                  pl.BlockSpec((S, HEAD_DIM), lambda b, h, i: (b, h)),
                  pl.BlockSpec((1, HEAD_DIM), lambda b, h, i: (0, h))],
        out_specs=pl.BlockSpec((tq, HEAD_DIM), lambda b, h, i: (b * nq + i, h)),
        out_shape=jax.ShapeDtypeStruct((B * S, SB_HEADS * HEAD_DIM), BF16),
        scratch_shapes=[pltpu.VMEM((R, KEY_BLOCK, HEAD_DIM), F32),
                        pltpu.VMEM((R, KEY_BLOCK, KEY_BLOCK), F32)],
        compiler_params=_cparams("parallel", "parallel", "parallel"),
        name="sb_prompt",
    )(q, k, v, g_sb.reshape(1, -1))


def _sb_sample_kernel(q_ref, kn_ref, vn_ref, ck_ref, cv_ref, g_ref, o_ref, acc_ref, carry_ref, *, S, P):
    H = SB_HEADS
    hs = lambda h: slice(h * HEAD_DIM, (h + 1) * HEAD_DIM)
    umat = _suffix_matrix()
    q = jnp.stack([q_ref[:, hs(h)] for h in range(H)])
    pad = jnp.zeros((KEY_BLOCK - S, HEAD_DIM), BF16)

    def new_keys(ref):
        return jnp.stack([jnp.concatenate([ref[:, hs(h)], pad], axis=0) for h in range(H)])

    def past_keys(ref, j):
        start = pl.multiple_of(j * KEY_BLOCK * H, KEY_BLOCK * H)
        return jnp.stack([ref[pl.ds(start + h, KEY_BLOCK, stride=H), :].astype(BF16) for h in range(H)])

    shape3 = (H, S, KEY_BLOCK)
    t_io = lax.broadcasted_iota(I32, shape3, 1)
    s_io = lax.broadcasted_iota(I32, shape3, 2)
    pv, carry = _sb_step(q, new_keys(kn_ref), new_keys(vn_ref), jnp.zeros(shape3, F32), umat, s_io < t_io)
    acc_ref[...] = pv
    carry_ref[...] = carry

    def cond(state):
        j, mx = state
        return (j >= 0) & (mx > EXP_ZERO_BELOW)

    def body(state):
        j, _ = state
        pv, carry = _sb_step(q, past_keys(ck_ref, j), past_keys(cv_ref, j), carry_ref[...], umat, None)
        acc_ref[...] += pv
        carry_ref[...] = carry
        return j - 1, jnp.max(carry)

    lax.while_loop(cond, body, (jnp.int32(P // KEY_BLOCK - 1), jnp.max(carry)))
    a = acc_ref[...]
    a = a * lax.rsqrt(jnp.mean(a * a, axis=-1, keepdims=True) + EPS)
    for h in range(H):
        o_ref[:, hs(h)] = (a[h] * g_ref[:, hs(h)]).astype(BF16)


def _sb_sample(q, kn, vn, cache_k, cache_v, g_sb, B, S, P):
    W = SB_HEADS * HEAD_DIM
    row = pl.BlockSpec((S, W), lambda b: (b, 0))
    past = pl.BlockSpec((P * SB_HEADS, HEAD_DIM), lambda b: (b, 0))
    return pl.pallas_call(
        functools.partial(_sb_sample_kernel, S=S, P=P),
        grid=(B,),
        in_specs=[row, row, row, past, past, pl.BlockSpec((1, W), lambda b: (0, 0))],
        out_specs=row,
        out_shape=jax.ShapeDtypeStruct((B * S, W), BF16),
        scratch_shapes=[pltpu.VMEM((SB_HEADS, S, HEAD_DIM), F32),
                        pltpu.VMEM((SB_HEADS, S, KEY_BLOCK), F32)],
        compiler_params=_cparams("parallel"),
        name="sb_sample",
    )(q, kn, vn, cache_k.reshape(B * P * SB_HEADS, HEAD_DIM), cache_v.reshape(B * P * SB_HEADS, HEAD_DIM),
      g_sb.reshape(1, W))


def _mlstm_kernel(q_ref, k_ref, v_ref, o_ref, gt_ref, c0_ref, n0_ref, m0_ref, g_ref,
                  out_ref, c_out_ref, n_out_ref, m_out_ref, cext_ref, m_ref, *, L, Lp):
    c = pl.program_id(1)
    H = ML_HEADS
    VW = ML_V_DIM + LANES
    lane_row = lax.broadcasted_iota(I32, (1, LANES), 1)
    onehot0 = jnp.where(lane_row == 0, 1.0, 0.0)

    @pl.when(c == 0)
    def _():
        for h in range(H):
            cext_ref[h] = jnp.concatenate([c0_ref[0, h], n0_ref[0, h] * onehot0], axis=1)
            m_ref[h] = jnp.broadcast_to(m0_ref[0, h], (8, LANES))

    def pad_rows(a, fill=0.0):
        if Lp == L:
            return a
        return jnp.concatenate([a, jnp.full((Lp - L, a.shape[1]), fill, a.dtype)], axis=0)

    gt = gt_ref[...]
    lane = lax.broadcasted_iota(I32, (Lp, LANES), 1)
    if Lp != L:
        gt = jnp.concatenate([gt, jnp.broadcast_to(jnp.where(lane_row < H, NEG_BIG, 0.0), (Lp - L, LANES))], axis=0)
    lf = jnp.where((lane >= H) & (lane < 2 * H), gt, 0.0)
    ti = lax.broadcasted_iota(I32, (Lp, Lp), 0)
    si = lax.broadcasted_iota(I32, (Lp, Lp), 1)
    causal = si <= ti
    tri = jnp.where(causal, 1.0, 0.0).astype(BF16)
    bc = sum(_mm(tri, p) for p in _split3(lf))
    bc_t = bc.T
    gt_t = gt.T
    ones_blk = jnp.broadcast_to(onehot0, (Lp, LANES)).astype(BF16)

    for h in range(H):
        b_col = bc[:, H + h:H + h + 1]
        b_row = bc_t[H + h:H + h + 1, :]
        i_col = gt[:, h:h + 1]
        i_row = gt_t[h:h + 1, :]
        m_prev = m_ref[h][0:1, 0:1]
        qh = pad_rows(q_ref[:, h * HEAD_DIM:(h + 1) * HEAD_DIM])
        kh = pad_rows(k_ref[:, h * HEAD_DIM:(h + 1) * HEAD_DIM])
        vh = pad_rows(v_ref[:, h * ML_V_DIM:(h + 1) * ML_V_DIM])
        vext = jnp.concatenate([vh, ones_blk], axis=1)
        cext = cext_ref[h]

        log_d = jnp.where(causal, b_col - b_row + i_row, NEG_BIG)
        log_inter = b_col + m_prev
        m_row = jnp.maximum(log_inter, jnp.max(log_d, axis=1, keepdims=True))
        dmat = jnp.exp(log_d - m_row)
        s = lax.dot_general(qh, kh, (((1,), (1,)), ((), ())), preferred_element_type=F32) * dmat
        inter = jnp.exp(log_inter - m_row)
        num = _mm(s.astype(BF16), vext) + inter * _mm(qh, cext.astype(BF16))
        den = num[:, ML_V_DIM:ML_V_DIM + 1]
        hh = num[:L, :ML_V_DIM] / jnp.maximum(jnp.abs(den), jnp.exp(-m_row))[:L]
        hn = hh * lax.rsqrt(jnp.mean(hh * hh, axis=-1, keepdims=True) + EPS)
        cols = slice(h * ML_V_DIM, (h + 1) * ML_V_DIM)
        ogate = 1.0 / (1.0 + jnp.exp(-o_ref[:, cols]))
        out_ref[:, cols] = (ogate * (hn * g_ref[:, cols])).astype(BF16)

        b_last = b_col[Lp - 1:Lp, :]
        log_w = b_last - b_col + i_col
        m_new = jnp.maximum(b_last + m_prev, jnp.max(log_w, axis=0, keepdims=True))
        wk = jnp.exp(log_w - m_new)
        decay = jnp.exp(b_last + m_prev - m_new)
        upd = lax.dot_general(kh, (wk * vext.astype(F32)).astype(BF16), (((0,), (0,)), ((), ())),
                              preferred_element_type=F32)
        cnew = decay * cext + upd
        cext_ref[h] = cnew
        m_ref[h] = jnp.broadcast_to(m_new, (8, LANES))

    @pl.when(c == pl.num_programs(1) - 1)
    def _():
        for h in range(H):
            cf = cext_ref[h]
            c_out_ref[0, h] = cf[:, :ML_V_DIM]
            n_out_ref[0, h] = cf[:, ML_V_DIM:ML_V_DIM + 1]
            m_out_ref[0, h] = m_ref[h][0:1, 0:1]


def _mlstm(qm, km, vm, om, gt, C0, n0, m0, g_ml, B, S, L):
    nc = S // L
    Lp = max(L, LANES)
    H = ML_HEADS
    VW = ML_V_DIM + LANES
    row = lambda w: pl.BlockSpec((L, w), lambda b, c: (b * nc + c, 0))
    st = lambda *tail: pl.BlockSpec((1, H) + tail, lambda b, c: (b, 0, 0, 0))
    return pl.pallas_call(
        functools.partial(_mlstm_kernel, L=L, Lp=Lp),
        grid=(B, nc),
        in_specs=[row(H * HEAD_DIM), row(H * HEAD_DIM), row(H * ML_V_DIM), row(H * ML_V_DIM), row(LANES),
                  st(HEAD_DIM, ML_V_DIM), st(HEAD_DIM, 1), st(1, 1),
                  pl.BlockSpec((1, H * ML_V_DIM), lambda b, c: (0, 0))],
        out_specs=[row(H * ML_V_DIM), st(HEAD_DIM, ML_V_DIM), st(HEAD_DIM, 1), st(1, 1)],
        out_shape=[jax.ShapeDtypeStruct((B * S, H * ML_V_DIM), BF16),
                   jax.ShapeDtypeStruct((B, H, HEAD_DIM, ML_V_DIM), F32),
                   jax.ShapeDtypeStruct((B, H, HEAD_DIM, 1), F32),
                   jax.ShapeDtypeStruct((B, H, 1, 1), F32)],
        scratch_shapes=[pltpu.VMEM((H, HEAD_DIM, VW), F32), pltpu.VMEM((H, 8, LANES), F32)],
        compiler_params=_cparams("parallel", "arbitrary"),
        name="mlstm",
    )(qm, km, vm, om, gt, C0, n0.reshape(B, H, HEAD_DIM, 1), m0.reshape(B, H, 1, 1), g_ml.reshape(1, -1))


def _pack_halves(lo, hi):
    lo_bits = pltpu.bitcast(lo.astype(BF16).astype(F32), jnp.uint32) >> 16
    hi_bits = pltpu.bitcast(hi.astype(BF16).astype(F32), jnp.uint32) & jnp.uint32(0xFFFF0000)
    return lo_bits | hi_bits


def _unpack_halves(w):
    lo = pltpu.bitcast(w << 16, F32).astype(BF16)
    hi = pltpu.bitcast(w & jnp.uint32(0xFFFF0000), F32).astype(BF16)
    return lo, hi


def _outproj_router_kernel(x_ref, sb_ref, ml_ref, wos_ref, wom_ref, g_ref, wr_hi_ref, wr_lo_ref, br_ref, c0_ref,
                           x2_ref, xn_ref, eid_ref, gate_ref, rank_ref, cnt_ref, carry_ref):
    i = pl.program_id(0)

    @pl.when(i == 0)
    def _():
        carry_ref[...] = c0_ref[...]

    x2 = x_ref[...] + _mm(sb_ref[...], wos_ref[...]) + _mm(ml_ref[...], wom_ref[...])
    x2_ref[...] = x2
    xn = _rms(x2, g_ref[...])
    half = xn.shape[1] // 2
    xn_ref[...] = _pack_halves(xn[:, :half], xn[:, half:])
    hi, lo = _split2(xn)
    logits = _mm(hi, wr_hi_ref[...]) + _mm(lo, wr_hi_ref[...]) + _mm(hi, wr_lo_ref[...]) + br_ref[...]
    tm = logits.shape[0]
    lane = lax.broadcasted_iota(I32, (tm, LANES), 1)
    vals, ids = [], []
    cnt = jnp.zeros((tm, LANES), F32)
    for _ in range(TOP_K):
        mx = jnp.max(logits, axis=-1, keepdims=True)
        idx = jnp.min(jnp.where(logits == mx, lane, LANES), axis=-1, keepdims=True)
        sel = lane == idx
        vals.append(mx)
        ids.append(idx)
        logits = jnp.where(sel, -jnp.inf, logits)
        cnt = cnt + jnp.where(sel, 1.0, 0.0)
    es = [jnp.exp(v - vals[0]) for v in vals]
    inv = 1.0 / sum(es)
    ti = lax.broadcasted_iota(I32, (tm, tm), 0)
    si = lax.broadcasted_iota(I32, (tm, tm), 1)
    before = _mm(jnp.where(si < ti, 1.0, 0.0).astype(BF16), cnt.astype(BF16)) + carry_ref[...]
    eid_o = jnp.zeros((tm, LANES), I32)
    gate_o = jnp.zeros((tm, LANES), F32)
    rank_o = jnp.zeros((tm, LANES), I32)
    for k in range(TOP_K):
        rk = jnp.sum(jnp.where(lane == ids[k], before, 0.0), axis=-1, keepdims=True)
        eid_o = jnp.where(lane == k, ids[k], eid_o)
        gate_o = jnp.where(lane == k, es[k] * inv, gate_o)
        rank_o = jnp.where(lane == k, rk.astype(I32), rank_o)
    eid_ref[...] = eid_o
    gate_ref[...] = gate_o
    rank_ref[...] = rank_o
    carry_ref[...] += jnp.sum(cnt, axis=0, keepdims=True)
    cnt_ref[...] = carry_ref[...]


def _outproj_router(x2d, sb_o, ml_o, w_out, norm_g, w_router, b_router, counts_in, tm):
    T, D = x2d.shape
    W = sb_o.shape[1]
    assert w_out.shape[0] == 2 * W
    w_bf = w_out.astype(BF16)
    half_w = lambda blk: pl.BlockSpec((W, D), lambda i: (blk, 0), pipeline_mode=pl.Buffered(1))
    wr = jnp.zeros((D, LANES), F32).at[:, :N_EXPERTS].set(w_router)
    wr_hi = wr.astype(BF16)
    wr_lo = (wr - wr_hi.astype(F32)).astype(BF16)
    br = jnp.full((1, LANES), NEG_BIG, F32).at[0, :N_EXPERTS].set(b_router)
    row = lambda w: pl.BlockSpec((tm, w), lambda i: (i, 0))
    return pl.pallas_call(
        _outproj_router_kernel,
        grid=(T // tm,),
        in_specs=[row(D), row(W), row(W), half_w(0), half_w(1),
                  _resident((1, D)), _resident(wr_hi.shape), _resident(wr_lo.shape), _resident(br.shape),
                  _resident((1, LANES))],
        out_specs=[row(D), row(D // 2), row(LANES), row(LANES), row(LANES),
                   pl.BlockSpec((1, LANES), lambda i: (0, 0))],
        out_shape=[jax.ShapeDtypeStruct((T, D), F32), jax.ShapeDtypeStruct((T, D // 2), jnp.uint32),
                   jax.ShapeDtypeStruct((T, LANES), I32), jax.ShapeDtypeStruct((T, LANES), F32),
                   jax.ShapeDtypeStruct((T, LANES), I32), jax.ShapeDtypeStruct((1, LANES), F32)],
        scratch_shapes=[pltpu.VMEM((1, LANES), F32)],
        compiler_params=_cparams("arbitrary"),
        name="outproj_router",
    )(x2d, sb_o, ml_o, w_bf, w_bf, norm_g.reshape(1, D), wr_hi, wr_lo, br, counts_in)


ROW_BLOCK = 128
IT_TILE, IT_COL, IT_EXPERT, IT_FIRST, IT_BLOCKS, IT_NEXT_EXPERT, IT_NEXT_COL = range(7)


def _segment_weights(it_ref, w, copies, cast):
    @pl.when(it_ref[IT_FIRST, w] == 1)
    def _():
        @pl.when(w == 0)
        def _():
            for c in copies(it_ref[IT_EXPERT, w], it_ref[IT_COL, w]):
                c.start()

        for c in copies(it_ref[IT_EXPERT, w], it_ref[IT_COL, w]):
            c.wait()
        cast()

        @pl.when(it_ref[IT_NEXT_EXPERT, w] >= 0)
        def _():
            for c in copies(it_ref[IT_NEXT_EXPERT, w], it_ref[IT_NEXT_COL, w]):
                c.start()


def _for_filled_rows(blocks, tm, compute):
    for lvl in range(1, tm // ROW_BLOCK + 1):
        @pl.when(blocks == lvl)
        def _():
            compute(lvl * ROW_BLOCK)


def _gate_up_kernel(it_ref, n_ref, x_ref, w_hbm, bg_ref, bl_ref, act_ref, stage, wg_s, wl_s, x_s, sems, *, nj):
    w = pl.program_id(0)
    tm, half = x_ref.shape
    tn = act_ref.shape[1]

    def copies(e, j):
        return [pltpu.make_async_copy(w_hbm.at[e, :, pl.ds(pl.multiple_of((h * nj + j) * tn, tn), tn)],
                                      stage.at[h], sems.at[h]) for h in range(2)]

    def cast():
        wg_s[...] = stage[0].astype(BF16)
        wl_s[...] = stage[1].astype(BF16)

    @pl.when(w < n_ref[0])
    def _():
        _segment_weights(it_ref, w, copies, cast)

        def compute(rows):
            lo, hi = _unpack_halves(x_ref[:rows, :])
            x_s[:rows, :half] = lo
            x_s[:rows, half:] = hi
            x = x_s[:rows, :]
            glu = jnp.minimum(_mm(x, wg_s[...]) + bg_ref[0], SWIGLU_LIMIT)
            lin = jnp.clip(_mm(x, wl_s[...]) + bl_ref[0], -SWIGLU_LIMIT, SWIGLU_LIMIT)
            act = glu * (1.0 / (1.0 + jnp.exp(-SWIGLU_ALPHA * glu))) * (lin + 1.0)
            act_ref[:rows, :] = act.astype(BF16)
            if rows < tm:
                act_ref[rows:, :] = jnp.zeros((tm - rows, tn), BF16)

        _for_filled_rows(it_ref[IT_BLOCKS, w], tm, compute)

    @pl.when(w >= n_ref[0])
    def _():
        act_ref[...] = jnp.zeros_like(act_ref)


def _down_kernel(it_ref, n_ref, a_ref, w_hbm, b_ref, o_ref, stage, w_s, sem):
    w = pl.program_id(0)
    tm = a_ref.shape[0]
    tn = w_s.shape[1]

    def copies(e, j):
        return [pltpu.make_async_copy(w_hbm.at[e, :, pl.ds(pl.multiple_of(j * tn, tn), tn)], stage, sem)]

    def cast():
        w_s[...] = stage[...].astype(BF16)

    @pl.when(w < n_ref[0])
    def _():
        _segment_weights(it_ref, w, copies, cast)

        def compute(rows):
            out = _mm(a_ref[:rows, :], w_s[...]) + b_ref[0]
            o_ref[:rows, :] = _pack_halves(out[:, :tn // 2], out[:, tn // 2:])
            if rows < tm:
                o_ref[rows:, :] = jnp.zeros((tm - rows, tn // 2), jnp.uint32)

        _for_filled_rows(it_ref[IT_BLOCKS, w], tm, compute)

    @pl.when(w >= n_ref[0])
    def _():
        o_ref[...] = jnp.zeros_like(o_ref)


def _work_list(counts, tm, nj, n_tiles_max):
    tpe = (counts + tm - 1) // tm
    tile_start = jnp.cumsum(tpe) - tpe
    item_end = jnp.cumsum(tpe * nj)
    item_start = item_end - tpe * nj
    n_items = item_end[-1:]
    w = jnp.arange(n_tiles_max * nj, dtype=I32)
    wc = jnp.minimum(w, n_items[0] - 1)
    e = jnp.minimum(jnp.sum((wc[:, None] >= item_end[None, :]).astype(I32), axis=1), N_EXPERTS - 1)
    ids = jnp.where(tpe > 0, jnp.arange(N_EXPERTS, dtype=I32), N_EXPERTS)
    later = jnp.concatenate([lax.cummin(ids[::-1])[::-1][1:], jnp.full((1,), N_EXPERTS, I32)])
    next_nonempty = jnp.where(later < N_EXPERTS, later, -1)
    per_expert = jnp.stack([item_start, tpe, tile_start, counts, next_nonempty], axis=1).astype(F32)
    onehot = (e[:, None] == jnp.arange(N_EXPERTS, dtype=I32)[None, :]).astype(F32)
    looked = jnp.dot(onehot, per_expert, precision=lax.Precision.HIGHEST).astype(I32)
    item_start_e, tpe_e, tile_start_e, counts_e, next_nonempty_e = (looked[:, c] for c in range(5))
    local = wc - item_start_e
    t = jnp.maximum(tpe_e, 1)
    j = local // t
    il = local - j * t
    real = w < n_items[0]
    first = ((il == 0) & real).astype(I32)
    tail = jnp.maximum(w - n_items[0], 0)
    tile = jnp.where(real, tile_start_e + il, jnp.sum(tpe) + tail // nj)
    j = jnp.where(real, j, tail % nj)
    blocks = (jnp.clip(counts_e - il * tm, 1, tm) + ROW_BLOCK - 1) // ROW_BLOCK
    last_col = j + 1 >= nj
    next_e = jnp.where(last_col, next_nonempty_e, e)
    next_j = jnp.where(last_col, 0, j + 1)
    table = jnp.stack([tile, j, e, first, blocks, next_e, next_j]).astype(I32)
    return table, n_items.astype(I32)


def _expert_ffn(xs, counts, w_gu, b_gu, w_dn, b_dn, tm, tn, tn_down):
    n_rows, half = xs.shape
    D = 2 * half
    F = w_dn.shape[1]
    n_tiles = n_rows // tm
    nj = F // tn
    row_tile = lambda w, it, n: (it[IT_TILE, w], 0)
    out_tile = lambda w, it, n: (it[IT_TILE, w], it[IT_COL, w])
    act = pl.pallas_call(
        functools.partial(_gate_up_kernel, nj=nj),
        grid_spec=pltpu.PrefetchScalarGridSpec(
            num_scalar_prefetch=2,
            grid=(n_tiles * nj,),
            in_specs=[pl.BlockSpec((tm, half), row_tile),
                      pl.BlockSpec(memory_space=pl.ANY),
                      pl.BlockSpec((1, 1, tn), lambda w, it, n: (it[IT_EXPERT, w], 0, it[IT_COL, w])),
                      pl.BlockSpec((1, 1, tn), lambda w, it, n: (it[IT_EXPERT, w], 0, nj + it[IT_COL, w]))],
            out_specs=pl.BlockSpec((tm, tn), out_tile),
            scratch_shapes=[pltpu.VMEM((2, D, tn), F32), pltpu.VMEM((D, tn), BF16), pltpu.VMEM((D, tn), BF16),
                            pltpu.VMEM((tm, D), BF16), pltpu.SemaphoreType.DMA((2,))]),
        out_shape=jax.ShapeDtypeStruct((n_rows, F), BF16),
        compiler_params=_cparams("arbitrary"),
        name="expert_gate_up",
    )(*_work_list(counts, tm, nj, n_tiles), xs, w_gu, b_gu, b_gu)

    tn = tn_down
    njd = D // tn
    return pl.pallas_call(
        _down_kernel,
        grid_spec=pltpu.PrefetchScalarGridSpec(
            num_scalar_prefetch=2,
            grid=(n_tiles * njd,),
            in_specs=[pl.BlockSpec((tm, F), row_tile),
                      pl.BlockSpec(memory_space=pl.ANY),
                      pl.BlockSpec((1, 1, tn), lambda w, it, n: (it[IT_EXPERT, w], 0, it[IT_COL, w]))],
            out_specs=pl.BlockSpec((tm, tn // 2), out_tile),
            scratch_shapes=[pltpu.VMEM((F, tn), F32), pltpu.VMEM((F, tn), BF16), pltpu.SemaphoreType.DMA]),
        out_shape=jax.ShapeDtypeStruct((n_rows, D // 2), jnp.uint32),
        compiler_params=_cparams("arbitrary"),
        name="expert_down",
    )(*_work_list(counts, tm, njd, n_tiles), act, w_dn, b_dn)


def _scatter_kernel(zero_ref, dest_ref, xa_ref, xb_ref, xs_ref, zbuf, sem, zsem, *, tiles_a, slot_tile):
    i = pl.program_id(0)
    tm = xa_ref.shape[0]

    @pl.when(i == 0)
    def _():
        zbuf[...] = jnp.zeros_like(zbuf)

        def tile_copy(z):
            start = pl.multiple_of(zero_ref[1 + z] * slot_tile, slot_tile)
            return pltpu.make_async_copy(zbuf, xs_ref.at[pl.ds(start, slot_tile), :], zsem)

        def start(z, carry):
            tile_copy(z).start()
            return carry

        def wait(z, carry):
            tile_copy(z).wait()
            return carry

        lax.fori_loop(0, zero_ref[0], start, 0)
        lax.fori_loop(0, zero_ref[0], wait, 0)

    def scatter_from(x_ref):
        def issue(r, carry):
            for k in range(TOP_K):
                pltpu.make_async_copy(x_ref.at[pl.ds(r, 1), :], xs_ref.at[pl.ds(dest_ref[r * TOP_K + k], 1), :],
                                      sem).start()
            return carry

        lax.fori_loop(0, tm, issue, 0, unroll=8)
        for _ in range(TOP_K):
            pltpu.make_async_copy(x_ref, xs_ref.at[pl.ds(0, tm), :], sem).wait()

    @pl.when(i < tiles_a)
    def _():
        scatter_from(xa_ref)

    @pl.when(i >= tiles_a)
    def _():
        scatter_from(xb_ref)


def _scatter_rows(xa, xb, dest, zero_list, n_rows, tm, slot_tile):
    Ta, half = xa.shape
    Tb = xb.shape[0]
    tiles_a, tiles_b = Ta // tm, Tb // tm
    return pl.pallas_call(
        functools.partial(_scatter_kernel, tiles_a=tiles_a, slot_tile=slot_tile),
        grid_spec=pltpu.PrefetchScalarGridSpec(
            num_scalar_prefetch=1,
            grid=(tiles_a + tiles_b,),
            in_specs=[pl.BlockSpec((tm * TOP_K,), lambda i, z: (i,), memory_space=pltpu.SMEM),
                      pl.BlockSpec((tm, half), lambda i, z: (jnp.minimum(i, tiles_a - 1), 0)),
                      pl.BlockSpec((tm, half), lambda i, z: (jnp.maximum(i - tiles_a, 0), 0))],
            out_specs=pl.BlockSpec(memory_space=pl.ANY),
            scratch_shapes=[pltpu.VMEM((slot_tile, half), jnp.uint32), pltpu.SemaphoreType.DMA,
                            pltpu.SemaphoreType.DMA]),
        out_shape=jax.ShapeDtypeStruct((n_rows, half), jnp.uint32),
        compiler_params=_cparams("arbitrary"),
        name="scatter_rows",
    )(zero_list, dest.reshape((Ta + Tb) * TOP_K), xa, xb)


COMBINE_PARTS = 2


def _combine_kernel(dest_ref, x2_ref, gate_ref, g_ref, out_ref, y_ref, buf, sems, *, col_tile):
    tm, D = x2_ref.shape
    hw = col_tile // 2

    def unpack(w):
        lo = pltpu.bitcast(w << 16, F32)
        hi = pltpu.bitcast(w & jnp.uint32(0xFFFF0000), F32)
        parts = []
        for j in range(D // col_tile):
            parts += [lo[:, j * hw:(j + 1) * hw], hi[:, j * hw:(j + 1) * hw]]
        return jnp.concatenate(parts, axis=1)

    rows_per_part = tm // COMBINE_PARTS

    def issue(r, part):
        for k in range(TOP_K):
            pltpu.make_async_copy(out_ref.at[pl.ds(dest_ref[r * TOP_K + k], 1), :], buf.at[k, pl.ds(r, 1), :],
                                  sems.at[part]).start()
        return part

    for part in range(COMBINE_PARTS):
        lax.fori_loop(part * rows_per_part, (part + 1) * rows_per_part, issue, part, unroll=8)
    for part in range(COMBINE_PARTS):
        rows = pl.ds(part * rows_per_part, rows_per_part)
        for k in range(TOP_K):
            pltpu.make_async_copy(out_ref.at[pl.ds(0, rows_per_part), :], buf.at[k, rows, :], sems.at[part]).wait()
        gate = gate_ref[rows, :]
        acc = x2_ref[rows, :]
        for k in range(TOP_K):
            acc = acc + gate[:, k:k + 1] * unpack(buf[k, rows, :])
        y_ref[rows, :] = _rms(acc, g_ref[...])


def _combine(x2, gate, dest, out, final_g, tm, col_tile):
    T, D = x2.shape
    row = lambda w: pl.BlockSpec((tm, w), lambda i: (i, 0))
    return pl.pallas_call(
        functools.partial(_combine_kernel, col_tile=col_tile),
        grid=(T // tm,),
        in_specs=[pl.BlockSpec((tm * TOP_K,), lambda i: (i,), memory_space=pltpu.SMEM),
                  row(D), row(LANES), _resident((1, D)), pl.BlockSpec(memory_space=pl.ANY)],
        out_specs=row(D),
        out_shape=jax.ShapeDtypeStruct((T, D), F32),
        scratch_shapes=[pltpu.VMEM((TOP_K, tm, D // 2), jnp.uint32), pltpu.SemaphoreType.DMA((COMBINE_PARTS,))],
        compiler_params=_cparams("arbitrary"),
        name="combine_norm",
    )(dest.reshape(T * TOP_K), x2, gate, final_g.reshape(1, D), out)


def _tile(n, pref):
    return pref if n % pref == 0 else n


def _mixers(x, past, lw):
    (norm_mix_g, w_in, b_igate, b_fgate, g_sb_out, g_ml_out) = lw
    B, S, D = x.shape
    T = B * S
    x2d = x.reshape(T, D)
    q, kf, kb, vf, vb, qm, km, vm, om, gt = _in_projection(x2d, norm_mix_g, w_in, b_igate, b_fgate, _tile(T, 256))
    if past is None:
        sb_o = _sb_prompt(q, kb, vb, g_sb_out, B, S, R=min(8, S // KEY_BLOCK))
        C0 = jnp.zeros((B, ML_HEADS, HEAD_DIM, ML_V_DIM), F32)
        n0 = jnp.zeros((B, ML_HEADS, HEAD_DIM), F32)
        m0 = jnp.zeros((B, ML_HEADS), F32)
        L = _tile(S, 256)
    else:
        cache_k, cache_v, C0, n0, m0 = past
        sb_o = _sb_sample(q, kb, vb, cache_k, cache_v, g_sb_out, B, S, cache_k.shape[1])
        L = S
    ml_o, C, n, m = _mlstm(qm, km, vm, om, gt, C0, n0, m0, g_ml_out, B, S, L)
    state = (kf.reshape(B, S, SB_HEADS, HEAD_DIM), vf.reshape(B, S, SB_HEADS, HEAD_DIM),
             C, n.reshape(B, ML_HEADS, HEAD_DIM), m.reshape(B, ML_HEADS))
    return (x2d, sb_o, ml_o), state


MOE_ROW_TILE = 512
MOE_COL_TILE = 1024
MOE_DOWN_COL_TILE = 2048
COMBINE_TILE = 256


def kernel(x_prompt, x_sample, cache_k, cache_v, state_C, state_n, state_m, norm_mix_g, w_in, b_igate, b_fgate,
           g_sb_out, g_ml_out, w_out, norm_ffn_g, w_router, b_router, w_gate_up, b_gate_up, w_down, b_down,
           final_norm_g):
    assert w_in.shape[0] == 1, "single-layer trunk"
    lw = (norm_mix_g[0], w_in[0], b_igate[0], b_fgate[0], g_sb_out[0], g_ml_out[0])
    E = w_gate_up.shape[1]
    D = x_prompt.shape[-1]
    groups = [_mixers(x_prompt, None, lw),
              _mixers(x_sample, (cache_k[0], cache_v[0], state_C[0], state_n[0], state_m[0]), lw)]

    counts = jnp.zeros((1, LANES), F32)
    routed = []
    for (x2d, sb_o, ml_o), _ in groups:
        x2, xn2, eid, gate, rank, counts = _outproj_router(x2d, sb_o, ml_o, w_out[0], norm_ffn_g[0], w_router[0],
                                                           b_router[0], counts, _tile(x2d.shape[0], 512))
        routed.append((x2, xn2, eid, gate, rank))
    tm = MOE_ROW_TILE
    cnt = counts[0, :N_EXPERTS].astype(I32)
    padded = (cnt + tm - 1) // tm * tm
    pstart = jnp.cumsum(padded) - padded
    n_assign = sum(r[0].shape[0] for r in routed) * TOP_K
    n_rows = (-(-n_assign // tm) + N_EXPERTS) * tm
    dests = [pstart[eid[:, :TOP_K]] + rank[:, :TOP_K] for (_, _, eid, _, rank) in routed]

    tpe = padded // tm
    n_tiles = n_rows // tm
    tile_ids = jnp.arange(n_tiles, dtype=I32)
    last_of_expert = jnp.any((tile_ids[:, None] == (jnp.cumsum(tpe) - 1)[None, :]) & (tpe[None, :] > 0), axis=1)
    needs_zero = last_of_expert | (tile_ids >= jnp.sum(tpe))
    zero_list = jnp.concatenate([jnp.sum(needs_zero.astype(I32))[None],
                                 jnp.nonzero(needs_zero, size=n_tiles, fill_value=0)[0].astype(I32)])
    xs = _scatter_rows(routed[0][1], routed[1][1], jnp.concatenate(dests, axis=0), zero_list, n_rows,
                       _tile(routed[1][1].shape[0], 512), tm)
    out = _expert_ffn(xs, cnt, w_gate_up[0], b_gate_up[0].reshape(E, 1, -1), w_down[0],
                      b_down[0].reshape(E, 1, -1), tm, MOE_COL_TILE, MOE_DOWN_COL_TILE)
    ys = [_combine(x2, gate, dest, out, final_norm_g, _tile(x2.shape[0], COMBINE_TILE), MOE_DOWN_COL_TILE)
          for (x2, _, _, gate, _), dest in zip(routed, dests)]

    (kp, vp, Cp, np_, mp), (ks, vs, Cs, ns, ms) = groups[0][1], groups[1][1]
    return (ys[0].reshape(x_prompt.shape), ys[1].reshape(x_sample.shape),
            kp[None], vp[None], Cp[None], np_[None], mp[None], ks[None], vs[None], Cs[None], ns[None], ms[None])
```

```python
import functools

import jax
import jax.numpy as jnp
from jax import lax
from jax.experimental import pallas as pl
from jax.experimental.pallas import tpu as pltpu

F32 = jnp.float32
BF16 = jnp.bfloat16
I32 = jnp.int32

EPS = 1e-6
SB_HEADS = 8
HEAD_DIM = 128
ML_HEADS = 4
ML_V_DIM = 256
N_EXPERTS = 32
TOP_K = 4
SWIGLU_ALPHA = 1.702
SWIGLU_LIMIT = 7.0
LANES = 128
KEY_BLOCK = 128
NEG_BIG = -1e30
EXP_ZERO_BELOW = -105.0
VMEM_LIMIT = 56 * 1024 * 1024


def _cparams(*sem, vmem_limit=VMEM_LIMIT):
    return pltpu.CompilerParams(dimension_semantics=sem, vmem_limit_bytes=vmem_limit)


def _resident(shape):
    nd = len(shape)
    return pl.BlockSpec(shape, lambda *_: (0,) * nd, pipeline_mode=pl.Buffered(1))


def _rms(x, g):
    return x * lax.rsqrt(jnp.mean(x * x, axis=-1, keepdims=True) + EPS) * g


def _log_sigmoid(z):
    return jnp.minimum(z, 0.0) - jnp.log(1.0 + jnp.exp(-jnp.abs(z)))


def _split2(x):
    hi = x.astype(BF16)
    lo = (x - hi.astype(F32)).astype(BF16)
    return hi, lo


def _split3(x):
    h1 = x.astype(BF16)
    r = x - h1.astype(F32)
    h2 = r.astype(BF16)
    h3 = (r - h2.astype(F32)).astype(BF16)
    return h1, h2, h3


def _mm(a, b):
    return jnp.dot(a, b, preferred_element_type=F32)


def _inproj_kernel(x_ref, g_ref, wq_ref, wk_ref, wv_ref, wqm_ref, wkm_ref, wvm_ref, wom_ref,
                   wg_ref, bg_ref,
                   q_ref, kf_ref, kb_ref, vf_ref, vb_ref, qm_ref, km_ref, vm_ref, om_ref, gt_ref):
    xn = _rms(x_ref[...], g_ref[...]).astype(BF16)
    q_ref[...] = (_mm(xn, wq_ref[...]) * (HEAD_DIM ** -0.5)).astype(BF16)
    k = _mm(xn, wk_ref[...])
    kf_ref[...] = k
    kb_ref[...] = k.astype(BF16)
    v = _mm(xn, wv_ref[...])
    vf_ref[...] = v
    vb_ref[...] = v.astype(BF16)
    qm_ref[...] = _mm(xn, wqm_ref[...]).astype(BF16)
    km_ref[...] = (_mm(xn, wkm_ref[...]) * (HEAD_DIM ** -0.5)).astype(BF16)
    vm_ref[...] = _mm(xn, wvm_ref[...]).astype(BF16)
    om_ref[...] = _mm(xn, wom_ref[...])
    gpre = _mm(xn, wg_ref[...]) + bg_ref[...]
    lane = lax.broadcasted_iota(I32, gpre.shape, 1)
    is_f = (lane >= ML_HEADS) & (lane < 2 * ML_HEADS)
    gt_ref[...] = jnp.where(is_f, _log_sigmoid(gpre), gpre)


def _in_projection(x2d, norm_g, w_in, b_igate, b_fgate, tm):
    T, D = x2d.shape
    sbw = SB_HEADS * HEAD_DIM
    mqk = ML_HEADS * HEAD_DIM
    mlw = ML_HEADS * ML_V_DIM
    w_bf = w_in.astype(BF16)
    o = 0
    w_specs = []
    for width in (sbw, sbw, sbw, mqk, mqk, mlw, mlw):
        assert o % width == 0
        w_specs.append(pl.BlockSpec((D, width), functools.partial(lambda blk, i: (0, blk), o // width),
                                    pipeline_mode=pl.Buffered(1)))
        o += width
    wg = jnp.zeros((D, LANES), F32).at[:, :2 * ML_HEADS].set(w_in[:, o:o + 2 * ML_HEADS]).astype(BF16)
    bg = jnp.zeros((1, LANES), F32).at[0, :ML_HEADS].set(b_igate).at[0, ML_HEADS:2 * ML_HEADS].set(b_fgate)
    row = lambda w: pl.BlockSpec((tm, w), lambda i: (i, 0))
    out_widths = (sbw, sbw, sbw, sbw, sbw, mqk, mqk, mlw, mlw, LANES)
    out_dtypes = (BF16, F32, BF16, F32, BF16, BF16, BF16, BF16, F32, F32)
    return pl.pallas_call(
        _inproj_kernel,
        grid=(T // tm,),
        in_specs=[row(D), _resident((1, D))] + w_specs + [_resident(wg.shape), _resident(bg.shape)],
        out_specs=[row(w) for w in out_widths],
        out_shape=[jax.ShapeDtypeStruct((T, w), dt) for w, dt in zip(out_widths, out_dtypes)],
        compiler_params=_cparams("parallel"),
        name="in_projection",
    )(x2d, norm_g.reshape(1, D), *([w_bf] * len(w_specs)), wg, bg)


def _suffix_matrix():
    j = lax.broadcasted_iota(I32, (KEY_BLOCK, 2 * KEY_BLOCK), 0)
    c = lax.broadcasted_iota(I32, (KEY_BLOCK, 2 * KEY_BLOCK), 1)
    return jnp.where((c >= KEY_BLOCK) | (j > c), 1.0, 0.0).astype(BF16)


def _sb_step(q, k, v, carry, umat, mask):
    R, bq, _ = q.shape
    z = jnp.einsum("rqd,rkd->rqk", q, k, preferred_element_type=F32)
    lp = jnp.log(1.0 + jnp.exp(-jnp.abs(z)))
    log_beta = jnp.minimum(z, 0.0) - lp
    log_stay = log_beta - z
    if mask is not None:
        log_stay = jnp.where(mask, log_stay, 0.0)
    hi, lo = _split2(log_stay)
    st = _mm(hi.reshape(R * bq, KEY_BLOCK), umat) + _mm(lo.reshape(R * bq, KEY_BLOCK), umat)
    st = st.reshape(R, bq, 2 * KEY_BLOCK)
    w = jnp.exp(log_beta + st[:, :, :KEY_BLOCK] + carry)
    if mask is not None:
        w = jnp.where(mask, w, 0.0)
    pv = jnp.einsum("rqk,rkd->rqd", w.astype(BF16), v, preferred_element_type=F32)
    return pv, carry + st[:, :, KEY_BLOCK:]


def _sb_prompt_kernel(q_ref, k_ref, v_ref, g_ref, o_ref, acc_ref, carry_ref, *, R):
    qi = pl.program_id(2)
    blk0 = qi * R
    q = q_ref[...].reshape(R, KEY_BLOCK, HEAD_DIM)
    umat = _suffix_matrix()
    shape3 = (R, KEY_BLOCK, KEY_BLOCK)
    t_io = lax.broadcasted_iota(I32, shape3, 1)
    s_io = lax.broadcasted_iota(I32, shape3, 2)
    r_io = lax.broadcasted_iota(I32, shape3, 0)

    def load(ref, d):
        return jnp.stack([ref[pl.ds(pl.multiple_of(jnp.maximum(blk0 + r - d, 0) * KEY_BLOCK, KEY_BLOCK),
                                    KEY_BLOCK), :] for r in range(R)])

    def penalty(d_next):
        return jnp.where(r_io < d_next - blk0, NEG_BIG, 0.0)

    pv, carry = _sb_step(q, load(k_ref, 0), load(v_ref, 0), jnp.zeros(shape3, F32), umat, s_io < t_io)
    acc_ref[...] = pv
    carry = carry + penalty(1)
    carry_ref[...] = carry

    def cond(state):
        d, mx = state
        return (d < blk0 + R) & (mx > EXP_ZERO_BELOW)

    def body(state):
        d, _ = state
        pv, carry = _sb_step(q, load(k_ref, d), load(v_ref, d), carry_ref[...], umat, None)
        acc_ref[...] += pv
        carry = carry + penalty(d + 1)
        carry_ref[...] = carry
        return d + 1, jnp.max(carry)

    lax.while_loop(cond, body, (jnp.int32(1), jnp.max(carry)))
    a = acc_ref[...]
    out = a * lax.rsqrt(jnp.mean(a * a, axis=-1, keepdims=True) + EPS) * g_ref[...]
    o_ref[...] = out.reshape(R * KEY_BLOCK, HEAD_DIM).astype(BF16)


def _sb_prompt(q, k, v, g_sb, B, S, R):
    tq = R * KEY_BLOCK
    nq = S // tq
    return pl.pallas_call(
        functools.partial(_sb_prompt_kernel, R=R),
        grid=(B, SB_HEADS, nq),
        in_specs=[pl.BlockSpec((tq, HEAD_DIM), lambda b, h, i: (b * nq + i, h)),
                  pl.BlockSpec((S, HEAD_DIM), lambda b, h, i: (b, h)),
                  pl.BlockSpec((S, HEAD_DIM), lambda b, h, i: (b, h)),
                  pl.BlockSpec((1, HEAD_DIM), lambda b, h, i: (0, h))],
        out_specs=pl.BlockSpec((tq, HEAD_DIM), lambda b, h, i: (b * nq + i, h)),
        out_shape=jax.ShapeDtypeStruct((B * S, SB_HEADS * HEAD_DIM), BF16),
        scratch_shapes=[pltpu.VMEM((R, KEY_BLOCK, HEAD_DIM), F32),
                        pltpu.VMEM((R, KEY_BLOCK, KEY_BLOCK), F32)],
        compiler_params=_cparams("parallel", "parallel", "parallel"),
        name="sb_prompt",
    )(q, k, v, g_sb.reshape(1, -1))


def _sb_sample_kernel(q_ref, kn_ref, vn_ref, ck_ref, cv_ref, g_ref, o_ref, acc_ref, carry_ref, *, S, P):
    H = SB_HEADS
    hs = lambda h: slice(h * HEAD_DIM, (h + 1) * HEAD_DIM)
    umat = _suffix_matrix()
    q = jnp.stack([q_ref[:, hs(h)] for h in range(H)])
    pad = jnp.zeros((KEY_BLOCK - S, HEAD_DIM), BF16)

    def new_keys(ref):
        return jnp.stack([jnp.concatenate([ref[:, hs(h)], pad], axis=0) for h in range(H)])

    def past_keys(ref, j):
        start = pl.multiple_of(j * KEY_BLOCK * H, KEY_BLOCK * H)
        return jnp.stack([ref[pl.ds(start + h, KEY_BLOCK, stride=H), :].astype(BF16) for h in range(H)])

    shape3 = (H, S, KEY_BLOCK)
    t_io = lax.broadcasted_iota(I32, shape3, 1)
    s_io = lax.broadcasted_iota(I32, shape3, 2)
    pv, carry = _sb_step(q, new_keys(kn_ref), new_keys(vn_ref), jnp.zeros(shape3, F32), umat, s_io < t_io)
    acc_ref[...] = pv
    carry_ref[...] = carry

    def cond(state):
        j, mx = state
        return (j >= 0) & (mx > EXP_ZERO_BELOW)

    def body(state):
        j, _ = state
        pv, carry = _sb_step(q, past_keys(ck_ref, j), past_keys(cv_ref, j), carry_ref[...], umat, None)
        acc_ref[...] += pv
        carry_ref[...] = carry
        return j - 1, jnp.max(carry)

    lax.while_loop(cond, body, (jnp.int32(P // KEY_BLOCK - 1), jnp.max(carry)))
    a = acc_ref[...]
    a = a * lax.rsqrt(jnp.mean(a * a, axis=-1, keepdims=True) + EPS)
    for h in range(H):
        o_ref[:, hs(h)] = (a[h] * g_ref[:, hs(h)]).astype(BF16)


def _sb_sample(q, kn, vn, cache_k, cache_v, g_sb, B, S, P):
    W = SB_HEADS * HEAD_DIM
    row = pl.BlockSpec((S, W), lambda b: (b, 0))
    past = pl.BlockSpec((P * SB_HEADS, HEAD_DIM), lambda b: (b, 0))
    return pl.pallas_call(
        functools.partial(_sb_sample_kernel, S=S, P=P),
        grid=(B,),
        in_specs=[row, row, row, past, past, pl.BlockSpec((1, W), lambda b: (0, 0))],
        out_specs=row,
        out_shape=jax.ShapeDtypeStruct((B * S, W), BF16),
        scratch_shapes=[pltpu.VMEM((SB_HEADS, S, HEAD_DIM), F32),
                        pltpu.VMEM((SB_HEADS, S, KEY_BLOCK), F32)],
        compiler_params=_cparams("parallel"),
        name="sb_sample",
    )(q, kn, vn, cache_k.reshape(B * P * SB_HEADS, HEAD_DIM), cache_v.reshape(B * P * SB_HEADS, HEAD_DIM),
      g_sb.reshape(1, W))


def _mlstm_kernel(q_ref, k_ref, v_ref, o_ref, gt_ref, c0_ref, n0_ref, m0_ref, g_ref,
                  out_ref, c_out_ref, n_out_ref, m_out_ref, cext_ref, m_ref, *, L, Lp):
    c = pl.program_id(1)
    H = ML_HEADS
    VW = ML_V_DIM + LANES
    lane_row = lax.broadcasted_iota(I32, (1, LANES), 1)
    onehot0 = jnp.where(lane_row == 0, 1.0, 0.0)

    @pl.when(c == 0)
    def _():
        for h in range(H):
            cext_ref[h] = jnp.concatenate([c0_ref[0, h], n0_ref[0, h] * onehot0], axis=1)
            m_ref[h] = jnp.broadcast_to(m0_ref[0, h], (8, LANES))

    def pad_rows(a, fill=0.0):
        if Lp == L:
            return a
        return jnp.concatenate([a, jnp.full((Lp - L, a.shape[1]), fill, a.dtype)], axis=0)

    gt = gt_ref[...]
    lane = lax.broadcasted_iota(I32, (Lp, LANES), 1)
    if Lp != L:
        gt = jnp.concatenate([gt, jnp.broadcast_to(jnp.where(lane_row < H, NEG_BIG, 0.0), (Lp - L, LANES))], axis=0)
    lf = jnp.where((lane >= H) & (lane < 2 * H), gt, 0.0)
    ti = lax.broadcasted_iota(I32, (Lp, Lp), 0)
    si = lax.broadcasted_iota(I32, (Lp, Lp), 1)
    causal = si <= ti
    tri = jnp.where(causal, 1.0, 0.0).astype(BF16)
    bc = sum(_mm(tri, p) for p in _split3(lf))
    bc_t = bc.T
    gt_t = gt.T
    ones_blk = jnp.broadcast_to(onehot0, (Lp, LANES)).astype(BF16)

    for h in range(H):
        b_col = bc[:, H + h:H + h + 1]
        b_row = bc_t[H + h:H + h + 1, :]
        i_col = gt[:, h:h + 1]
        i_row = gt_t[h:h + 1, :]
        m_prev = m_ref[h][0:1, 0:1]
        qh = pad_rows(q_ref[:, h * HEAD_DIM:(h + 1) * HEAD_DIM])
        kh = pad_rows(k_ref[:, h * HEAD_DIM:(h + 1) * HEAD_DIM])
        vh = pad_rows(v_ref[:, h * ML_V_DIM:(h + 1) * ML_V_DIM])
        vext = jnp.concatenate([vh, ones_blk], axis=1)
        cext = cext_ref[h]

        log_d = jnp.where(causal, b_col - b_row + i_row, NEG_BIG)
        log_inter = b_col + m_prev
        m_row = jnp.maximum(log_inter, jnp.max(log_d, axis=1, keepdims=True))
        dmat = jnp.exp(log_d - m_row)
        s = lax.dot_general(qh, kh, (((1,), (1,)), ((), ())), preferred_element_type=F32) * dmat
        inter = jnp.exp(log_inter - m_row)
        num = _mm(s.astype(BF16), vext) + inter * _mm(qh, cext.astype(BF16))
        den = num[:, ML_V_DIM:ML_V_DIM + 1]
        hh = num[:L, :ML_V_DIM] / jnp.maximum(jnp.abs(den), jnp.exp(-m_row))[:L]
        hn = hh * lax.rsqrt(jnp.mean(hh * hh, axis=-1, keepdims=True) + EPS)
        cols = slice(h * ML_V_DIM, (h + 1) * ML_V_DIM)
        ogate = 1.0 / (1.0 + jnp.exp(-o_ref[:, cols]))
        out_ref[:, cols] = (ogate * (hn * g_ref[:, cols])).astype(BF16)

        b_last = b_col[Lp - 1:Lp, :]
        log_w = b_last - b_col + i_col
        m_new = jnp.maximum(b_last + m_prev, jnp.max(log_w, axis=0, keepdims=True))
        wk = jnp.exp(log_w - m_new)
        decay = jnp.exp(b_last + m_prev - m_new)
        upd = lax.dot_general(kh, (wk * vext.astype(F32)).astype(BF16), (((0,), (0,)), ((), ())),
                              preferred_element_type=F32)
        cnew = decay * cext + upd
        cext_ref[h] = cnew
        m_ref[h] = jnp.broadcast_to(m_new, (8, LANES))

    @pl.when(c == pl.num_programs(1) - 1)
    def _():
        for h in range(H):
            cf = cext_ref[h]
            c_out_ref[0, h] = cf[:, :ML_V_DIM]
            n_out_ref[0, h] = cf[:, ML_V_DIM:ML_V_DIM + 1]
            m_out_ref[0, h] = m_ref[h][0:1, 0:1]


def _mlstm(qm, km, vm, om, gt, C0, n0, m0, g_ml, B, S, L):
    nc = S // L
    Lp = max(L, LANES)
    H = ML_HEADS
    VW = ML_V_DIM + LANES
    row = lambda w: pl.BlockSpec((L, w), lambda b, c: (b * nc + c, 0))
    st = lambda *tail: pl.BlockSpec((1, H) + tail, lambda b, c: (b, 0, 0, 0))
    return pl.pallas_call(
        functools.partial(_mlstm_kernel, L=L, Lp=Lp),
        grid=(B, nc),
        in_specs=[row(H * HEAD_DIM), row(H * HEAD_DIM), row(H * ML_V_DIM), row(H * ML_V_DIM), row(LANES),
                  st(HEAD_DIM, ML_V_DIM), st(HEAD_DIM, 1), st(1, 1),
                  pl.BlockSpec((1, H * ML_V_DIM), lambda b, c: (0, 0))],
        out_specs=[row(H * ML_V_DIM), st(HEAD_DIM, ML_V_DIM), st(HEAD_DIM, 1), st(1, 1)],
        out_shape=[jax.ShapeDtypeStruct((B * S, H * ML_V_DIM), BF16),
                   jax.ShapeDtypeStruct((B, H, HEAD_DIM, ML_V_DIM), F32),
                   jax.ShapeDtypeStruct((B, H, HEAD_DIM, 1), F32),
                   jax.ShapeDtypeStruct((B, H, 1, 1), F32)],
        scratch_shapes=[pltpu.VMEM((H, HEAD_DIM, VW), F32), pltpu.VMEM((H, 8, LANES), F32)],
        compiler_params=_cparams("parallel", "arbitrary"),
        name="mlstm",
    )(qm, km, vm, om, gt, C0, n0.reshape(B, H, HEAD_DIM, 1), m0.reshape(B, H, 1, 1), g_ml.reshape(1, -1))


def _pack_halves(lo, hi):
    lo_bits = pltpu.bitcast(lo.astype(BF16).astype(F32), jnp.uint32) >> 16
    hi_bits = pltpu.bitcast(hi.astype(BF16).astype(F32), jnp.uint32) & jnp.uint32(0xFFFF0000)
    return lo_bits | hi_bits


def _unpack_halves(w):
    lo = pltpu.bitcast(w << 16, F32).astype(BF16)
    hi = pltpu.bitcast(w & jnp.uint32(0xFFFF0000), F32).astype(BF16)
    return lo, hi


def _outproj_router_kernel(x_ref, sb_ref, ml_ref, wos_ref, wom_ref, g_ref, wr_hi_ref, wr_lo_ref, br_ref, c0_ref,
                           x2_ref, xn_ref, eid_ref, gate_ref, rank_ref, cnt_ref, carry_ref):
    i = pl.program_id(0)

    @pl.when(i == 0)
    def _():
        carry_ref[...] = c0_ref[...]

    x2 = x_ref[...] + _mm(sb_ref[...], wos_ref[...]) + _mm(ml_ref[...], wom_ref[...])
    x2_ref[...] = x2
    xn = _rms(x2, g_ref[...])
    half = xn.shape[1] // 2
    xn_ref[...] = _pack_halves(xn[:, :half], xn[:, half:])
    hi, lo = _split2(xn)
    logits = _mm(hi, wr_hi_ref[...]) + _mm(lo, wr_hi_ref[...]) + _mm(hi, wr_lo_ref[...]) + br_ref[...]
    tm = logits.shape[0]
    lane = lax.broadcasted_iota(I32, (tm, LANES), 1)
    vals, ids = [], []
    cnt = jnp.zeros((tm, LANES), F32)
    for _ in range(TOP_K):
        mx = jnp.max(logits, axis=-1, keepdims=True)
        idx = jnp.min(jnp.where(logits == mx, lane, LANES), axis=-1, keepdims=True)
        sel = lane == idx
        vals.append(mx)
        ids.append(idx)
        logits = jnp.where(sel, -jnp.inf, logits)
        cnt = cnt + jnp.where(sel, 1.0, 0.0)
    es = [jnp.exp(v - vals[0]) for v in vals]
    inv = 1.0 / sum(es)
    ti = lax.broadcasted_iota(I32, (tm, tm), 0)
    si = lax.broadcasted_iota(I32, (tm, tm), 1)
    before = _mm(jnp.where(si < ti, 1.0, 0.0).astype(BF16), cnt.astype(BF16)) + carry_ref[...]
    eid_o = jnp.zeros((tm, LANES), I32)
    gate_o = jnp.zeros((tm, LANES), F32)
    rank_o = jnp.zeros((tm, LANES), I32)
    for k in range(TOP_K):
        rk = jnp.sum(jnp.where(lane == ids[k], before, 0.0), axis=-1, keepdims=True)
        eid_o = jnp.where(lane == k, ids[k], eid_o)
        gate_o = jnp.where(lane == k, es[k] * inv, gate_o)
        rank_o = jnp.where(lane == k, rk.astype(I32), rank_o)
    eid_ref[...] = eid_o
    gate_ref[...] = gate_o
    rank_ref[...] = rank_o
    carry_ref[...] += jnp.sum(cnt, axis=0, keepdims=True)
    cnt_ref[...] = carry_ref[...]


def _outproj_router(x2d, sb_o, ml_o, w_out, norm_g, w_router, b_router, counts_in, tm):
    T, D = x2d.shape
    W = sb_o.shape[1]
    assert w_out.shape[0] == 2 * W
    w_bf = w_out.astype(BF16)
    half_w = lambda blk: pl.BlockSpec((W, D), lambda i: (blk, 0), pipeline_mode=pl.Buffered(1))
    wr = jnp.zeros((D, LANES), F32).at[:, :N_EXPERTS].set(w_router)
    wr_hi = wr.astype(BF16)
    wr_lo = (wr - wr_hi.astype(F32)).astype(BF16)
    br = jnp.full((1, LANES), NEG_BIG, F32).at[0, :N_EXPERTS].set(b_router)
    row = lambda w: pl.BlockSpec((tm, w), lambda i: (i, 0))
    return pl.pallas_call(
        _outproj_router_kernel,
        grid=(T // tm,),
        in_specs=[row(D), row(W), row(W), half_w(0), half_w(1),
                  _resident((1, D)), _resident(wr_hi.shape), _resident(wr_lo.shape), _resident(br.shape),
                  _resident((1, LANES))],
        out_specs=[row(D), row(D // 2), row(LANES), row(LANES), row(LANES),
                   pl.BlockSpec((1, LANES), lambda i: (0, 0))],
        out_shape=[jax.ShapeDtypeStruct((T, D), F32), jax.ShapeDtypeStruct((T, D // 2), jnp.uint32),
                   jax.ShapeDtypeStruct((T, LANES), I32), jax.ShapeDtypeStruct((T, LANES), F32),
                   jax.ShapeDtypeStruct((T, LANES), I32), jax.ShapeDtypeStruct((1, LANES), F32)],
        scratch_shapes=[pltpu.VMEM((1, LANES), F32)],
        compiler_params=_cparams("arbitrary"),
        name="outproj_router",
    )(x2d, sb_o, ml_o, w_bf, w_bf, norm_g.reshape(1, D), wr_hi, wr_lo, br, counts_in)


ROW_BLOCK = 128
IT_TILE, IT_COL, IT_EXPERT, IT_FIRST, IT_BLOCKS, IT_NEXT_EXPERT, IT_NEXT_COL = range(7)


def _segment_weights(it_ref, w, copies, cast):
    @pl.when(it_ref[IT_FIRST, w] == 1)
    def _():
        @pl.when(w == 0)
        def _():
            for c in copies(it_ref[IT_EXPERT, w], it_ref[IT_COL, w]):
                c.start()

        for c in copies(it_ref[IT_EXPERT, w], it_ref[IT_COL, w]):
            c.wait()
        cast()

        @pl.when(it_ref[IT_NEXT_EXPERT, w] >= 0)
        def _():
            for c in copies(it_ref[IT_NEXT_EXPERT, w], it_ref[IT_NEXT_COL, w]):
                c.start()


def _for_filled_rows(blocks, tm, compute):
    for lvl in range(1, tm // ROW_BLOCK + 1):
        @pl.when(blocks == lvl)
        def _():
            compute(lvl * ROW_BLOCK)


def _gate_up_kernel(it_ref, n_ref, x_ref, w_hbm, bg_ref, bl_ref, act_ref, stage, wg_s, wl_s, x_s, sems, *, nj):
    w = pl.program_id(0)
    tm, half = x_ref.shape
    tn = act_ref.shape[1]

    def copies(e, j):
        return [pltpu.make_async_copy(w_hbm.at[e, :, pl.ds(pl.multiple_of((h * nj + j) * tn, tn), tn)],
                                      stage.at[h], sems.at[h]) for h in range(2)]

    def cast():
        wg_s[...] = stage[0].astype(BF16)
        wl_s[...] = stage[1].astype(BF16)

    @pl.when(w < n_ref[0])
    def _():
        _segment_weights(it_ref, w, copies, cast)

        def compute(rows):
            lo, hi = _unpack_halves(x_ref[:rows, :])
            x_s[:rows, :half] = lo
            x_s[:rows, half:] = hi
            x = x_s[:rows, :]
            glu = jnp.minimum(_mm(x, wg_s[...]) + bg_ref[0], SWIGLU_LIMIT)
            lin = jnp.clip(_mm(x, wl_s[...]) + bl_ref[0], -SWIGLU_LIMIT, SWIGLU_LIMIT)
            act = glu * (1.0 / (1.0 + jnp.exp(-SWIGLU_ALPHA * glu))) * (lin + 1.0)
            act_ref[:rows, :] = act.astype(BF16)
            if rows < tm:
                act_ref[rows:, :] = jnp.zeros((tm - rows, tn), BF16)

        _for_filled_rows(it_ref[IT_BLOCKS, w], tm, compute)

    @pl.when(w >= n_ref[0])
    def _():
        act_ref[...] = jnp.zeros_like(act_ref)


def _down_kernel(it_ref, n_ref, a_ref, w_hbm, b_ref, o_ref, stage, w_s, sem):
    w = pl.program_id(0)
    tm = a_ref.shape[0]
    tn = w_s.shape[1]

    def copies(e, j):
        return [pltpu.make_async_copy(w_hbm.at[e, :, pl.ds(pl.multiple_of(j * tn, tn), tn)], stage, sem)]

    def cast():
        w_s[...] = stage[...].astype(BF16)

    @pl.when(w < n_ref[0])
    def _():
        _segment_weights(it_ref, w, copies, cast)

        def compute(rows):
            out = _mm(a_ref[:rows, :], w_s[...]) + b_ref[0]
            o_ref[:rows, :] = _pack_halves(out[:, :tn // 2], out[:, tn // 2:])
            if rows < tm:
                o_ref[rows:, :] = jnp.zeros((tm - rows, tn // 2), jnp.uint32)

        _for_filled_rows(it_ref[IT_BLOCKS, w], tm, compute)

    @pl.when(w >= n_ref[0])
    def _():
        o_ref[...] = jnp.zeros_like(o_ref)


def _work_list(counts, tm, nj, n_tiles_max):
    tpe = (counts + tm - 1) // tm
    tile_start = jnp.cumsum(tpe) - tpe
    item_end = jnp.cumsum(tpe * nj)
    item_start = item_end - tpe * nj
    n_items = item_end[-1:]
    w = jnp.arange(n_tiles_max * nj, dtype=I32)
    wc = jnp.minimum(w, n_items[0] - 1)
    e = jnp.minimum(jnp.sum((wc[:, None] >= item_end[None, :]).astype(I32), axis=1), N_EXPERTS - 1)
    ids = jnp.where(tpe > 0, jnp.arange(N_EXPERTS, dtype=I32), N_EXPERTS)
    later = jnp.concatenate([lax.cummin(ids[::-1])[::-1][1:], jnp.full((1,), N_EXPERTS, I32)])
    next_nonempty = jnp.where(later < N_EXPERTS, later, -1)
    per_expert = jnp.stack([item_start, tpe, tile_start, counts, next_nonempty], axis=1).astype(F32)
    onehot = (e[:, None] == jnp.arange(N_EXPERTS, dtype=I32)[None, :]).astype(F32)
    looked = jnp.dot(onehot, per_expert, precision=lax.Precision.HIGHEST).astype(I32)
    item_start_e, tpe_e, tile_start_e, counts_e, next_nonempty_e = (looked[:, c] for c in range(5))
    local = wc - item_start_e
    t = jnp.maximum(tpe_e, 1)
    j = local // t
    il = local - j * t
    real = w < n_items[0]
    first = ((il == 0) & real).astype(I32)
    tail = jnp.maximum(w - n_items[0], 0)
    tile = jnp.where(real, tile_start_e + il, jnp.sum(tpe) + tail // nj)
    j = jnp.where(real, j, tail % nj)
    blocks = (jnp.clip(counts_e - il * tm, 1, tm) + ROW_BLOCK - 1) // ROW_BLOCK
    last_col = j + 1 >= nj
    next_e = jnp.where(last_col, next_nonempty_e, e)
    next_j = jnp.where(last_col, 0, j + 1)
    table = jnp.stack([tile, j, e, first, blocks, next_e, next_j]).astype(I32)
    return table, n_items.astype(I32)


def _expert_ffn(xs, counts, w_gu, b_gu, w_dn, b_dn, tm, tn, tn_down):
    n_rows, half = xs.shape
    D = 2 * half
    F = w_dn.shape[1]
    n_tiles = n_rows // tm
    nj = F // tn
    row_tile = lambda w, it, n: (it[IT_TILE, w], 0)
    out_tile = lambda w, it, n: (it[IT_TILE, w], it[IT_COL, w])
    act = pl.pallas_call(
        functools.partial(_gate_up_kernel, nj=nj),
        grid_spec=pltpu.PrefetchScalarGridSpec(
            num_scalar_prefetch=2,
            grid=(n_tiles * nj,),
            in_specs=[pl.BlockSpec((tm, half), row_tile),
                      pl.BlockSpec(memory_space=pl.ANY),
                      pl.BlockSpec((1, 1, tn), lambda w, it, n: (it[IT_EXPERT, w], 0, it[IT_COL, w])),
                      pl.BlockSpec((1, 1, tn), lambda w, it, n: (it[IT_EXPERT, w], 0, nj + it[IT_COL, w]))],
            out_specs=pl.BlockSpec((tm, tn), out_tile),
            scratch_shapes=[pltpu.VMEM((2, D, tn), F32), pltpu.VMEM((D, tn), BF16), pltpu.VMEM((D, tn), BF16),
                            pltpu.VMEM((tm, D), BF16), pltpu.SemaphoreType.DMA((2,))]),
        out_shape=jax.ShapeDtypeStruct((n_rows, F), BF16),
        compiler_params=_cparams("arbitrary"),
        name="expert_gate_up",
    )(*_work_list(counts, tm, nj, n_tiles), xs, w_gu, b_gu, b_gu)

    tn = tn_down
    njd = D // tn
    return pl.pallas_call(
        _down_kernel,
        grid_spec=pltpu.PrefetchScalarGridSpec(
            num_scalar_prefetch=2,
            grid=(n_tiles * njd,),
            in_specs=[pl.BlockSpec((tm, F), row_tile),
                      pl.BlockSpec(memory_space=pl.ANY),
                      pl.BlockSpec((1, 1, tn), lambda w, it, n: (it[IT_EXPERT, w], 0, it[IT_COL, w]))],
            out_specs=pl.BlockSpec((tm, tn // 2), out_tile),
            scratch_shapes=[pltpu.VMEM((F, tn), F32), pltpu.VMEM((F, tn), BF16), pltpu.SemaphoreType.DMA]),
        out_shape=jax.ShapeDtypeStruct((n_rows, D // 2), jnp.uint32),
        compiler_params=_cparams("arbitrary"),
        name="expert_down",
    )(*_work_list(counts, tm, njd, n_tiles), act, w_dn, b_dn)


def _scatter_kernel(zero_ref, dest_ref, xa_ref, xb_ref, xs_ref, zbuf, sem, zsem, *, tiles_a, slot_tile):
    i = pl.program_id(0)
    tm = xa_ref.shape[0]

    @pl.when(i == 0)
    def _():
        zbuf[...] = jnp.zeros_like(zbuf)

        def tile_copy(z):
            start = pl.multiple_of(zero_ref[1 + z] * slot_tile, slot_tile)
            return pltpu.make_async_copy(zbuf, xs_ref.at[pl.ds(start, slot_tile), :], zsem)

        def start(z, carry):
            tile_copy(z).start()
            return carry

        def wait(z, carry):
            tile_copy(z).wait()
            return carry

        lax.fori_loop(0, zero_ref[0], start, 0)
        lax.fori_loop(0, zero_ref[0], wait, 0)

    def scatter_from(x_ref):
        def issue(r, carry):
            for k in range(TOP_K):
                pltpu.make_async_copy(x_ref.at[pl.ds(r, 1), :], xs_ref.at[pl.ds(dest_ref[r * TOP_K + k], 1), :],
                                      sem).start(priority=k % 2)
            return carry

        lax.fori_loop(0, tm, issue, 0, unroll=8)
        for _ in range(TOP_K):
            pltpu.make_async_copy(x_ref, xs_ref.at[pl.ds(0, tm), :], sem).wait()

    @pl.when(i < tiles_a)
    def _():
        scatter_from(xa_ref)

    @pl.when(i >= tiles_a)
    def _():
        scatter_from(xb_ref)


def _scatter_rows(xa, xb, dest, zero_list, n_rows, tm, slot_tile):
    Ta, half = xa.shape
    Tb = xb.shape[0]
    tiles_a, tiles_b = Ta // tm, Tb // tm
    return pl.pallas_call(
        functools.partial(_scatter_kernel, tiles_a=tiles_a, slot_tile=slot_tile),
        grid_spec=pltpu.PrefetchScalarGridSpec(
            num_scalar_prefetch=1,
            grid=(tiles_a + tiles_b,),
            in_specs=[pl.BlockSpec((tm * TOP_K,), lambda i, z: (i,), memory_space=pltpu.SMEM),
                      pl.BlockSpec((tm, half), lambda i, z: (jnp.minimum(i, tiles_a - 1), 0)),
                      pl.BlockSpec((tm, half), lambda i, z: (jnp.maximum(i - tiles_a, 0), 0))],
            out_specs=pl.BlockSpec(memory_space=pl.ANY),
            scratch_shapes=[pltpu.VMEM((slot_tile, half), jnp.uint32), pltpu.SemaphoreType.DMA,
                            pltpu.SemaphoreType.DMA]),
        out_shape=jax.ShapeDtypeStruct((n_rows, half), jnp.uint32),
        compiler_params=_cparams("arbitrary"),
        name="scatter_rows",
    )(zero_list, dest.reshape((Ta + Tb) * TOP_K), xa, xb)


COMBINE_PARTS = 2


def _combine_kernel(dest_ref, x2_ref, gate_ref, g_ref, out_ref, y_ref, buf, sems, *, col_tile):
    tm, D = x2_ref.shape
    hw = col_tile // 2

    def unpack(w):
        lo = pltpu.bitcast(w << 16, F32)
        hi = pltpu.bitcast(w & jnp.uint32(0xFFFF0000), F32)
        parts = []
        for j in range(D // col_tile):
            parts += [lo[:, j * hw:(j + 1) * hw], hi[:, j * hw:(j + 1) * hw]]
        return jnp.concatenate(parts, axis=1)

    rows_per_part = tm // COMBINE_PARTS

    def issue(r, part):
        for k in range(TOP_K):
            pltpu.make_async_copy(out_ref.at[pl.ds(dest_ref[r * TOP_K + k], 1), :], buf.at[k, pl.ds(r, 1), :],
                                  sems.at[part]).start(priority=k % 2)
        return part

    for part in range(COMBINE_PARTS):
        lax.fori_loop(part * rows_per_part, (part + 1) * rows_per_part, issue, part, unroll=8)
    for part in range(COMBINE_PARTS):
        rows = pl.ds(part * rows_per_part, rows_per_part)
        for k in range(TOP_K):
            pltpu.make_async_copy(out_ref.at[pl.ds(0, rows_per_part), :], buf.at[k, rows, :], sems.at[part]).wait()
        gate = gate_ref[rows, :]
        acc = x2_ref[rows, :]
        for k in range(TOP_K):
            acc = acc + gate[:, k:k + 1] * unpack(buf[k, rows, :])
        y_ref[rows, :] = _rms(acc, g_ref[...])


def _combine(x2, gate, dest, out, final_g, tm, col_tile):
    T, D = x2.shape
    row = lambda w: pl.BlockSpec((tm, w), lambda i: (i, 0))
    return pl.pallas_call(
        functools.partial(_combine_kernel, col_tile=col_tile),
        grid=(T // tm,),
        in_specs=[pl.BlockSpec((tm * TOP_K,), lambda i: (i,), memory_space=pltpu.SMEM),
                  row(D), row(LANES), _resident((1, D)), pl.BlockSpec(memory_space=pl.ANY)],
        out_specs=row(D),
        out_shape=jax.ShapeDtypeStruct((T, D), F32),
        scratch_shapes=[pltpu.VMEM((TOP_K, tm, D // 2), jnp.uint32), pltpu.SemaphoreType.DMA((COMBINE_PARTS,))],
        compiler_params=_cparams("arbitrary"),
        name="combine_norm",
    )(dest.reshape(T * TOP_K), x2, gate, final_g.reshape(1, D), out)


def _tile(n, pref):
    return pref if n % pref == 0 else n


def _mixers(x, past, lw):
    (norm_mix_g, w_in, b_igate, b_fgate, g_sb_out, g_ml_out) = lw
    B, S, D = x.shape
    T = B * S
    x2d = x.reshape(T, D)
    q, kf, kb, vf, vb, qm, km, vm, om, gt = _in_projection(x2d, norm_mix_g, w_in, b_igate, b_fgate, _tile(T, 256))
    if past is None:
        sb_o = _sb_prompt(q, kb, vb, g_sb_out, B, S, R=min(8, S // KEY_BLOCK))
        C0 = jnp.zeros((B, ML_HEADS, HEAD_DIM, ML_V_DIM), F32)
        n0 = jnp.zeros((B, ML_HEADS, HEAD_DIM), F32)
        m0 = jnp.zeros((B, ML_HEADS), F32)
        L = _tile(S, 256)
    else:
        cache_k, cache_v, C0, n0, m0 = past
        sb_o = _sb_sample(q, kb, vb, cache_k, cache_v, g_sb_out, B, S, cache_k.shape[1])
        L = S
    ml_o, C, n, m = _mlstm(qm, km, vm, om, gt, C0, n0, m0, g_ml_out, B, S, L)
    state = (kf.reshape(B, S, SB_HEADS, HEAD_DIM), vf.reshape(B, S, SB_HEADS, HEAD_DIM),
             C, n.reshape(B, ML_HEADS, HEAD_DIM), m.reshape(B, ML_HEADS))
    return (x2d, sb_o, ml_o), state


MOE_ROW_TILE = 512
MOE_COL_TILE = 1024
MOE_DOWN_COL_TILE = 2048
COMBINE_TILE = 256


def kernel(x_prompt, x_sample, cache_k, cache_v, state_C, state_n, state_m, norm_mix_g, w_in, b_igate, b_fgate,
           g_sb_out, g_ml_out, w_out, norm_ffn_g, w_router, b_router, w_gate_up, b_gate_up, w_down, b_down,
           final_norm_g):
    assert w_in.shape[0] == 1, "single-layer trunk"
    lw = (norm_mix_g[0], w_in[0], b_igate[0], b_fgate[0], g_sb_out[0], g_ml_out[0])
    E = w_gate_up.shape[1]
    D = x_prompt.shape[-1]
    groups = [_mixers(x_prompt, None, lw),
              _mixers(x_sample, (cache_k[0], cache_v[0], state_C[0], state_n[0], state_m[0]), lw)]

    counts = jnp.zeros((1, LANES), F32)
    routed = []
    for (x2d, sb_o, ml_o), _ in groups:
        x2, xn2, eid, gate, rank, counts = _outproj_router(x2d, sb_o, ml_o, w_out[0], norm_ffn_g[0], w_router[0],
                                                           b_router[0], counts, _tile(x2d.shape[0], 512))
        routed.append((x2, xn2, eid, gate, rank))
    tm = MOE_ROW_TILE
    cnt = counts[0, :N_EXPERTS].astype(I32)
    padded = (cnt + tm - 1) // tm * tm
    pstart = jnp.cumsum(padded) - padded
    n_assign = sum(r[0].shape[0] for r in routed) * TOP_K
    n_rows = (-(-n_assign // tm) + N_EXPERTS) * tm
    dests = [pstart[eid[:, :TOP_K]] + rank[:, :TOP_K] for (_, _, eid, _, rank) in routed]

    tpe = padded // tm
    n_tiles = n_rows // tm
    tile_ids = jnp.arange(n_tiles, dtype=I32)
    last_of_expert = jnp.any((tile_ids[:, None] == (jnp.cumsum(tpe) - 1)[None, :]) & (tpe[None, :] > 0), axis=1)
    needs_zero = last_of_expert | (tile_ids >= jnp.sum(tpe))
    zero_list = jnp.concatenate([jnp.sum(needs_zero.astype(I32))[None],
                                 jnp.nonzero(needs_zero, size=n_tiles, fill_value=0)[0].astype(I32)])
    xs = _scatter_rows(routed[0][1], routed[1][1], jnp.concatenate(dests, axis=0), zero_list, n_rows,
                       _tile(routed[1][1].shape[0], 512), tm)
    out = _expert_ffn(xs, cnt, w_gate_up[0], b_gate_up[0].reshape(E, 1, -1), w_down[0],
                      b_down[0].reshape(E, 1, -1), tm, MOE_COL_TILE, MOE_DOWN_COL_TILE)
    ys = [_combine(x2, gate, dest, out, final_norm_g, _tile(x2.shape[0], COMBINE_TILE), MOE_DOWN_COL_TILE)
          for (x2, _, _, gate, _), dest in zip(routed, dests)]

    (kp, vp, Cp, np_, mp), (ks, vs, Cs, ns, ms) = groups[0][1], groups[1][1]
    return (ys[0].reshape(x_prompt.shape), ys[1].reshape(x_sample.shape),
            kp[None], vp[None], Cp[None], np_[None], mp[None], ks[None], vs[None], Cs[None], ns[None], ms[None])
```
